```python
import math
import jax
import jax.numpy as jnp
from jax import lax
import numpy as np

D_MODEL = 2048
BATCH = 16
SEQ = 2048
DEPTH = 2

GRID_W = 64
CTX_LEN = 256
RMS_EPS = 1e-6

BRANCH_W = D_MODEL // 2
N_BRANCH = 3

DA_HD = 64
DA_HEADS = BRANCH_W // (2 * DA_HD)
Q_BLOCK = 128
ROPE_THETA = 10000.0
ROPE_PAIRS_AXIS = DA_HD // 4

GM_CHUNK = 128
GM_GW = 128
GM_GROUPS = BRANCH_W // GM_GW

DN_HD = 128
DN_HEADS = BRANCH_W // DN_HD
DN_CHUNK = 64
DN_CONV = 5

N_EXPERTS = 32
TOP_K = 4
EXPERT_FF = D_MODEL // 2
SWIGLU_LIMIT = 7.0
SWIGLU_ALPHA = 1.702
MOE_BLOCK = 128

IN_SEGMENTS = (
    ('da_q', 2 * DA_HEADS * DA_HD),
    ('da_k', 2 * DA_HEADS * DA_HD),
    ('da_v', BRANCH_W),
    ('gm_u', BRANCH_W),
    ('gm_v', BRANCH_W),
    ('dn_q', BRANCH_W),
    ('dn_k', BRANCH_W),
    ('dn_v', BRANCH_W),
    ('dn_z', BRANCH_W),
    ('dn_beta', 2 * DN_HEADS),
    ('dn_a', 2 * DN_HEADS),
    ('gate', N_BRANCH * D_MODEL),
)
ALL_SEGMENTS = ('da_q', 'da_k', 'da_v', 'gm_u', 'gm_v', 'dn_q', 'dn_k', 'dn_v', 'dn_z', 'dn_beta', 'dn_a', 'gate')
CTX_STATE_SEGMENTS = ('da_k', 'da_v', 'dn_k', 'dn_v', 'dn_beta', 'dn_a')

kernel_name = 'hybrid_diffattn_gmlp_deltanet_moe_dit'


def rms_norm(x, g):
    xf = x.astype(jnp.float32)
    y = xf * lax.rsqrt(jnp.mean(xf * xf, axis=-1, keepdims=True) + RMS_EPS)
    return (y * g.astype(jnp.float32)).astype(x.dtype)


def layer_norm(x, g, b):
    xf = x.astype(jnp.float32)
    xc = xf - jnp.mean(xf, axis=-1, keepdims=True)
    var = jnp.mean(xc * xc, axis=-1, keepdims=True)
    return (xc * lax.rsqrt(var + RMS_EPS) * g.astype(jnp.float32) + b.astype(jnp.float32)).astype(x.dtype)


def adaln(x, g, shift, scale):
    return rms_norm(x, g) * (1 + scale) + shift


def modulation(cond, w_mod, b_mod, n_chunks):
    d = w_mod.shape[0]
    m = jax.nn.silu(cond) @ w_mod[:, :n_chunks * d] + b_mod[:n_chunks * d]
    return jnp.split(m, n_chunks, axis=-1)


def in_proj(h, w_in, names):
    offs, o = {}, 0
    for name, width in IN_SEGMENTS:
        offs[name] = (o, width)
        o += width
    if tuple(names) == ALL_SEGMENTS:
        w = w_in
    else:
        w = jnp.concatenate([w_in[:, offs[n][0]:offs[n][0] + offs[n][1]] for n in names], axis=1)
    y = h @ w
    out, o = {}, 0
    for n in names:
        out[n] = y[..., o:o + offs[n][1]]
        o += offs[n][1]
    return out


def axial_rope(n):
    rows = n // GRID_W
    row = jnp.repeat(jnp.arange(rows, dtype=jnp.float32), GRID_W)
    col = jnp.tile(jnp.arange(GRID_W, dtype=jnp.float32), rows)
    inv = ROPE_THETA ** (-jnp.arange(ROPE_PAIRS_AXIS, dtype=jnp.float32) / ROPE_PAIRS_AXIS)
    ang = jnp.concatenate([row[:, None] * inv, col[:, None] * inv], axis=-1)
    return jnp.cos(ang), jnp.sin(ang)


def apply_rope(x, cos, sin):
    half = x.shape[-1] // 2
    xf = x.astype(jnp.float32)
    x1, x2 = xf[..., :half], xf[..., half:]
    c = cos[None, :, None, None, :]
    s = sin[None, :, None, None, :]
    return jnp.concatenate([x1 * c - x2 * s, x1 * s + x2 * c], axis=-1).astype(x.dtype)


def diff_attn_core(q, k, v, lam):
    s = jnp.einsum('bqhmd,bkhmd->bhmqk', q, k).astype(jnp.float32) * (DA_HD ** -0.5)
    p = jax.nn.softmax(s, axis=-1)
    a = p[:, :, 0] - lam * p[:, :, 1]
    return jnp.einsum('bhqk,bkhe->bqhe', a.astype(v.dtype), v)


def diff_attention(pl, pc, cos, sin, lam_params, subln_g, layer_idx, ctx_out):
    b, n = pl['da_q'].shape[:2]
    nc = pc['da_k'].shape[1]
    lam_init = 0.8 - 0.6 * math.exp(-0.3 * layer_idx)
    lp = lam_params.astype(jnp.float32)
    lam = jnp.exp(jnp.sum(lp[0] * lp[1])) - jnp.exp(jnp.sum(lp[2] * lp[3])) + lam_init
    def heads(t, m):
        return t.reshape(b, m, DA_HEADS, 2, DA_HD)
    ql = apply_rope(heads(pl['da_q'], n), cos, sin)
    kl = apply_rope(heads(pl['da_k'], n), cos, sin)
    kc = heads(pc['da_k'], nc)
    vl = pl['da_v'].reshape(b, n, DA_HEADS, 2 * DA_HD)
    vc = pc['da_v'].reshape(b, nc, DA_HEADS, 2 * DA_HD)
    k_all = jnp.concatenate([kc, kl], axis=1)
    v_all = jnp.concatenate([vc, vl], axis=1)
    nb = n // Q_BLOCK
    q_blocks = jnp.moveaxis(ql.reshape(b, nb, Q_BLOCK, DA_HEADS, 2, DA_HD), 1, 0)
    o_blocks = lax.map(lambda qb: diff_attn_core(qb, k_all, v_all, lam), q_blocks)
    ol = jnp.moveaxis(o_blocks, 0, 1).reshape(b, n, DA_HEADS, 2 * DA_HD)
    def finish(o):
        return (rms_norm(o, subln_g) * (1.0 - lam_init)).reshape(o.shape[0], o.shape[1], DA_HEADS * 2 * DA_HD)
    yl = finish(ol)
    yc = finish(diff_attn_core(heads(pc['da_q'], nc), kc, vc, lam)) if ctx_out else None
    return yl, yc


def spatial_gating(p, ln_g, ln_b, ws, bs):
    u = jax.nn.gelu(p['gm_u'])
    v = layer_norm(jax.nn.gelu(p['gm_v']), ln_g, ln_b)
    b, n, _ = v.shape
    vb = v.reshape(b, n // GM_CHUNK, GM_CHUNK, GM_GROUPS, GM_GW)
    s = jnp.einsum('gpq,bnqgc->bnpgc', ws, vb) + jnp.swapaxes(bs, 0, 1)[:, :, None]
    return u * s.reshape(b, n, GM_GROUPS * GM_GW)


def short_conv(x, w):
    return lax.conv_general_dilated(
        x, w[:, None, :].astype(x.dtype), window_strides=(1,),
        padding=[(DN_CONV // 2, DN_CONV // 2)],
        dimension_numbers=('NWC', 'WIO', 'NWC'), feature_group_count=x.shape[-1])


def l2norm(x):
    return x * lax.rsqrt(jnp.sum(x * x, axis=-1, keepdims=True) + RMS_EPS)


def dn_inputs(p, conv_w, a_log, dt_bias, with_q):
    b, n, _ = p['dn_k'].shape
    def heads(t):
        return jnp.moveaxis(t.reshape(b, n, DN_HEADS, DN_HD), 2, 1).astype(jnp.float32)
    k = l2norm(heads(jax.nn.silu(short_conv(p['dn_k'], conv_w[1]))))
    v = heads(jax.nn.silu(short_conv(p['dn_v'], conv_w[2])))
    q = l2norm(heads(jax.nn.silu(short_conv(p['dn_q'], conv_w[0])))) * (DN_HD ** -0.5) if with_q else None
    beta = jax.nn.sigmoid(p['dn_beta'].astype(jnp.float32)).reshape(b, n, 2, DN_HEADS)
    a = p['dn_a'].astype(jnp.float32).reshape(b, n, 2, DN_HEADS) + dt_bias.astype(jnp.float32)
    g = -jnp.exp(a_log.astype(jnp.float32)) * jax.nn.softplus(a)
    return q, k, v, jnp.transpose(beta, (2, 0, 3, 1)), jnp.transpose(g, (2, 0, 3, 1))


def gated_delta_chunked(q, k, v, beta, g, s0):
    b, h, n, dk = k.shape
    dv = v.shape[-1]
    nc = n // DN_CHUNK
    def chunks(t):
        return t.reshape(b, h, nc, DN_CHUNK, *t.shape[3:])
    k, v, beta, g = chunks(k), chunks(v), chunks(beta), chunks(g)
    g = jnp.cumsum(g, axis=-1)
    pos = jnp.arange(DN_CHUNK)
    incl = pos[:, None] >= pos[None, :]
    strict = pos[:, None] > pos[None, :]
    decay = jnp.exp(jnp.where(incl, g[..., :, None] - g[..., None, :], -jnp.inf))
    k_beta = k * beta[..., None]
    lower = jnp.where(strict, jnp.einsum('bhnid,bhnjd->bhnij', k_beta, k) * decay, 0.0)
    rhs = jnp.concatenate([v * beta[..., None], k_beta * jnp.exp(g)[..., None]], axis=-1)
    sol = lax.linalg.triangular_solve(jnp.eye(DN_CHUNK, dtype=k.dtype) + lower, rhs,
                                      left_side=True, lower=True, unit_diagonal=True)
    u, w = sol[..., :dv], sol[..., dv:]
    k_tail = k * jnp.exp(g[..., -1:] - g)[..., None]
    chunk_decay = jnp.exp(g[..., -1])
    def lead(t):
        return jnp.moveaxis(t, 2, 0)
    def advance(state, u_i, w_i, kt_i, cd_i):
        v_new = u_i - w_i @ state
        new_state = state * cd_i[..., None, None] + jnp.einsum('bhcd,bhce->bhde', kt_i, v_new)
        return new_state, v_new
    if q is None:
        def state_step(state, xs):
            new_state, _ = advance(state, *xs)
            return new_state, None
        s_final, _ = lax.scan(state_step, s0, (lead(u), lead(w), lead(k_tail), lead(chunk_decay)))
        return None, s_final
    q = chunks(q)
    q_intra = jnp.where(incl, jnp.einsum('bhnid,bhnjd->bhnij', q, k) * decay, 0.0)
    q_dec = q * jnp.exp(g)[..., None]
    def out_step(state, xs):
        u_i, w_i, kt_i, cd_i, qd_i, qi_i = xs
        new_state, v_new = advance(state, u_i, w_i, kt_i, cd_i)
        return new_state, qd_i @ state + qi_i @ v_new
    s_final, o = lax.scan(out_step, s0, (lead(u), lead(w), lead(k_tail), lead(chunk_decay), lead(q_dec), lead(q_intra)))
    return jnp.moveaxis(o, 0, 2).reshape(b, h, n, dv), s_final


def dn_output(o, z, norm_g):
    b, h, n, hd = o.shape
    o = rms_norm(jnp.moveaxis(o, 1, 2), norm_g) * jax.nn.silu(z.reshape(b, n, h, hd).astype(jnp.float32))
    return o.reshape(b, n, h * hd).astype(z.dtype)


def gated_deltanet(pl, pc, conv_w, a_log, dt_bias, norm_g, ctx_out):
    b = pl['dn_k'].shape[0]
    ql, kl, vl, bl, gl = dn_inputs(pl, conv_w, a_log, dt_bias, True)
    qc, kc, vc, bc, gc = dn_inputs(pc, conv_w, a_log, dt_bias, ctx_out)
    s0 = jnp.zeros((b, DN_HEADS, DN_HD, DN_HD), jnp.float32)
    def rev(t):
        return None if t is None else jnp.flip(t, axis=2)
    oc_f, sc_f = gated_delta_chunked(qc, kc, vc, bc[0], gc[0], s0)
    oc_b, sc_b = gated_delta_chunked(rev(qc), rev(kc), rev(vc), rev(bc[1]), rev(gc[1]), s0)
    ol_f, _ = gated_delta_chunked(ql, kl, vl, bl[0], gl[0], sc_f)
    ol_b, _ = gated_delta_chunked(rev(ql), rev(kl), rev(vl), rev(bl[1]), rev(gl[1]), sc_b)
    yl = dn_output(ol_f + rev(ol_b), pl['dn_z'], norm_g)
    yc = dn_output(oc_f + rev(oc_b), pc['dn_z'], norm_g) if ctx_out else None
    return yl, yc


def merge(gate_pre, ys, b_gate, w_branch, w_out):
    lead = gate_pre.shape[:-1]
    gates = jax.nn.sigmoid(gate_pre.reshape(*lead, N_BRANCH, -1) + b_gate)
    z = sum(gates[..., i, :] * (ys[i] @ w_branch[i]) for i in range(N_BRANCH))
    return z @ w_out


def clamped_swiglu(hgl):
    x_glu, x_lin = hgl[..., ::2], hgl[..., 1::2]
    x_glu = jnp.minimum(x_glu, SWIGLU_LIMIT)
    x_lin = jnp.clip(x_lin, -SWIGLU_LIMIT, SWIGLU_LIMIT)
    return x_glu * jax.nn.sigmoid(SWIGLU_ALPHA * x_glu) * (x_lin + 1)


def moe(h, w_router, b_router, w_e1, b_e1, w_e2, b_e2):
    shape = h.shape
    t = h.reshape(-1, shape[-1])
    n_tok = t.shape[0]
    logits = (t @ w_router).astype(jnp.float32) + b_router.astype(jnp.float32)
    top_v, top_i = lax.top_k(logits, TOP_K)
    top_w = jax.nn.softmax(top_v, axis=-1)
    n_assign = n_tok * TOP_K
    flat_e = top_i.reshape(n_assign)
    order = jnp.argsort(flat_e)
    sorted_e = flat_e[order]
    counts = jnp.bincount(flat_e, length=N_EXPERTS)
    padded = (counts + MOE_BLOCK - 1) // MOE_BLOCK * MOE_BLOCK
    pad_end = jnp.cumsum(padded)
    pad_start = pad_end - padded
    start = jnp.cumsum(counts) - counts
    dest = pad_start[sorted_e] + jnp.arange(n_assign) - start[sorted_e]
    n_blocks = -(-n_assign // MOE_BLOCK) + N_EXPERTS
    n_rows = n_blocks * MOE_BLOCK
    row_tok = jnp.zeros(n_rows, jnp.int32).at[dest].set((order // TOP_K).astype(jnp.int32))
    row_w = jnp.zeros(n_rows, jnp.float32).at[dest].set(top_w.reshape(n_assign)[order])
    blk_e = jnp.minimum(jnp.searchsorted(pad_end, jnp.arange(n_blocks) * MOE_BLOCK, side='right'), N_EXPERTS - 1)
    def run_block(args):
        tok, wgt, e = args
        hid = clamped_swiglu(t[tok] @ w_e1[e] + b_e1[e])
        return ((hid @ w_e2[e] + b_e2[e]) * wgt[:, None]).astype(t.dtype)
    y = lax.map(run_block, (row_tok.reshape(n_blocks, MOE_BLOCK), row_w.reshape(n_blocks, MOE_BLOCK), blk_e))
    out = jax.ops.segment_sum(y.reshape(n_rows, shape[-1]), row_tok, num_segments=n_tok)
    return out.reshape(shape)


def trunk_layer(xl, xc, c, c_ctx, w_mod, b_mod, norm1, w_in, da_lambda, da_subln, gm_ln_g, gm_ln_b,
                gm_ws, gm_bs, dn_conv, dn_a_log, dn_dt_bias, dn_norm, b_gate, w_branch, w_out, norm2,
                w_router, b_router, w_e1, b_e1, w_e2, b_e2, layer_idx, ctx_out, cos, sin):
    sh1, sc1, gt1, sh2, sc2, gt2 = modulation(c, w_mod, b_mod, 6)
    cm = modulation(c_ctx, w_mod, b_mod, 6 if ctx_out else 2)
    hl = adaln(xl, norm1, sh1[:, None, :], sc1[:, None, :])
    hc = adaln(xc, norm1, cm[0], cm[1])
    pl = in_proj(hl, w_in, ALL_SEGMENTS)
    pc = in_proj(hc, w_in, ALL_SEGMENTS if ctx_out else CTX_STATE_SEGMENTS)
    ya_l, ya_c = diff_attention(pl, pc, cos, sin, da_lambda, da_subln, layer_idx, ctx_out)
    yd_l, yd_c = gated_deltanet(pl, pc, dn_conv, dn_a_log, dn_dt_bias, dn_norm, ctx_out)
    yg_l = spatial_gating(pl, gm_ln_g, gm_ln_b, gm_ws, gm_bs)
    xl = xl + gt1[:, None, :] * merge(pl['gate'], (ya_l, yg_l, yd_l), b_gate, w_branch, w_out)
    xl = xl + gt2[:, None, :] * moe(adaln(xl, norm2, sh2[:, None, :], sc2[:, None, :]),
                                    w_router, b_router, w_e1, b_e1, w_e2, b_e2)
    if ctx_out:
        yg_c = spatial_gating(pc, gm_ln_g, gm_ln_b, gm_ws, gm_bs)
        xc = xc + cm[2] * merge(pc['gate'], (ya_c, yg_c, yd_c), b_gate, w_branch, w_out)
        xc = xc + cm[5] * moe(adaln(xc, norm2, cm[3], cm[4]), w_router, b_router, w_e1, b_e1, w_e2, b_e2)
    return xl, xc


def setup_inputs(seed: int = 0) -> dict:
    key = jax.random.key(seed)
    ks = jax.random.split(key, 29)
    f32 = jnp.float32
    L, D = DEPTH, D_MODEL
    n_in = sum(w for _, w in IN_SEGMENTS)
    def nrm(k, shape, s):
        return jax.random.normal(k, shape, f32) * s
    def gain(k, shape):
        return 1.0 + nrm(k, shape, 0.05)
    a_init = jax.random.uniform(ks[15], (L, 2, DN_HEADS), f32, 1.0, 16.0)
    dt = jnp.exp(jax.random.uniform(ks[16], (L, 2, DN_HEADS), f32, math.log(1e-3), math.log(1e-1)))
    return {
        'x': nrm(ks[0], (BATCH, SEQ, D), 1.0),
        'c': nrm(ks[1], (BATCH, D), 1.0),
        'ctx': nrm(ks[2], (BATCH, CTX_LEN, D), 1.0),
        'c_ctx': nrm(ks[3], (D,), 1.0),
        'w_mod': nrm(ks[4], (L, D, 6 * D), 0.5 * D ** -0.5),
        'b_mod': nrm(ks[5], (L, 6 * D), 0.02),
        'norm1': gain(ks[6], (L, D)),
        'w_in': nrm(ks[7], (L, D, n_in), D ** -0.5),
        'da_lambda': nrm(ks[8], (L, 4, DA_HD), 0.1),
        'da_subln': gain(ks[9], (L, 2 * DA_HD)),
        'gm_ln_g': gain(ks[10], (L, BRANCH_W)),
        'gm_ln_b': nrm(ks[11], (L, BRANCH_W), 0.02),
        'gm_ws': nrm(ks[12], (L, GM_GROUPS, GM_CHUNK, GM_CHUNK), GM_CHUNK ** -0.5),
        'gm_bs': gain(ks[13], (L, GM_GROUPS, GM_CHUNK)),
        'dn_conv': nrm(ks[14], (L, 3, DN_CONV, BRANCH_W), DN_CONV ** -0.5),
        'dn_a_log': jnp.log(a_init),
        'dn_dt_bias': dt + jnp.log(-jnp.expm1(-dt)),
        'dn_norm': gain(ks[17], (L, DN_HD)),
        'b_gate': nrm(ks[18], (L, N_BRANCH, D), 0.02),
        'w_branch': nrm(ks[19], (L, N_BRANCH, BRANCH_W, D), BRANCH_W ** -0.5),
        'w_out': nrm(ks[20], (L, D, D), D ** -0.5),
        'norm2': gain(ks[21], (L, D)),
        'w_router': nrm(ks[22], (L, D, N_EXPERTS), D ** -0.5),
        'b_router': nrm(ks[23], (L, N_EXPERTS), 0.01),
        'w_e1': nrm(ks[24], (L, N_EXPERTS, D, 2 * EXPERT_FF), D ** -0.5),
        'b_e1': nrm(ks[25], (L, N_EXPERTS, 2 * EXPERT_FF), 0.02),
        'w_e2': nrm(ks[26], (L, N_EXPERTS, EXPERT_FF, D), EXPERT_FF ** -0.5),
        'b_e2': nrm(ks[27], (L, N_EXPERTS, D), 0.02),
        'norm_f': gain(ks[28], (D,)),
    }


def reference(x, c, ctx, c_ctx, w_mod, b_mod, norm1, w_in, da_lambda, da_subln, gm_ln_g, gm_ln_b,
              gm_ws, gm_bs, dn_conv, dn_a_log, dn_dt_bias, dn_norm, b_gate, w_branch, w_out, norm2,
              w_router, b_router, w_e1, b_e1, w_e2, b_e2, norm_f):
    cos, sin = axial_rope(x.shape[1])
    xl, xc = x, ctx
    for l in range(DEPTH):
        xl, xc = trunk_layer(xl, xc, c, c_ctx, w_mod[l], b_mod[l], norm1[l], w_in[l], da_lambda[l], da_subln[l],
                             gm_ln_g[l], gm_ln_b[l], gm_ws[l], gm_bs[l], dn_conv[l], dn_a_log[l], dn_dt_bias[l],
                             dn_norm[l], b_gate[l], w_branch[l], w_out[l], norm2[l], w_router[l], b_router[l],
                             w_e1[l], b_e1[l], w_e2[l], b_e2[l], l, l < DEPTH - 1, cos, sin)
    return rms_norm(xl, norm_f)
```

```python
import functools
import math

import jax
import jax.numpy as jnp
from jax import lax
from jax.experimental import pallas as pl
from jax.experimental.pallas import tpu as pltpu

F32 = jnp.float32
BF16 = jnp.bfloat16

D_MODEL = 2048
GRID_W = 64
RMS_EPS = 1e-6
BRANCH_W = D_MODEL // 2
N_BRANCH = 3
DA_HD = 64
DA_HEADS = BRANCH_W // (2 * DA_HD)
ROPE_THETA = 10000.0
ROPE_PAIRS_AXIS = DA_HD // 4
GM_CHUNK = 128
GM_GW = 128
GM_GROUPS = BRANCH_W // GM_GW
DN_HD = 128
DN_HEADS = BRANCH_W // DN_HD
DN_CHUNK = 64
DN_CONV = 5
N_EXPERTS = 32
TOP_K = 4
EXPERT_FF = D_MODEL // 2
SWIGLU_LIMIT = 7.0
SWIGLU_ALPHA = 1.702

LANES = 128
VMEM_LIMIT = 56 * 1024 * 1024

OFF_DA_Q = 0
OFF_DA_K = 1024
OFF_DA_V = 2048
OFF_GM_U = 3072
OFF_GM_V = 4096
OFF_DN_Q = 5120
OFF_DN_K = 6144
OFF_DN_V = 7168
OFF_DN_Z = 8192
OFF_SMALL = 9216
OFF_GATE = 9248
N_MAIN = 9216 + N_BRANCH * D_MODEL
OFF_GATE_MAIN = 9216

MOE_TM = 512


def _cparams(sem):
    return pltpu.CompilerParams(dimension_semantics=sem, vmem_limit_bytes=VMEM_LIMIT)


def _pick(n, cands):
    for c in cands:
        if n % c == 0:
            return c
    raise ValueError(f"no tile for {n} in {cands}")


def _sigmoid(x):
    return jax.nn.sigmoid(x)


def _silu(x):
    return x * _sigmoid(x)


def _gelu_tanh(x):
    return x * (0.5 * (1.0 + jnp.tanh(0.7978845608028654 * (x + 0.044715 * (x * x * x)))))


def _bdot(a, b):
    return jnp.dot(a.astype(BF16), b.astype(BF16), preferred_element_type=F32)


def _split(a):
    hi = a.astype(BF16)
    lo = (a - hi.astype(F32)).astype(BF16)
    return hi, lo


def _dot3(a, b):
    ah, al = _split(a)
    bh, bl = _split(b)
    return (jnp.dot(ah, bh, preferred_element_type=F32)
            + (jnp.dot(al, bh, preferred_element_type=F32)
               + jnp.dot(ah, bl, preferred_element_type=F32)))


def _mod_kernel(c_ref, w_ref, b_ref, o_ref):
    s = _silu(c_ref[...])
    o_ref[...] = _bdot(s, w_ref[...]) + b_ref[...]


def _modulation(cond, w_mod, b_mod):
    nl, d, n6 = w_mod.shape
    r = cond.shape[0]
    tn = 1024
    return pl.pallas_call(
        _mod_kernel,
        grid=(nl, n6 // tn),
        in_specs=[
            pl.BlockSpec((r, d), lambda l, j: (0, 0)),
            pl.BlockSpec((None, d, tn), lambda l, j: (l, 0, j)),
            pl.BlockSpec((None, 1, tn), lambda l, j: (l, 0, j)),
        ],
        out_specs=pl.BlockSpec((None, r, tn), lambda l, j: (l, 0, j)),
        out_shape=jax.ShapeDtypeStruct((nl, r, n6), F32),
        compiler_params=_cparams(("arbitrary", "arbitrary")),
        name="modulation",
    )(cond, w_mod, b_mod.reshape(nl, 1, n6))


def _adaln_tile(x, g, shl, scl, shc, scc, row0, seq):
    tm = x.shape[0]
    y = x * lax.rsqrt(jnp.mean(x * x, axis=-1, keepdims=True) + RMS_EPS) * g
    row = row0 + lax.broadcasted_iota(jnp.int32, (tm, 1), 0)
    is_ctx = row >= seq
    scale = jnp.where(is_ctx, scc, scl)
    shift = jnp.where(is_ctx, shc, shl)
    return y * (1.0 + scale) + shift


def _adaln_kernel(x_ref, g_ref, shl_ref, scl_ref, shc_ref, scc_ref, o_ref, *, tm, seq):
    h = _adaln_tile(x_ref[...], g_ref[...], shl_ref[...], scl_ref[...], shc_ref[...], scc_ref[...],
                    pl.program_id(1) * tm, seq)
    o_ref[...] = h.astype(o_ref.dtype)


def _mod_specs(d):
    return [
        pl.BlockSpec((1, d), lambda b, i: (0, 0)),
        pl.BlockSpec((None, 1, d), lambda b, i: (b, 0, 0)),
        pl.BlockSpec((None, 1, d), lambda b, i: (b, 0, 0)),
        pl.BlockSpec((1, d), lambda b, i: (0, 0)),
        pl.BlockSpec((1, d), lambda b, i: (0, 0)),
    ]


def _adaln(xu, g, shl, scl, shc, scc, seq):
    bsz, t, d = xu.shape
    tm = _pick(t, (768, 384, 256, 128))
    return pl.pallas_call(
        functools.partial(_adaln_kernel, tm=tm, seq=seq),
        grid=(bsz, t // tm),
        in_specs=[pl.BlockSpec((None, tm, d), lambda b, i: (b, i, 0))] + _mod_specs(d),
        out_specs=pl.BlockSpec((None, tm, d), lambda b, i: (b, i, 0)),
        out_shape=jax.ShapeDtypeStruct((bsz, t, d), BF16),
        compiler_params=_cparams(("parallel", "parallel")),
        name="adaln",
    )(xu, g, shl, scl, shc, scc)


def _adaln_router_kernel(x_ref, g_ref, shl_ref, scl_ref, shc_ref, scc_ref, wr_ref, br_ref,
                         h_ref, idx_ref, wt_ref, *, tm, seq):
    h = _adaln_tile(x_ref[...], g_ref[...], shl_ref[...], scl_ref[...], shc_ref[...], scc_ref[...],
                    pl.program_id(1) * tm, seq)
    h_ref[...] = h.astype(h_ref.dtype)
    logits = _dot3(h, wr_ref[...]) + br_ref[...]
    lane = lax.broadcasted_iota(jnp.int32, logits.shape, 1).astype(F32)
    vals, idxs = [], []
    cur = logits
    for _ in range(TOP_K):
        m = jnp.max(cur, axis=-1, keepdims=True)
        am = jnp.min(jnp.where(cur == m, lane, float(LANES)), axis=-1, keepdims=True)
        vals.append(m)
        idxs.append(am)
        cur = jnp.where(lane == am, -jnp.inf, cur)
    es = [jnp.exp(v - vals[0]) for v in vals]
    tot = es[0] + es[1] + es[2] + es[3]
    wt = jnp.zeros(logits.shape, F32)
    ix = jnp.zeros(logits.shape, F32)
    for k in range(TOP_K):
        wt = jnp.where(lane == k, es[k] / tot, wt)
        ix = jnp.where(lane == k, idxs[k], ix)
    idx_ref[...] = ix.astype(jnp.int32)
    wt_ref[...] = wt


def _adaln_router(xu, g, shl, scl, shc, scc, w_router, b_router, seq, rows):
    bsz, t, d = xu.shape
    tm = _pick(rows, (768, 512, 384, 256, 128))
    wr = jnp.zeros((d, LANES), F32).at[:, :N_EXPERTS].set(w_router)
    br = jnp.full((1, LANES), -1e30, F32).at[0, :N_EXPERTS].set(b_router)
    return pl.pallas_call(
        functools.partial(_adaln_router_kernel, tm=tm, seq=seq),
        grid=(bsz, rows // tm),
        in_specs=[pl.BlockSpec((None, tm, d), lambda b, i: (b, i, 0))] + _mod_specs(d) + [
            pl.BlockSpec((d, LANES), lambda b, i: (0, 0)),
            pl.BlockSpec((1, LANES), lambda b, i: (0, 0)),
        ],
        out_specs=[
            pl.BlockSpec((None, tm, d), lambda b, i: (b, i, 0)),
            pl.BlockSpec((None, tm, LANES), lambda b, i: (b, i, 0)),
            pl.BlockSpec((None, tm, LANES), lambda b, i: (b, i, 0)),
        ],
        out_shape=[
            jax.ShapeDtypeStruct((bsz, rows, d), BF16),
            jax.ShapeDtypeStruct((bsz, rows, LANES), jnp.int32),
            jax.ShapeDtypeStruct((bsz, rows, LANES), F32),
        ],
        compiler_params=_cparams(("parallel", "parallel")),
        name="adaln_router",
    )(xu, g, shl, scl, shc, scc, wr, br)


def _inproj_kernel(h_ref, w_ref, cos_ref, sin_ref, o_ref, *, tn, n_rope_tiles, n_q_tiles):
    j = pl.program_id(1)
    acc = jnp.dot(h_ref[...], w_ref[...], preferred_element_type=F32)

    @pl.when(j >= n_rope_tiles)
    def _():
        o_ref[...] = acc.astype(o_ref.dtype)

    @pl.when(j < n_rope_tiles)
    def _():
        scale = jnp.where(j < n_q_tiles, DA_HD ** -0.5, 1.0).astype(F32)
        cos = cos_ref[...] * scale
        sin = sin_ref[...] * scale
        lane = lax.broadcasted_iota(jnp.int32, cos.shape, 1)
        first = (lane % DA_HD) < (DA_HD // 2)
        for c in range(tn // LANES):
            a = acc[:, c * LANES:(c + 1) * LANES]
            sw = jnp.where(first, pltpu.roll(a, LANES - DA_HD // 2, 1), pltpu.roll(a, DA_HD // 2, 1))
            o_ref[:, c * LANES:(c + 1) * LANES] = (a * cos + sw * sin).astype(o_ref.dtype)


def _in_proj_main(h, w_main, cos_t, sin_t):
    bsz, t, d = h.shape
    n = w_main.shape[1]
    tn = 512
    return pl.pallas_call(
        functools.partial(_inproj_kernel, tn=tn, n_rope_tiles=OFF_DA_V // tn, n_q_tiles=OFF_DA_K // tn),
        grid=(bsz, n // tn),
        in_specs=[
            pl.BlockSpec((None, t, d), lambda b, j: (b, 0, 0)),
            pl.BlockSpec((d, tn), lambda b, j: (0, j)),
            pl.BlockSpec((t, LANES), lambda b, j: (0, 0)),
            pl.BlockSpec((t, LANES), lambda b, j: (0, 0)),
        ],
        out_specs=pl.BlockSpec((None, t, tn), lambda b, j: (b, 0, j)),
        out_shape=jax.ShapeDtypeStruct((bsz, t, n), BF16),
        compiler_params=_cparams(("parallel", "arbitrary")),
        name="in_proj",
    )(h, w_main, cos_t, sin_t)


def _mm_kernel(x_ref, w_ref, o_ref):
    o_ref[...] = jnp.dot(x_ref[...], w_ref[...], preferred_element_type=F32).astype(o_ref.dtype)


def _in_proj_small(h, w_small):
    bsz, t, d = h.shape
    return pl.pallas_call(
        _mm_kernel,
        grid=(bsz,),
        in_specs=[
            pl.BlockSpec((None, t, d), lambda b: (b, 0, 0)),
            pl.BlockSpec((d, LANES), lambda b: (0, 0)),
        ],
        out_specs=pl.BlockSpec((None, t, LANES), lambda b: (b, 0, 0)),
        out_shape=jax.ShapeDtypeStruct((bsz, t, LANES), F32),
        compiler_params=_cparams(("parallel",)),
        name="in_proj_small",
    )(h, w_small)


def _rope_tables(seq, t):
    rows = seq // GRID_W
    row = jnp.repeat(jnp.arange(rows, dtype=F32), GRID_W)
    col = jnp.tile(jnp.arange(GRID_W, dtype=F32), rows)
    inv = ROPE_THETA ** (-jnp.arange(ROPE_PAIRS_AXIS, dtype=F32) / ROPE_PAIRS_AXIS)
    ang = jnp.concatenate([row[:, None] * inv, col[:, None] * inv], axis=-1)
    cos, sin = jnp.cos(ang), jnp.sin(ang)
    cos_t = jnp.tile(cos, (1, LANES // (DA_HD // 2)))
    sin_t = jnp.tile(jnp.concatenate([-sin, sin], axis=-1), (1, LANES // DA_HD))
    pad = t - seq
    cos_t = jnp.concatenate([cos_t, jnp.ones((pad, LANES), F32)], axis=0)
    sin_t = jnp.concatenate([sin_t, jnp.zeros((pad, LANES), F32)], axis=0)
    return cos_t, sin_t


def _attn_kernel(lam_ref, g_ref, q_ref, k_ref, v_ref, o_ref, *, seq, tq, lam_init):
    qi = pl.program_id(2)
    lp = lam_ref[...]
    l1 = jnp.sum(lp[0:1] * lp[1:2], axis=-1, keepdims=True)
    l2 = jnp.sum(lp[2:3] * lp[3:4], axis=-1, keepdims=True)
    lam = jnp.exp(l1) - jnp.exp(l2) + lam_init
    q = q_ref[...].astype(F32)
    lane = lax.broadcasted_iota(jnp.int32, q.shape, 1)
    qq = jnp.concatenate([jnp.where(lane < DA_HD, q, 0.0), jnp.where(lane >= DA_HD, q, 0.0)],
                         axis=0).astype(BF16)

    def core(k, v):
        s = lax.dot_general(qq, k, (((1,), (1,)), ((), ())), preferred_element_type=F32)
        m = jnp.max(s, axis=-1, keepdims=True)
        p = jnp.exp(s - m)
        den = jnp.sum(p, axis=-1, keepdims=True)
        o = jnp.dot(p.astype(BF16), v, preferred_element_type=F32) / den
        o = o[:tq] - lam * o[tq:]
        y = o * lax.rsqrt(jnp.mean(o * o, axis=-1, keepdims=True) + RMS_EPS) * g_ref[...]
        o_ref[...] = (y * (1.0 - lam_init)).astype(o_ref.dtype)

    @pl.when(qi * tq < seq)
    def _():
        core(k_ref[...], v_ref[...])

    @pl.when(qi * tq >= seq)
    def _():
        core(k_ref[seq:, :], v_ref[seq:, :])


def _diff_attention(p, lam_params, subln_g, seq, rows, lam_init):
    bsz, t, _ = p.shape
    tq = _pick(math.gcd(seq, t - seq), (256, 128))
    cq, ck, cv = OFF_DA_Q // LANES, OFF_DA_K // LANES, OFF_DA_V // LANES
    return pl.pallas_call(
        functools.partial(_attn_kernel, seq=seq, tq=tq, lam_init=lam_init),
        grid=(bsz, DA_HEADS, rows // tq),
        in_specs=[
            pl.BlockSpec((4, DA_HD), lambda b, h, i: (0, 0)),
            pl.BlockSpec((1, 2 * DA_HD), lambda b, h, i: (0, 0)),
            pl.BlockSpec((None, tq, LANES), lambda b, h, i: (b, i, cq + h)),
            pl.BlockSpec((None, t, LANES), lambda b, h, i: (b, 0, ck + h)),
            pl.BlockSpec((None, t, LANES), lambda b, h, i: (b, 0, cv + h)),
        ],
        out_specs=pl.BlockSpec((None, tq, LANES), lambda b, h, i: (b, i, h)),
        out_shape=jax.ShapeDtypeStruct((bsz, rows, BRANCH_W), BF16),
        compiler_params=_cparams(("parallel", "parallel", "arbitrary")),
        name="diff_attention",
    )(lam_params, subln_g.reshape(1, -1), p, p, p)


def _gmlp_kernel(u_ref, v_ref, lng_ref, lnb_ref, ws_ref, bs_ref, o_ref, *, nchunks):
    for c in range(nchunks):
        r0 = c * GM_CHUNK
        u = _gelu_tanh(u_ref[r0:r0 + GM_CHUNK, :].astype(F32))
        v = _gelu_tanh(v_ref[r0:r0 + GM_CHUNK, :].astype(F32))
        xc = v - jnp.mean(v, axis=-1, keepdims=True)
        var = jnp.mean(xc * xc, axis=-1, keepdims=True)
        vn = (xc * lax.rsqrt(var + RMS_EPS) * lng_ref[...] + lnb_ref[...]).astype(BF16)
        for g in range(GM_GROUPS):
            cs = slice(g * GM_GW, (g + 1) * GM_GW)
            s = jnp.dot(ws_ref[g], vn[:, cs], preferred_element_type=F32) + bs_ref[g]
            o_ref[r0:r0 + GM_CHUNK, cs] = (u[:, cs] * s).astype(o_ref.dtype)


def _spatial_gating(p, ln_g, ln_b, ws, bs, rows):
    bsz, t, _ = p.shape
    tm = _pick(rows, (768, 512, 384, 256, 128))
    cu, cv = OFF_GM_U // BRANCH_W, OFF_GM_V // BRANCH_W
    bs_b = jnp.broadcast_to(bs[:, :, None], (GM_GROUPS, GM_CHUNK, GM_GW)).astype(F32)
    return pl.pallas_call(
        functools.partial(_gmlp_kernel, nchunks=tm // GM_CHUNK),
        grid=(bsz, rows // tm),
        in_specs=[
            pl.BlockSpec((None, tm, BRANCH_W), lambda b, i: (b, i, cu)),
            pl.BlockSpec((None, tm, BRANCH_W), lambda b, i: (b, i, cv)),
            pl.BlockSpec((1, BRANCH_W), lambda b, i: (0, 0)),
            pl.BlockSpec((1, BRANCH_W), lambda b, i: (0, 0)),
            pl.BlockSpec((GM_GROUPS, GM_CHUNK, GM_CHUNK), lambda b, i: (0, 0, 0)),
            pl.BlockSpec((GM_GROUPS, GM_CHUNK, GM_GW), lambda b, i: (0, 0, 0)),
        ],
        out_specs=pl.BlockSpec((None, tm, BRANCH_W), lambda b, i: (b, i, 0)),
        out_shape=jax.ShapeDtypeStruct((bsz, rows, BRANCH_W), BF16),
        compiler_params=_cparams(("parallel", "parallel")),
        name="spatial_gating",
    )(p, p, ln_g.reshape(1, -1), ln_b.reshape(1, -1), ws.astype(BF16), bs_b)


def _tri_inverse(lmat, ii, jj, eye):
    base = 16
    md = jnp.where((ii // base) == (jj // base), -lmat, 0.0)
    x = eye + md
    pw = md
    for _ in range(3):
        pw = _dot3(pw, pw)
        x = x + _dot3(x, pw)
    bs = base
    while bs < DN_CHUNK:
        off = ((ii // (2 * bs)) == (jj // (2 * bs))) & ((ii // bs) != (jj // bs))
        cmat = jnp.where(off, lmat, 0.0)
        x = x - _dot3(_dot3(x, cmat), x)
        bs *= 2
    return x


def _dn_kernel(alog_ref, dtb_ref, q_ref, k_ref, v_ref, z_ref, sc_ref, sr_ref, cw_ref, ng_ref, o_ref,
               qn_ref, kn_ref, vn_ref, colv_ref, rowv_ref, oacc_ref, st_ref, *, seq, t):
    h = pl.program_id(1)
    nc = t // DN_CHUNK
    n_lat = seq // DN_CHUNK
    n_ctx = nc - n_lat

    row = lax.broadcasted_iota(jnp.int32, (t, 1), 0)
    seg_lo = jnp.where(row < seq, 0, seq)
    seg_hi = jnp.where(row < seq, seq, t)

    def conv_silu(x_ref, w):
        x = x_ref[...].astype(F32)
        acc = x * w[DN_CONV // 2:DN_CONV // 2 + 1, :]
        for s in (-2, -1, 1, 2):
            xs = pltpu.roll(x, (-s) % t, 0)
            rs = row + s
            ok = (rs >= seg_lo) & (rs < seg_hi)
            acc = acc + jnp.where(ok, xs, 0.0) * w[DN_CONV // 2 + s:DN_CONV // 2 + s + 1, :]
        return _silu(acc)

    def l2n(x):
        return x * lax.rsqrt(jnp.sum(x * x, axis=-1, keepdims=True) + RMS_EPS)

    qn_ref[...] = l2n(conv_silu(q_ref, cw_ref[0])) * (DN_HD ** -0.5)
    kn_ref[...] = l2n(conv_silu(k_ref, cw_ref[1]))
    vn_ref[...] = conv_silu(v_ref, cw_ref[2])

    def softplus(x):
        return jnp.maximum(x, 0.0) + jnp.log1p(jnp.exp(-jnp.abs(x)))

    def beta_and_decay(raw, idx):
        a_log = jnp.where(idx == 2, alog_ref[0, h], alog_ref[1, h])
        dt = jnp.where(idx == 2, dtb_ref[0, h], dtb_ref[1, h])
        return jnp.where(idx < 2, _sigmoid(raw), -jnp.exp(a_log) * softplus(raw + dt))

    sc = sc_ref[...]
    colv_ref[...] = beta_and_decay(sc, lax.broadcasted_iota(jnp.int32, sc.shape, 1))
    sr = sr_ref[...]
    rowv_ref[...] = beta_and_decay(sr, lax.broadcasted_iota(jnp.int32, sr.shape, 1))

    oacc_ref[...] = jnp.zeros(oacc_ref.shape, F32)
    st_ref[...] = jnp.zeros(st_ref.shape, F32)

    cshape = (DN_CHUNK, DN_CHUNK)
    ii = lax.broadcasted_iota(jnp.int32, cshape, 0)
    jj = lax.broadcasted_iota(jnp.int32, cshape, 1)
    eye = jnp.where(ii == jj, 1.0, 0.0).astype(F32)

    def chunk_step(c, d):
        r0 = pl.multiple_of(c * DN_CHUNK, DN_CHUNK)
        rows = pl.ds(r0, DN_CHUNK)
        kk = kn_ref[rows, :]
        qq = qn_ref[rows, :]
        vv = vn_ref[rows, :]
        cv = colv_ref[rows, :]
        rv = rowv_ref[c]
        bcol, gcol = cv[:, d:d + 1], cv[:, 2 + d:3 + d]
        grow = rv[2 + d:3 + d, :]
        if d == 0:
            incl, strict, incl_t = ii >= jj, ii > jj, ii <= jj
        else:
            incl, strict, incl_t = ii <= jj, ii < jj, ii >= jj
        gc_col = jnp.sum(jnp.where(incl, grow, 0.0), axis=1, keepdims=True)
        gc_row = jnp.sum(jnp.where(incl_t, gcol, 0.0), axis=0, keepdims=True)
        g_tot = jnp.sum(grow, axis=1, keepdims=True)
        dec = jnp.exp(jnp.where(incl, gc_col - gc_row, -jnp.inf))
        kb = kk.astype(BF16)
        gram = lax.dot_general(kb, kb, (((1,), (1,)), ((), ())), preferred_element_type=F32)
        qk = lax.dot_general(qq.astype(BF16), kb, (((1,), (1,)), ((), ())), preferred_element_type=F32)
        lmat = jnp.where(strict, bcol * gram * dec, 0.0)
        eg = jnp.exp(gc_col)
        rhs = jnp.concatenate([vv * bcol, kk * (bcol * eg)], axis=1)
        sol = _bdot(_tri_inverse(lmat, ii, jj, eye), rhs)
        u, w = sol[:, :DN_HD], sol[:, DN_HD:]
        q_intra = jnp.where(incl, qk * dec, 0.0)
        q_dec = qq * eg
        k_tail = kk * jnp.exp(g_tot - gc_col)
        cd = jnp.exp(g_tot)

        state = st_ref[d]
        ws = _bdot(jnp.concatenate([w, q_dec], axis=0), state)
        v_new = u - ws[:DN_CHUNK]
        o = ws[DN_CHUNK:] + _bdot(q_intra, v_new)
        upd = lax.dot_general(k_tail.astype(BF16), v_new.astype(BF16), (((0,), (0,)), ((), ())),
                              preferred_element_type=F32)
        st_ref[d] = state * cd + upd
        oacc_ref[rows, :] = oacc_ref[rows, :] + o

    def body(i, carry):
        cf = jnp.where(i < n_ctx, i + n_lat, i - n_ctx)
        chunk_step(cf, 0)
        chunk_step(nc - 1 - i, 1)
        return carry

    lax.fori_loop(0, nc, body, 0)

    o = oacc_ref[...]
    y = o * lax.rsqrt(jnp.mean(o * o, axis=-1, keepdims=True) + RMS_EPS) * ng_ref[...]
    o_ref[...] = (y * _silu(z_ref[...].astype(F32))).astype(o_ref.dtype)


def _gated_deltanet(p, small, conv_w, a_log, dt_bias, norm_g, seq):
    bsz, t, _ = p.shape
    nc = t // DN_CHUNK
    beta = small[..., :2 * DN_HEADS].reshape(bsz, t, 2, DN_HEADS)
    a = small[..., 2 * DN_HEADS:4 * DN_HEADS].reshape(bsz, t, 2, DN_HEADS)
    cols = jnp.concatenate([beta, a, jnp.zeros_like(beta), jnp.zeros_like(a)], axis=2)
    s_col = jnp.transpose(cols, (0, 3, 1, 2))
    s_row = jnp.transpose(cols.reshape(bsz, nc, DN_CHUNK, 8, DN_HEADS), (0, 4, 1, 3, 2))
    cq, ck, cv, cz = (OFF_DN_Q // LANES, OFF_DN_K // LANES, OFF_DN_V // LANES, OFF_DN_Z // LANES)
    slab = lambda c0: pl.BlockSpec((None, t, LANES), lambda b, h: (b, 0, c0 + h))
    smem = pl.BlockSpec(memory_space=pltpu.SMEM)
    return pl.pallas_call(
        functools.partial(_dn_kernel, seq=seq, t=t),
        grid=(bsz, DN_HEADS),
        in_specs=[
            smem, smem,
            slab(cq), slab(ck), slab(cv), slab(cz),
            pl.BlockSpec((None, None, t, 8), lambda b, h: (b, h, 0, 0)),
            pl.BlockSpec((None, None, nc, 8, DN_CHUNK), lambda b, h: (b, h, 0, 0, 0)),
            pl.BlockSpec((3, DN_CONV, LANES), lambda b, h: (0, 0, h)),
            pl.BlockSpec((1, DN_HD), lambda b, h: (0, 0)),
        ],
        out_specs=pl.BlockSpec((None, t, LANES), lambda b, h: (b, 0, h)),
        out_shape=jax.ShapeDtypeStruct((bsz, t, BRANCH_W), BF16),
        scratch_shapes=[
            pltpu.VMEM((t, DN_HD), F32), pltpu.VMEM((t, DN_HD), F32), pltpu.VMEM((t, DN_HD), F32),
            pltpu.VMEM((t, 8), F32), pltpu.VMEM((nc, 8, DN_CHUNK), F32),
            pltpu.VMEM((t, DN_HD), F32), pltpu.VMEM((2, DN_HD, DN_HD), F32),
        ],
        compiler_params=_cparams(("parallel", "parallel")),
        name="gated_deltanet",
    )(a_log, dt_bias, p, p, p, p, s_col, s_row, conv_w, norm_g.reshape(1, -1))


def _merge_kernel(ya_ref, yg_ref, yd_ref, ga_ref, gg_ref, gd_ref, wb_ref, bg_ref, o_ref):
    acc = None
    for i, (y_ref, g_ref) in enumerate(((ya_ref, ga_ref), (yg_ref, gg_ref), (yd_ref, gd_ref))):
        gate = _sigmoid(g_ref[...].astype(F32) + bg_ref[i])
        term = gate * jnp.dot(y_ref[...], wb_ref[i], preferred_element_type=F32)
        acc = term if acc is None else acc + term
    o_ref[...] = acc.astype(o_ref.dtype)


def _merge(ya, yg, yd, p, w_branch, b_gate, rows):
    bsz = p.shape[0]
    d = D_MODEL
    tm = _pick(rows, (768, 512, 384, 256, 128))
    tn = 512
    g0 = OFF_GATE_MAIN // tn
    y_spec = pl.BlockSpec((None, tm, BRANCH_W), lambda b, i, j: (b, i, 0))
    gate_spec = lambda k: pl.BlockSpec((None, tm, tn), lambda b, i, j: (b, i, g0 + k * (d // tn) + j))
    return pl.pallas_call(
        _merge_kernel,
        grid=(bsz, rows // tm, d // tn),
        in_specs=[y_spec, y_spec, y_spec, gate_spec(0), gate_spec(1), gate_spec(2),
                  pl.BlockSpec((N_BRANCH, BRANCH_W, tn), lambda b, i, j: (0, 0, j)),
                  pl.BlockSpec((N_BRANCH, 1, tn), lambda b, i, j: (0, 0, j))],
        out_specs=pl.BlockSpec((None, tm, tn), lambda b, i, j: (b, i, j)),
        out_shape=jax.ShapeDtypeStruct((bsz, rows, d), BF16),
        compiler_params=_cparams(("parallel", "parallel", "arbitrary")),
        name="merge_branches",
    )(ya, yg, yd, p, p, p, w_branch, b_gate.reshape(N_BRANCH, 1, d))


def _outproj_kernel(z_ref, w_ref, x_ref, gl_ref, gc_ref, o_ref, *, tm, seq):
    acc = jnp.dot(z_ref[...], w_ref[...], preferred_element_type=F32)
    row = pl.program_id(1) * tm + lax.broadcasted_iota(jnp.int32, (tm, 1), 0)
    gate = jnp.where(row >= seq, gc_ref[...], gl_ref[...])
    o_ref[...] = x_ref[...] + gate * acc


def _out_proj_residual(z, w_out, xu, gate_l, gate_c, seq, rows):
    bsz, t, d = xu.shape
    tm = _pick(rows, (768, 512, 384, 256, 128))
    tn = 512
    return pl.pallas_call(
        functools.partial(_outproj_kernel, tm=tm, seq=seq),
        grid=(bsz, rows // tm, d // tn),
        in_specs=[
            pl.BlockSpec((None, tm, d), lambda b, i, j: (b, i, 0)),
            pl.BlockSpec((d, tn), lambda b, i, j: (0, j)),
            pl.BlockSpec((None, tm, tn), lambda b, i, j: (b, i, j)),
            pl.BlockSpec((None, 1, tn), lambda b, i, j: (b, 0, j)),
            pl.BlockSpec((1, tn), lambda b, i, j: (0, j)),
        ],
        out_specs=pl.BlockSpec((None, tm, tn), lambda b, i, j: (b, i, j)),
        out_shape=jax.ShapeDtypeStruct((bsz, rows, d), F32),
        compiler_params=_cparams(("parallel", "parallel", "arbitrary")),
        name="out_proj_residual",
    )(z, w_out, xu, gate_l, gate_c)


def _expert_kernel(be_ref, bv_ref, x_ref, w1g_ref, w1l_ref, b1g_ref, b1l_ref, w2_ref, b2_ref, rw_ref, o_ref):
    i = pl.program_id(0)

    @pl.when(bv_ref[i] > 0)
    def _():
        x = x_ref[...]
        xg = jnp.dot(x, w1g_ref[...], preferred_element_type=F32) + b1g_ref[...]
        xl = jnp.dot(x, w1l_ref[...], preferred_element_type=F32) + b1l_ref[...]
        xg = jnp.minimum(xg, SWIGLU_LIMIT)
        xl = jnp.clip(xl, -SWIGLU_LIMIT, SWIGLU_LIMIT)
        hid = xg * _sigmoid(SWIGLU_ALPHA * xg) * (xl + 1.0)
        y = jnp.dot(hid.astype(BF16), w2_ref[...], preferred_element_type=F32) + b2_ref[...]
        o_ref[...] = (y * rw_ref[...]).astype(o_ref.dtype)


def _experts(xs, blk_e, blk_valid, row_w, w1g, w1l, b1g, b1l, w2, b2):
    n_rows, d = xs.shape
    tm = MOE_TM
    ff = EXPERT_FF
    wmap = lambda i, be, bv: (be[i], 0, 0)
    grid_spec = pltpu.PrefetchScalarGridSpec(
        num_scalar_prefetch=2,
        grid=(n_rows // tm,),
        in_specs=[
            pl.BlockSpec((tm, d), lambda i, be, bv: (i, 0)),
            pl.BlockSpec((None, d, ff), wmap),
            pl.BlockSpec((None, d, ff), wmap),
            pl.BlockSpec((None, 1, ff), wmap),
            pl.BlockSpec((None, 1, ff), wmap),
            pl.BlockSpec((None, ff, d), wmap),
            pl.BlockSpec((None, 1, d), wmap),
            pl.BlockSpec((tm, 1), lambda i, be, bv: (i, 0)),
        ],
        out_specs=pl.BlockSpec((tm, d), lambda i, be, bv: (i, 0)),
    )
    return pl.pallas_call(
        _expert_kernel,
        grid_spec=grid_spec,
        out_shape=jax.ShapeDtypeStruct((n_rows, d), BF16),
        compiler_params=_cparams(("arbitrary",)),
        name="moe_experts",
    )(blk_e, blk_valid, xs, w1g, w1l, b1g, b1l, w2, b2, row_w)


def _moe(h2, top_i, top_w, w1g, w1l, b1g, b1l, w2, b2):
    n_tok, d = h2.shape
    tm = MOE_TM
    n_assign = n_tok * TOP_K
    flat_e = top_i.reshape(n_assign)
    order = jnp.argsort(flat_e)
    sorted_e = flat_e[order]
    counts = jnp.bincount(flat_e, length=N_EXPERTS)
    padded = (counts + tm - 1) // tm * tm
    pad_end = jnp.cumsum(padded)
    pad_start = pad_end - padded
    start = jnp.cumsum(counts) - counts
    dest = (pad_start[sorted_e] + jnp.arange(n_assign) - start[sorted_e]).astype(jnp.int32)
    n_blocks = -(-n_assign // tm) + N_EXPERTS
    n_rows = n_blocks * tm
    row_tok = jnp.zeros(n_rows, jnp.int32).at[dest].set((order // TOP_K).astype(jnp.int32))
    row_w = jnp.zeros(n_rows, F32).at[dest].set(top_w.reshape(n_assign)[order])
    blk_start = jnp.arange(n_blocks) * tm
    blk_valid = (blk_start < pad_end[-1]).astype(jnp.int32)
    blk_e = jnp.minimum(jnp.searchsorted(pad_end, blk_start, side='right'), N_EXPERTS - 1).astype(jnp.int32)
    blk_e = jnp.where(blk_valid > 0, blk_e, blk_e[jnp.maximum(pad_end[-1] // tm - 1, 0)])
    pos = jnp.zeros(n_assign, jnp.int32).at[order].set(dest)
    xs = h2[row_tok]
    y = _experts(xs, blk_e, blk_valid, row_w.reshape(n_rows, 1), w1g, w1l, b1g, b1l, w2, b2)
    return jnp.sum(y[pos].astype(F32).reshape(n_tok, TOP_K, d), axis=1)


def _final_kernel(x_ref, g_ref, o_ref):
    x = x_ref[...]
    o_ref[...] = x * lax.rsqrt(jnp.mean(x * x, axis=-1, keepdims=True) + RMS_EPS) * g_ref[...]


def _final_norm(xu, g, seq):
    bsz, t, d = xu.shape
    tm = _pick(seq, (512, 256, 128))
    return pl.pallas_call(
        _final_kernel,
        grid=(bsz, seq // tm),
        in_specs=[pl.BlockSpec((None, tm, d), lambda b, i: (b, i, 0)),
                  pl.BlockSpec((1, d), lambda b, i: (0, 0))],
        out_specs=pl.BlockSpec((None, tm, d), lambda b, i: (b, i, 0)),
        out_shape=jax.ShapeDtypeStruct((bsz, seq, d), F32),
        compiler_params=_cparams(("parallel", "parallel")),
        name="final_norm",
    )(xu, g.reshape(1, d))


def _layer(xu, mod_l, mod_c, seq, layer_idx, ctx_out, cos_t, sin_t, norm1, w_in, da_lambda, da_subln,
           gm_ln_g, gm_ln_b, gm_ws, gm_bs, dn_conv, dn_a_log, dn_dt_bias, dn_norm, b_gate, w_branch,
           w_out, norm2, w_router, b_router, w_e1, b_e1, w_e2, b_e2):
    bsz, t, d = xu.shape
    rows = t if ctx_out else seq
    lam_init = 0.8 - 0.6 * math.exp(-0.3 * layer_idx)
    ml = [mod_l[:, k:k + 1, :] for k in range(6)]
    mc = [mod_c[k:k + 1, :] for k in range(6)]

    h = _adaln(xu, norm1.reshape(1, d), ml[0], ml[1], mc[0], mc[1], seq)
    w_main = jnp.concatenate([w_in[:, :OFF_SMALL], w_in[:, OFF_GATE:]], axis=1).astype(BF16)
    w_small = jnp.zeros((d, LANES), BF16).at[:, :OFF_GATE - OFF_SMALL].set(
        w_in[:, OFF_SMALL:OFF_GATE].astype(BF16))
    p = _in_proj_main(h, w_main, cos_t, sin_t)
    small = _in_proj_small(h, w_small)

    ya = _diff_attention(p, da_lambda, da_subln, seq, rows, lam_init)
    yg = _spatial_gating(p, gm_ln_g, gm_ln_b, gm_ws, gm_bs, rows)
    yd = _gated_deltanet(p, small, dn_conv, dn_a_log, dn_dt_bias, dn_norm, seq)
    z = _merge(ya, yg, yd, p, w_branch.astype(BF16), b_gate, rows)
    xu = _out_proj_residual(z, w_out.astype(BF16), xu, ml[2], mc[2], seq, rows)

    h2, top_i, top_w = _adaln_router(xu, norm2.reshape(1, d), ml[3], ml[4], mc[3], mc[4],
                                     w_router, b_router, seq, rows)
    n_tok = bsz * rows
    w1g = w_e1[:, :, 0::2].astype(BF16)
    w1l = w_e1[:, :, 1::2].astype(BF16)
    b1g = b_e1[:, None, 0::2]
    b1l = b_e1[:, None, 1::2]
    y = _moe(h2.reshape(n_tok, d), top_i.reshape(n_tok, LANES)[:, :TOP_K],
             top_w.reshape(n_tok, LANES)[:, :TOP_K], w1g, w1l, b1g, b1l,
             w_e2.astype(BF16), b_e2[:, None, :]).reshape(bsz, rows, d)
    gate2 = ml[5] if not ctx_out else jnp.concatenate(
        [jnp.broadcast_to(ml[5], (bsz, seq, d)), jnp.broadcast_to(mc[5][None], (bsz, t - seq, d))], axis=1)
    return xu + gate2 * y


def kernel(x, c, ctx, c_ctx, w_mod, b_mod, norm1, w_in, da_lambda, da_subln, gm_ln_g, gm_ln_b, gm_ws, gm_bs,
           dn_conv, dn_a_log, dn_dt_bias, dn_norm, b_gate, w_branch, w_out, norm2, w_router, b_router,
           w_e1, b_e1, w_e2, b_e2, norm_f):
    bsz, seq, d = x.shape
    n_ctx = ctx.shape[1]
    t = seq + n_ctx
    depth = w_mod.shape[0]
    xu = jnp.concatenate([x, ctx], axis=1)
    r = -(-(bsz + 1) // 8) * 8
    cond = jnp.zeros((r, d), F32).at[:bsz].set(c).at[bsz].set(c_ctx)
    mod = _modulation(cond, w_mod, b_mod).reshape(depth, r, 6, d)
    cos_t, sin_t = _rope_tables(seq, t)
    for l in range(depth):
        xu = _layer(xu, mod[l, :bsz], mod[l, bsz], seq, l, l < depth - 1, cos_t, sin_t, norm1[l], w_in[l],
                    da_lambda[l], da_subln[l], gm_ln_g[l], gm_ln_b[l], gm_ws[l], gm_bs[l], dn_conv[l],
                    dn_a_log[l], dn_dt_bias[l], dn_norm[l], b_gate[l], w_branch[l], w_out[l], norm2[l],
                    w_router[l], b_router[l], w_e1[l], b_e1[l], w_e2[l], b_e2[l])
    return _final_norm(xu, norm_f, seq)
```

```python
import functools
import math

import jax
import jax.numpy as jnp
from jax import lax
from jax.experimental import pallas as pl
from jax.experimental.pallas import tpu as pltpu

F32 = jnp.float32
BF16 = jnp.bfloat16

D_MODEL = 2048
GRID_W = 64
RMS_EPS = 1e-6
BRANCH_W = D_MODEL // 2
N_BRANCH = 3
DA_HD = 64
DA_HEADS = BRANCH_W // (2 * DA_HD)
ROPE_THETA = 10000.0
ROPE_PAIRS_AXIS = DA_HD // 4
GM_CHUNK = 128
GM_GW = 128
GM_GROUPS = BRANCH_W // GM_GW
DN_HD = 128
DN_HEADS = BRANCH_W // DN_HD
DN_CHUNK = 64
DN_CONV = 5
N_EXPERTS = 32
TOP_K = 4
EXPERT_FF = D_MODEL // 2
SWIGLU_LIMIT = 7.0
SWIGLU_ALPHA = 1.702

LANES = 128
VMEM_LIMIT = 56 * 1024 * 1024

OFF_DA_Q = 0
OFF_DA_K = 1024
OFF_DA_V = 2048
OFF_GM_U = 3072
OFF_GM_V = 4096
OFF_DN_Q = 5120
OFF_DN_K = 6144
OFF_DN_V = 7168
OFF_DN_Z = 8192
OFF_SMALL = 9216
OFF_GATE = 9248
N_MAIN = 9216 + N_BRANCH * D_MODEL
OFF_GATE_MAIN = 9216

MOE_TM = 512


def _cparams(sem):
    return pltpu.CompilerParams(dimension_semantics=sem, vmem_limit_bytes=VMEM_LIMIT)


def _pick(n, cands):
    for c in cands:
        if n % c == 0:
            return c
    raise ValueError(f"no tile for {n} in {cands}")


def _sigmoid(x):
    return jax.nn.sigmoid(x)


def _silu(x):
    return x * _sigmoid(x)


def _gelu_tanh(x):
    return x * (0.5 * (1.0 + jnp.tanh(0.7978845608028654 * (x + 0.044715 * (x * x * x)))))


def _bdot(a, b):
    return jnp.dot(a.astype(BF16), b.astype(BF16), preferred_element_type=F32)


def _split(a):
    hi = a.astype(BF16)
    lo = (a - hi.astype(F32)).astype(BF16)
    return hi, lo


def _dot3(a, b):
    ah, al = _split(a)
    bh, bl = _split(b)
    return (jnp.dot(ah, bh, preferred_element_type=F32)
            + (jnp.dot(al, bh, preferred_element_type=F32)
               + jnp.dot(ah, bl, preferred_element_type=F32)))


def _mod_kernel(c_ref, w_ref, b_ref, o_ref):
    s = _silu(c_ref[...])
    o_ref[...] = _bdot(s, w_ref[...]) + b_ref[...]


def _modulation(cond, w_mod, b_mod):
    nl, d, n6 = w_mod.shape
    r = cond.shape[0]
    tn = 1024
    return pl.pallas_call(
        _mod_kernel,
        grid=(nl, n6 // tn),
        in_specs=[
            pl.BlockSpec((r, d), lambda l, j: (0, 0)),
            pl.BlockSpec((None, d, tn), lambda l, j: (l, 0, j)),
            pl.BlockSpec((None, 1, tn), lambda l, j: (l, 0, j)),
        ],
        out_specs=pl.BlockSpec((None, r, tn), lambda l, j: (l, 0, j)),
        out_shape=jax.ShapeDtypeStruct((nl, r, n6), F32),
        compiler_params=_cparams(("arbitrary", "arbitrary")),
        name="modulation",
    )(cond, w_mod, b_mod.reshape(nl, 1, n6))


def _adaln_tile(x, g, shl, scl, shc, scc, row0, seq):
    tm = x.shape[0]
    y = x * lax.rsqrt(jnp.mean(x * x, axis=-1, keepdims=True) + RMS_EPS) * g
    row = row0 + lax.broadcasted_iota(jnp.int32, (tm, 1), 0)
    is_ctx = row >= seq
    scale = jnp.where(is_ctx, scc, scl)
    shift = jnp.where(is_ctx, shc, shl)
    return y * (1.0 + scale) + shift


def _adaln_kernel(x_ref, g_ref, shl_ref, scl_ref, shc_ref, scc_ref, o_ref, *, tm, seq):
    h = _adaln_tile(x_ref[...], g_ref[...], shl_ref[...], scl_ref[...], shc_ref[...], scc_ref[...],
                    pl.program_id(1) * tm, seq)
    o_ref[...] = h.astype(o_ref.dtype)


def _mod_specs(d):
    return [
        pl.BlockSpec((1, d), lambda b, i: (0, 0)),
        pl.BlockSpec((None, 1, d), lambda b, i: (b, 0, 0)),
        pl.BlockSpec((None, 1, d), lambda b, i: (b, 0, 0)),
        pl.BlockSpec((1, d), lambda b, i: (0, 0)),
        pl.BlockSpec((1, d), lambda b, i: (0, 0)),
    ]


def _adaln(xu, g, shl, scl, shc, scc, seq):
    bsz, t, d = xu.shape
    tm = _pick(t, (768, 384, 256, 128))
    return pl.pallas_call(
        functools.partial(_adaln_kernel, tm=tm, seq=seq),
        grid=(bsz, t // tm),
        in_specs=[pl.BlockSpec((None, tm, d), lambda b, i: (b, i, 0))] + _mod_specs(d),
        out_specs=pl.BlockSpec((None, tm, d), lambda b, i: (b, i, 0)),
        out_shape=jax.ShapeDtypeStruct((bsz, t, d), BF16),
        compiler_params=_cparams(("parallel", "parallel")),
        name="adaln",
    )(xu, g, shl, scl, shc, scc)


def _adaln_router_kernel(x_ref, g_ref, shl_ref, scl_ref, shc_ref, scc_ref, wr_ref, br_ref,
                         h_ref, idx_ref, wt_ref, *, tm, seq):
    h = _adaln_tile(x_ref[...], g_ref[...], shl_ref[...], scl_ref[...], shc_ref[...], scc_ref[...],
                    pl.program_id(1) * tm, seq)
    h_ref[...] = h.astype(h_ref.dtype)
    logits = _dot3(h, wr_ref[...]) + br_ref[...]
    lane = lax.broadcasted_iota(jnp.int32, logits.shape, 1).astype(F32)
    vals, idxs = [], []
    cur = logits
    for _ in range(TOP_K):
        m = jnp.max(cur, axis=-1, keepdims=True)
        am = jnp.min(jnp.where(cur == m, lane, float(LANES)), axis=-1, keepdims=True)
        vals.append(m)
        idxs.append(am)
        cur = jnp.where(lane == am, -jnp.inf, cur)
    es = [jnp.exp(v - vals[0]) for v in vals]
    tot = es[0] + es[1] + es[2] + es[3]
    wt = jnp.zeros(logits.shape, F32)
    ix = jnp.zeros(logits.shape, F32)
    for k in range(TOP_K):
        wt = jnp.where(lane == k, es[k] / tot, wt)
        ix = jnp.where(lane == k, idxs[k], ix)
    idx_ref[...] = ix.astype(jnp.int32)
    wt_ref[...] = wt


def _adaln_router(xu, g, shl, scl, shc, scc, w_router, b_router, seq, rows):
    bsz, t, d = xu.shape
    tm = _pick(rows, (768, 512, 384, 256, 128))
    wr = jnp.zeros((d, LANES), F32).at[:, :N_EXPERTS].set(w_router)
    br = jnp.full((1, LANES), -1e30, F32).at[0, :N_EXPERTS].set(b_router)
    return pl.pallas_call(
        functools.partial(_adaln_router_kernel, tm=tm, seq=seq),
        grid=(bsz, rows // tm),
        in_specs=[pl.BlockSpec((None, tm, d), lambda b, i: (b, i, 0))] + _mod_specs(d) + [
            pl.BlockSpec((d, LANES), lambda b, i: (0, 0)),
            pl.BlockSpec((1, LANES), lambda b, i: (0, 0)),
        ],
        out_specs=[
            pl.BlockSpec((None, tm, d), lambda b, i: (b, i, 0)),
            pl.BlockSpec((None, tm, LANES), lambda b, i: (b, i, 0)),
            pl.BlockSpec((None, tm, LANES), lambda b, i: (b, i, 0)),
        ],
        out_shape=[
            jax.ShapeDtypeStruct((bsz, rows, d), BF16),
            jax.ShapeDtypeStruct((bsz, rows, LANES), jnp.int32),
            jax.ShapeDtypeStruct((bsz, rows, LANES), F32),
        ],
        compiler_params=_cparams(("parallel", "parallel")),
        name="adaln_router",
    )(xu, g, shl, scl, shc, scc, wr, br)


def _inproj_kernel(h_ref, w_ref, cos_ref, sin_ref, o_ref, *, tn, n_rope_tiles, n_q_tiles):
    j = pl.program_id(1)
    acc = jnp.dot(h_ref[...], w_ref[...], preferred_element_type=F32)

    @pl.when(j >= n_rope_tiles)
    def _():
        o_ref[...] = acc.astype(o_ref.dtype)

    @pl.when(j < n_rope_tiles)
    def _():
        scale = jnp.where(j < n_q_tiles, DA_HD ** -0.5, 1.0).astype(F32)
        cos = cos_ref[...] * scale
        sin = sin_ref[...] * scale
        lane = lax.broadcasted_iota(jnp.int32, cos.shape, 1)
        first = (lane % DA_HD) < (DA_HD // 2)
        for c in range(tn // LANES):
            a = acc[:, c * LANES:(c + 1) * LANES]
            sw = jnp.where(first, pltpu.roll(a, LANES - DA_HD // 2, 1), pltpu.roll(a, DA_HD // 2, 1))
            o_ref[:, c * LANES:(c + 1) * LANES] = (a * cos + sw * sin).astype(o_ref.dtype)


def _in_proj_main(h, w_main, cos_t, sin_t):
    bsz, t, d = h.shape
    n = w_main.shape[1]
    tn = 512
    return pl.pallas_call(
        functools.partial(_inproj_kernel, tn=tn, n_rope_tiles=OFF_DA_V // tn, n_q_tiles=OFF_DA_K // tn),
        grid=(bsz, n // tn),
        in_specs=[
            pl.BlockSpec((None, t, d), lambda b, j: (b, 0, 0)),
            pl.BlockSpec((d, tn), lambda b, j: (0, j)),
            pl.BlockSpec((t, LANES), lambda b, j: (0, 0)),
            pl.BlockSpec((t, LANES), lambda b, j: (0, 0)),
        ],
        out_specs=pl.BlockSpec((None, t, tn), lambda b, j: (b, 0, j)),
        out_shape=jax.ShapeDtypeStruct((bsz, t, n), BF16),
        compiler_params=_cparams(("parallel", "arbitrary")),
        name="in_proj",
    )(h, w_main, cos_t, sin_t)


def _mm_kernel(x_ref, w_ref, o_ref):
    o_ref[...] = jnp.dot(x_ref[...], w_ref[...], preferred_element_type=F32).astype(o_ref.dtype)


def _in_proj_small(h, w_small):
    bsz, t, d = h.shape
    return pl.pallas_call(
        _mm_kernel,
        grid=(bsz,),
        in_specs=[
            pl.BlockSpec((None, t, d), lambda b: (b, 0, 0)),
            pl.BlockSpec((d, LANES), lambda b: (0, 0)),
        ],
        out_specs=pl.BlockSpec((None, t, LANES), lambda b: (b, 0, 0)),
        out_shape=jax.ShapeDtypeStruct((bsz, t, LANES), F32),
        compiler_params=_cparams(("parallel",)),
        name="in_proj_small",
    )(h, w_small)


def _rope_tables(seq, t):
    rows = seq // GRID_W
    row = jnp.repeat(jnp.arange(rows, dtype=F32), GRID_W)
    col = jnp.tile(jnp.arange(GRID_W, dtype=F32), rows)
    inv = ROPE_THETA ** (-jnp.arange(ROPE_PAIRS_AXIS, dtype=F32) / ROPE_PAIRS_AXIS)
    ang = jnp.concatenate([row[:, None] * inv, col[:, None] * inv], axis=-1)
    cos, sin = jnp.cos(ang), jnp.sin(ang)
    cos_t = jnp.tile(cos, (1, LANES // (DA_HD // 2)))
    sin_t = jnp.tile(jnp.concatenate([-sin, sin], axis=-1), (1, LANES // DA_HD))
    pad = t - seq
    cos_t = jnp.concatenate([cos_t, jnp.ones((pad, LANES), F32)], axis=0)
    sin_t = jnp.concatenate([sin_t, jnp.zeros((pad, LANES), F32)], axis=0)
    return cos_t, sin_t


def _attn_kernel(lam_ref, g_ref, q_ref, k_ref, v_ref, o_ref, *, seq, tq, lam_init):
    qi = pl.program_id(2)
    lp = lam_ref[...]
    l1 = jnp.sum(lp[0:1] * lp[1:2], axis=-1, keepdims=True)
    l2 = jnp.sum(lp[2:3] * lp[3:4], axis=-1, keepdims=True)
    lam = jnp.exp(l1) - jnp.exp(l2) + lam_init
    q = q_ref[...].astype(F32)
    lane = lax.broadcasted_iota(jnp.int32, q.shape, 1)
    qq = jnp.concatenate([jnp.where(lane < DA_HD, q, 0.0), jnp.where(lane >= DA_HD, q, 0.0)],
                         axis=0).astype(BF16)

    def core(k, v):
        s = lax.dot_general(qq, k, (((1,), (1,)), ((), ())), preferred_element_type=F32)
        m = jnp.max(s, axis=-1, keepdims=True)
        p = jnp.exp(s - m)
        den = jnp.sum(p, axis=-1, keepdims=True)
        o = jnp.dot(p.astype(BF16), v, preferred_element_type=F32) / den
        o = o[:tq] - lam * o[tq:]
        y = o * lax.rsqrt(jnp.mean(o * o, axis=-1, keepdims=True) + RMS_EPS) * g_ref[...]
        o_ref[...] = (y * (1.0 - lam_init)).astype(o_ref.dtype)

    @pl.when(qi * tq < seq)
    def _():
        core(k_ref[...], v_ref[...])

    @pl.when(qi * tq >= seq)
    def _():
        core(k_ref[seq:, :], v_ref[seq:, :])


def _diff_attention(p, lam_params, subln_g, seq, rows, lam_init):
    bsz, t, _ = p.shape
    tq = _pick(math.gcd(seq, t - seq), (256, 128))
    cq, ck, cv = OFF_DA_Q // LANES, OFF_DA_K // LANES, OFF_DA_V // LANES
    return pl.pallas_call(
        functools.partial(_attn_kernel, seq=seq, tq=tq, lam_init=lam_init),
        grid=(bsz, DA_HEADS, rows // tq),
        in_specs=[
            pl.BlockSpec((4, DA_HD), lambda b, h, i: (0, 0)),
            pl.BlockSpec((1, 2 * DA_HD), lambda b, h, i: (0, 0)),
            pl.BlockSpec((None, tq, LANES), lambda b, h, i: (b, i, cq + h)),
            pl.BlockSpec((None, t, LANES), lambda b, h, i: (b, 0, ck + h)),
            pl.BlockSpec((None, t, LANES), lambda b, h, i: (b, 0, cv + h)),
        ],
        out_specs=pl.BlockSpec((None, tq, LANES), lambda b, h, i: (b, i, h)),
        out_shape=jax.ShapeDtypeStruct((bsz, rows, BRANCH_W), BF16),
        compiler_params=_cparams(("parallel", "parallel", "arbitrary")),
        name="diff_attention",
    )(lam_params, subln_g.reshape(1, -1), p, p, p)


def _gmlp_kernel(u_ref, v_ref, lng_ref, lnb_ref, ws_ref, bs_ref, o_ref, *, nchunks):
    for c in range(nchunks):
        r0 = c * GM_CHUNK
        u = _gelu_tanh(u_ref[r0:r0 + GM_CHUNK, :].astype(F32))
        v = _gelu_tanh(v_ref[r0:r0 + GM_CHUNK, :].astype(F32))
        xc = v - jnp.mean(v, axis=-1, keepdims=True)
        var = jnp.mean(xc * xc, axis=-1, keepdims=True)
        vn = (xc * lax.rsqrt(var + RMS_EPS) * lng_ref[...] + lnb_ref[...]).astype(BF16)
        for g in range(GM_GROUPS):
            cs = slice(g * GM_GW, (g + 1) * GM_GW)
            s = jnp.dot(ws_ref[g], vn[:, cs], preferred_element_type=F32) + bs_ref[g]
            o_ref[r0:r0 + GM_CHUNK, cs] = (u[:, cs] * s).astype(o_ref.dtype)


def _spatial_gating(p, ln_g, ln_b, ws, bs, rows):
    bsz, t, _ = p.shape
    tm = _pick(rows, (768, 512, 384, 256, 128))
    cu, cv = OFF_GM_U // BRANCH_W, OFF_GM_V // BRANCH_W
    bs_b = jnp.broadcast_to(bs[:, :, None], (GM_GROUPS, GM_CHUNK, GM_GW)).astype(F32)
    return pl.pallas_call(
        functools.partial(_gmlp_kernel, nchunks=tm // GM_CHUNK),
        grid=(bsz, rows // tm),
        in_specs=[
            pl.BlockSpec((None, tm, BRANCH_W), lambda b, i: (b, i, cu)),
            pl.BlockSpec((None, tm, BRANCH_W), lambda b, i: (b, i, cv)),
            pl.BlockSpec((1, BRANCH_W), lambda b, i: (0, 0)),
            pl.BlockSpec((1, BRANCH_W), lambda b, i: (0, 0)),
            pl.BlockSpec((GM_GROUPS, GM_CHUNK, GM_CHUNK), lambda b, i: (0, 0, 0)),
            pl.BlockSpec((GM_GROUPS, GM_CHUNK, GM_GW), lambda b, i: (0, 0, 0)),
        ],
        out_specs=pl.BlockSpec((None, tm, BRANCH_W), lambda b, i: (b, i, 0)),
        out_shape=jax.ShapeDtypeStruct((bsz, rows, BRANCH_W), BF16),
        compiler_params=_cparams(("parallel", "parallel")),
        name="spatial_gating",
    )(p, p, ln_g.reshape(1, -1), ln_b.reshape(1, -1), ws.astype(BF16), bs_b)


DN_BASE = 8
DN_PREP_GROUP = 2
DN_HEAD_GROUP = 2


def _tri_inverse(lmat, ii, jj, eye):
    md = jnp.where((ii // DN_BASE) == (jj // DN_BASE), -lmat, 0.0)
    x = eye + md
    pw = md
    span = 2
    while span < DN_BASE:
        pw = _dot3(pw, pw)
        x = x + _dot3(x, pw)
        span *= 2
    bs = DN_BASE
    while bs < DN_CHUNK:
        off = ((ii // (2 * bs)) == (jj // (2 * bs))) & ((ii // bs) != (jj // bs))
        cmat = jnp.where(off, lmat, 0.0)
        x = x - _dot3(_dot3(x, cmat), x)
        bs *= 2
    return x


def _dn_kernel(alog_ref, dtb_ref, q_ref, k_ref, v_ref, z_ref, sm_ref, sr_ref, cw_ref, ng_ref, o_ref,
               qn_ref, kn_ref, vn_ref, rowv_ref, u_ref, wq_ref, kt_ref, qi_ref, cd_ref, oacc_ref, st_ref,
               *, seq, t, hg):
    hblk = pl.program_id(1)
    nc = t // DN_CHUNK
    n_lat = seq // DN_CHUNK
    n_ctx = nc - n_lat
    hd_w = DN_HD

    row = lax.broadcasted_iota(jnp.int32, (t, 1), 0)
    seg_lo = jnp.where(row < seq, 0, seq)
    seg_hi = jnp.where(row < seq, seq, t)

    def conv_silu(x_ref, w):
        x = x_ref[...].astype(F32)
        acc = x * w[DN_CONV // 2:DN_CONV // 2 + 1, :]
        for s in (-2, -1, 1, 2):
            xs = pltpu.roll(x, (-s) % t, 0)
            rs = row + s
            ok = (rs >= seg_lo) & (rs < seg_hi)
            acc = acc + jnp.where(ok, xs, 0.0) * w[DN_CONV // 2 + s:DN_CONV // 2 + s + 1, :]
        return _silu(acc)

    def l2n(x):
        return x * lax.rsqrt(jnp.sum(x * x, axis=-1, keepdims=True) + RMS_EPS)

    qc = conv_silu(q_ref, cw_ref[0])
    kc = conv_silu(k_ref, cw_ref[1])
    vn_ref[...] = conv_silu(v_ref, cw_ref[2])
    for j in range(hg):
        cs = slice(j * hd_w, (j + 1) * hd_w)
        qn_ref[:, cs] = l2n(qc[:, cs]) * (DN_HD ** -0.5)
        kn_ref[:, cs] = l2n(kc[:, cs])

    def softplus(x):
        return jnp.maximum(x, 0.0) + jnp.log1p(jnp.exp(-jnp.abs(x)))

    def decay_rate(d, head, shape):
        return -jnp.exp(jnp.full(shape, alog_ref[d, head], F32))

    for j in range(hg):
        head = hblk * hg + j
        sr = sr_ref[j]
        idx = lax.broadcasted_iota(jnp.int32, sr.shape, 1)
        rate = jnp.where(idx == 2, decay_rate(0, head, sr.shape), decay_rate(1, head, sr.shape))
        dt = jnp.where(idx == 2, dtb_ref[0, head], dtb_ref[1, head])
        rowv_ref[j] = jnp.where(idx < 2, _sigmoid(sr), rate * softplus(sr + dt))

    st_ref[...] = jnp.zeros(st_ref.shape, F32)

    cshape = (DN_CHUNK, DN_CHUNK)
    ii = lax.broadcasted_iota(jnp.int32, cshape, 0)
    jj = lax.broadcasted_iota(jnp.int32, cshape, 1)
    eye = jnp.where(ii == jj, 1.0, 0.0).astype(F32)
    lane = lax.broadcasted_iota(jnp.int32, (DN_CHUNK, LANES), 1)

    def prep(c, j):
        head = hblk * hg + j
        cs = slice(j * hd_w, (j + 1) * hd_w)
        r0 = pl.multiple_of(c * DN_CHUNK, DN_CHUNK)
        rows = pl.ds(r0, DN_CHUNK)
        kk = kn_ref[rows, cs]
        qq = qn_ref[rows, cs]
        vv = vn_ref[rows, cs]
        sm = sm_ref[rows, :]
        rv = rowv_ref[j, c]
        kb = kk.astype(BF16)
        gram = lax.dot_general(kb, kb, (((1,), (1,)), ((), ())), preferred_element_type=F32)
        qk = lax.dot_general(qq.astype(BF16), kb, (((1,), (1,)), ((), ())), preferred_element_type=F32)

        def col(idx):
            return jnp.sum(jnp.where(lane == idx, sm, 0.0), axis=1, keepdims=True)

        for d in range(2):
            bcol = _sigmoid(col(d * DN_HEADS + head))
            gcol = decay_rate(d, head, (DN_CHUNK, 1)) * softplus(
                col(2 * DN_HEADS + d * DN_HEADS + head) + dtb_ref[d, head])
            grow = rv[2 + d:3 + d, :]
            if d == 0:
                incl, strict, incl_t = ii >= jj, ii > jj, ii <= jj
            else:
                incl, strict, incl_t = ii <= jj, ii < jj, ii >= jj
            gc_col = jnp.sum(jnp.where(incl, grow, 0.0), axis=1, keepdims=True)
            gc_row = jnp.sum(jnp.where(incl_t, gcol, 0.0), axis=0, keepdims=True)
            g_tot = jnp.sum(grow, axis=1, keepdims=True)
            dec = jnp.exp(jnp.where(incl, gc_col - gc_row, -jnp.inf))
            lmat = jnp.where(strict, bcol * gram * dec, 0.0)
            eg = jnp.exp(gc_col)
            rhs = jnp.concatenate([vv * bcol, kk * (bcol * eg)], axis=1)
            sol = _bdot(_tri_inverse(lmat, ii, jj, eye), rhs)
            idx = d * hg + j
            u_ref[idx, rows, :] = sol[:, :DN_HD]
            wq_ref[idx, c] = jnp.concatenate([sol[:, DN_HD:], qq * eg], axis=0).astype(BF16)
            kt_ref[idx, rows, :] = (kk * jnp.exp(g_tot - gc_col)).astype(BF16)
            qi_ref[idx, c] = jnp.where(incl, qk * dec, 0.0).astype(BF16)
            cd_ref[idx, c] = jnp.broadcast_to(jnp.exp(g_tot), (1, LANES))

    def prep_body(g, carry):
        for cc in range(DN_PREP_GROUP):
            for j in range(hg):
                prep(g * DN_PREP_GROUP + cc, j)
        return carry

    lax.fori_loop(0, nc // DN_PREP_GROUP, prep_body, 0)

    def scan_step(c, idx):
        rows = pl.ds(pl.multiple_of(c * DN_CHUNK, DN_CHUNK), DN_CHUNK)
        state = st_ref[idx]
        ws = jnp.dot(wq_ref[idx, c], state.astype(BF16), preferred_element_type=F32)
        v_new = u_ref[idx, rows, :] - ws[:DN_CHUNK]
        vb = v_new.astype(BF16)
        oacc_ref[idx, rows, :] = ws[DN_CHUNK:] + jnp.dot(qi_ref[idx, c], vb, preferred_element_type=F32)
        upd = lax.dot_general(kt_ref[idx, rows, :], vb, (((0,), (0,)), ((), ())), preferred_element_type=F32)
        st_ref[idx] = state * cd_ref[idx, c] + upd

    def scan_body(i, carry):
        cf = jnp.where(i < n_ctx, i + n_lat, i - n_ctx)
        cb = nc - 1 - i
        for j in range(hg):
            scan_step(cf, j)
            scan_step(cb, hg + j)
        return carry

    lax.fori_loop(0, nc, scan_body, 0)

    for j in range(hg):
        cs = slice(j * hd_w, (j + 1) * hd_w)
        o = oacc_ref[j] + oacc_ref[hg + j]
        y = o * lax.rsqrt(jnp.mean(o * o, axis=-1, keepdims=True) + RMS_EPS) * ng_ref[...]
        o_ref[:, cs] = (y * _silu(z_ref[:, cs].astype(F32))).astype(o_ref.dtype)


def _gated_deltanet(p, small, conv_w, a_log, dt_bias, norm_g, seq):
    bsz, t, _ = p.shape
    nc = t // DN_CHUNK
    hg = DN_HEAD_GROUP
    w = hg * DN_HD
    beta = small[..., :2 * DN_HEADS].reshape(bsz, t, 2, DN_HEADS)
    a = small[..., 2 * DN_HEADS:4 * DN_HEADS].reshape(bsz, t, 2, DN_HEADS)
    cols = jnp.concatenate([beta, a, jnp.zeros_like(beta), jnp.zeros_like(a)], axis=2)
    s_row = jnp.transpose(cols.reshape(bsz, nc, DN_CHUNK, 8, DN_HEADS), (0, 4, 1, 3, 2))
    cq, ck, cv, cz = (OFF_DN_Q // w, OFF_DN_K // w, OFF_DN_V // w, OFF_DN_Z // w)
    slab = lambda c0: pl.BlockSpec((None, t, w), lambda b, h: (b, 0, c0 + h))
    smem = pl.BlockSpec(memory_space=pltpu.SMEM)
    return pl.pallas_call(
        functools.partial(_dn_kernel, seq=seq, t=t, hg=hg),
        grid=(bsz, DN_HEADS // hg),
        in_specs=[
            smem, smem,
            slab(cq), slab(ck), slab(cv), slab(cz),
            pl.BlockSpec((None, t, LANES), lambda b, h: (b, 0, 0)),
            pl.BlockSpec((None, hg, nc, 8, DN_CHUNK), lambda b, h: (b, h, 0, 0, 0)),
            pl.BlockSpec((3, DN_CONV, w), lambda b, h: (0, 0, h)),
            pl.BlockSpec((1, DN_HD), lambda b, h: (0, 0)),
        ],
        out_specs=pl.BlockSpec((None, t, w), lambda b, h: (b, 0, h)),
        out_shape=jax.ShapeDtypeStruct((bsz, t, BRANCH_W), BF16),
        scratch_shapes=[
            pltpu.VMEM((t, w), F32), pltpu.VMEM((t, w), F32), pltpu.VMEM((t, w), F32),
            pltpu.VMEM((hg, nc, 8, DN_CHUNK), F32),
            pltpu.VMEM((2 * hg, t, DN_HD), F32),
            pltpu.VMEM((2 * hg, nc, 2 * DN_CHUNK, DN_HD), BF16),
            pltpu.VMEM((2 * hg, t, DN_HD), BF16),
            pltpu.VMEM((2 * hg, nc, DN_CHUNK, DN_CHUNK), BF16),
            pltpu.VMEM((2 * hg, nc, 1, LANES), F32),
            pltpu.VMEM((2 * hg, t, DN_HD), F32),
            pltpu.VMEM((2 * hg, DN_HD, DN_HD), F32),
        ],
        compiler_params=_cparams(("parallel", "parallel")),
        name="gated_deltanet",
    )(a_log, dt_bias, p, p, p, p, small, s_row, conv_w, norm_g.reshape(1, -1))


def _merge_kernel(ya_ref, yg_ref, yd_ref, ga_ref, gg_ref, gd_ref, wb_ref, bg_ref, o_ref):
    acc = None
    for i, (y_ref, g_ref) in enumerate(((ya_ref, ga_ref), (yg_ref, gg_ref), (yd_ref, gd_ref))):
        gate = _sigmoid(g_ref[...].astype(F32) + bg_ref[i])
        term = gate * jnp.dot(y_ref[...], wb_ref[i], preferred_element_type=F32)
        acc = term if acc is None else acc + term
    o_ref[...] = acc.astype(o_ref.dtype)


def _merge(ya, yg, yd, p, w_branch, b_gate, rows):
    bsz = p.shape[0]
    d = D_MODEL
    tm = _pick(rows, (768, 512, 384, 256, 128))
    tn = 512
    g0 = OFF_GATE_MAIN // tn
    y_spec = pl.BlockSpec((None, tm, BRANCH_W), lambda b, i, j: (b, i, 0))
    gate_spec = lambda k: pl.BlockSpec((None, tm, tn), lambda b, i, j: (b, i, g0 + k * (d // tn) + j))
    return pl.pallas_call(
        _merge_kernel,
        grid=(bsz, rows // tm, d // tn),
        in_specs=[y_spec, y_spec, y_spec, gate_spec(0), gate_spec(1), gate_spec(2),
                  pl.BlockSpec((N_BRANCH, BRANCH_W, tn), lambda b, i, j: (0, 0, j)),
                  pl.BlockSpec((N_BRANCH, 1, tn), lambda b, i, j: (0, 0, j))],
        out_specs=pl.BlockSpec((None, tm, tn), lambda b, i, j: (b, i, j)),
        out_shape=jax.ShapeDtypeStruct((bsz, rows, d), BF16),
        compiler_params=_cparams(("parallel", "parallel", "arbitrary")),
        name="merge_branches",
    )(ya, yg, yd, p, p, p, w_branch, b_gate.reshape(N_BRANCH, 1, d))


def _outproj_kernel(z_ref, w_ref, x_ref, gl_ref, gc_ref, o_ref, *, tm, seq):
    acc = jnp.dot(z_ref[...], w_ref[...], preferred_element_type=F32)
    row = pl.program_id(1) * tm + lax.broadcasted_iota(jnp.int32, (tm, 1), 0)
    gate = jnp.where(row >= seq, gc_ref[...], gl_ref[...])
    o_ref[...] = x_ref[...] + gate * acc


def _out_proj_residual(z, w_out, xu, gate_l, gate_c, seq, rows):
    bsz, t, d = xu.shape
    tm = _pick(rows, (768, 512, 384, 256, 128))
    tn = 512
    return pl.pallas_call(
        functools.partial(_outproj_kernel, tm=tm, seq=seq),
        grid=(bsz, rows // tm, d // tn),
        in_specs=[
            pl.BlockSpec((None, tm, d), lambda b, i, j: (b, i, 0)),
            pl.BlockSpec((d, tn), lambda b, i, j: (0, j)),
            pl.BlockSpec((None, tm, tn), lambda b, i, j: (b, i, j)),
            pl.BlockSpec((None, 1, tn), lambda b, i, j: (b, 0, j)),
            pl.BlockSpec((1, tn), lambda b, i, j: (0, j)),
        ],
        out_specs=pl.BlockSpec((None, tm, tn), lambda b, i, j: (b, i, j)),
        out_shape=jax.ShapeDtypeStruct((bsz, rows, d), F32),
        compiler_params=_cparams(("parallel", "parallel", "arbitrary")),
        name="out_proj_residual",
    )(z, w_out, xu, gate_l, gate_c)


W1_BLOCK = 2 * LANES


def _w1_prep_kernel(w_ref, perm_ref, o_ref):
    w = w_ref[...].astype(BF16)
    for blk in range(w.shape[1] // W1_BLOCK):
        cs = slice(blk * W1_BLOCK, (blk + 1) * W1_BLOCK)
        o_ref[:, cs] = jnp.dot(w[:, cs], perm_ref[...], preferred_element_type=F32).astype(o_ref.dtype)


def _w1_prep(w_e1):
    nl, ne, d, n = w_e1.shape
    tk = 512
    j = jnp.arange(W1_BLOCK)
    src = jnp.where(j < LANES, 2 * j, 2 * (j - LANES) + 1)
    perm = (jnp.arange(W1_BLOCK)[:, None] == src[None, :]).astype(BF16)
    return pl.pallas_call(
        _w1_prep_kernel,
        grid=(nl * ne, d // tk),
        in_specs=[pl.BlockSpec((None, tk, n), lambda e, k: (e, k, 0)),
                  pl.BlockSpec((W1_BLOCK, W1_BLOCK), lambda e, k: (0, 0))],
        out_specs=pl.BlockSpec((None, tk, n), lambda e, k: (e, k, 0)),
        out_shape=jax.ShapeDtypeStruct((nl * ne, d, n), BF16),
        compiler_params=_cparams(("parallel", "parallel")),
        name="expert_w1_prep",
    )(w_e1.reshape(nl * ne, d, n), perm)


def _regroup_bias(b_e1):
    ne, n = b_e1.shape
    return jnp.transpose(b_e1.reshape(ne, n // W1_BLOCK, LANES, 2), (0, 1, 3, 2)).reshape(ne, 1, n)


def _expert_kernel(be_ref, bv_ref, x_ref, w1_ref, b1_ref, w2_ref, b2_ref, o_ref, hid_ref):
    i = pl.program_id(0)

    @pl.when(bv_ref[i] > 0)
    def _():
        hgl = jnp.dot(x_ref[...], w1_ref[...], preferred_element_type=F32) + b1_ref[...]
        for blk in range(hgl.shape[1] // W1_BLOCK):
            xg = jnp.minimum(hgl[:, blk * W1_BLOCK:blk * W1_BLOCK + LANES], SWIGLU_LIMIT)
            xl = jnp.clip(hgl[:, blk * W1_BLOCK + LANES:(blk + 1) * W1_BLOCK], -SWIGLU_LIMIT, SWIGLU_LIMIT)
            hid_ref[:, blk * LANES:(blk + 1) * LANES] = (
                xg * _sigmoid(SWIGLU_ALPHA * xg) * (xl + 1.0)).astype(hid_ref.dtype)
        y = jnp.dot(hid_ref[...], w2_ref[...], preferred_element_type=F32) + b2_ref[...]
        o_ref[...] = y.astype(o_ref.dtype)


def _experts(xs, blk_e, blk_valid, w1, b1, w2, b2, e0):
    n_rows, d = xs.shape
    tm = MOE_TM
    ff = EXPERT_FF
    grid_spec = pltpu.PrefetchScalarGridSpec(
        num_scalar_prefetch=2,
        grid=(n_rows // tm,),
        in_specs=[
            pl.BlockSpec((tm, d), lambda i, be, bv: (i, 0)),
            pl.BlockSpec((None, d, 2 * ff), lambda i, be, bv: (e0 + be[i], 0, 0)),
            pl.BlockSpec((None, 1, 2 * ff), lambda i, be, bv: (be[i], 0, 0)),
            pl.BlockSpec((None, ff, d), lambda i, be, bv: (be[i], 0, 0)),
            pl.BlockSpec((None, 1, d), lambda i, be, bv: (be[i], 0, 0)),
        ],
        out_specs=pl.BlockSpec((tm, d), lambda i, be, bv: (i, 0)),
        scratch_shapes=[pltpu.VMEM((tm, ff), BF16)],
    )
    return pl.pallas_call(
        _expert_kernel,
        grid_spec=grid_spec,
        out_shape=jax.ShapeDtypeStruct((n_rows, d), BF16),
        compiler_params=_cparams(("arbitrary",)),
        name="moe_experts",
    )(blk_e, blk_valid, xs, w1, b1, w2, b2)


def _moe(h2, top_i, top_w, w1, b1, w2, b2, e0):
    n_tok, d = h2.shape
    tm = MOE_TM
    n_assign = n_tok * TOP_K
    flat_e = top_i.reshape(n_assign)
    order = jnp.argsort(flat_e)
    sorted_e = flat_e[order]
    counts = jnp.bincount(flat_e, length=N_EXPERTS)
    padded = (counts + tm - 1) // tm * tm
    pad_end = jnp.cumsum(padded)
    pad_start = pad_end - padded
    start = jnp.cumsum(counts) - counts
    dest = (pad_start[sorted_e] + jnp.arange(n_assign) - start[sorted_e]).astype(jnp.int32)
    n_blocks = -(-n_assign // tm) + N_EXPERTS
    n_rows = n_blocks * tm
    row_tok = jnp.zeros(n_rows, jnp.int32).at[dest].set((order // TOP_K).astype(jnp.int32))
    blk_start = jnp.arange(n_blocks) * tm
    blk_valid = (blk_start < pad_end[-1]).astype(jnp.int32)
    blk_e = jnp.minimum(jnp.searchsorted(pad_end, blk_start, side='right'), N_EXPERTS - 1).astype(jnp.int32)
    blk_e = jnp.where(blk_valid > 0, blk_e, blk_e[jnp.maximum(pad_end[-1] // tm - 1, 0)])
    pos = jnp.zeros(n_assign, jnp.int32).at[order].set(dest)
    xs = h2[row_tok]
    y = _experts(xs, blk_e, blk_valid, w1, b1, w2, b2, e0)
    return jnp.sum(y[pos].astype(F32).reshape(n_tok, TOP_K, d) * top_w[:, :, None], axis=1)


def _final_kernel(x_ref, g_ref, o_ref):
    x = x_ref[...]
    o_ref[...] = x * lax.rsqrt(jnp.mean(x * x, axis=-1, keepdims=True) + RMS_EPS) * g_ref[...]


def _final_norm(xu, g, seq):
    bsz, t, d = xu.shape
    tm = _pick(seq, (512, 256, 128))
    return pl.pallas_call(
        _final_kernel,
        grid=(bsz, seq // tm),
        in_specs=[pl.BlockSpec((None, tm, d), lambda b, i: (b, i, 0)),
                  pl.BlockSpec((1, d), lambda b, i: (0, 0))],
        out_specs=pl.BlockSpec((None, tm, d), lambda b, i: (b, i, 0)),
        out_shape=jax.ShapeDtypeStruct((bsz, seq, d), F32),
        compiler_params=_cparams(("parallel", "parallel")),
        name="final_norm",
    )(xu, g.reshape(1, d))


def _layer(xu, mod_l, mod_c, seq, layer_idx, ctx_out, cos_t, sin_t, norm1, w_in, da_lambda, da_subln,
           gm_ln_g, gm_ln_b, gm_ws, gm_bs, dn_conv, dn_a_log, dn_dt_bias, dn_norm, b_gate, w_branch,
           w_out, norm2, w_router, b_router, w1_all, b_e1, w_e2, b_e2):
    bsz, t, d = xu.shape
    rows = t if ctx_out else seq
    lam_init = 0.8 - 0.6 * math.exp(-0.3 * layer_idx)
    ml = [mod_l[:, k:k + 1, :] for k in range(6)]
    mc = [mod_c[k:k + 1, :] for k in range(6)]

    h = _adaln(xu, norm1.reshape(1, d), ml[0], ml[1], mc[0], mc[1], seq)
    w_main = jnp.concatenate([w_in[:, :OFF_SMALL], w_in[:, OFF_GATE:]], axis=1).astype(BF16)
    w_small = jnp.zeros((d, LANES), BF16).at[:, :OFF_GATE - OFF_SMALL].set(
        w_in[:, OFF_SMALL:OFF_GATE].astype(BF16))
    p = _in_proj_main(h, w_main, cos_t, sin_t)
    small = _in_proj_small(h, w_small)

    ya = _diff_attention(p, da_lambda, da_subln, seq, rows, lam_init)
    yg = _spatial_gating(p, gm_ln_g, gm_ln_b, gm_ws, gm_bs, rows)
    yd = _gated_deltanet(p, small, dn_conv, dn_a_log, dn_dt_bias, dn_norm, seq)
    z = _merge(ya, yg, yd, p, w_branch.astype(BF16), b_gate, rows)
    xu = _out_proj_residual(z, w_out.astype(BF16), xu, ml[2], mc[2], seq, rows)

    h2, top_i, top_w = _adaln_router(xu, norm2.reshape(1, d), ml[3], ml[4], mc[3], mc[4],
                                     w_router, b_router, seq, rows)
    n_tok = bsz * rows
    y = _moe(h2.reshape(n_tok, d), top_i.reshape(n_tok, LANES)[:, :TOP_K],
             top_w.reshape(n_tok, LANES)[:, :TOP_K], w1_all, _regroup_bias(b_e1),
             w_e2.astype(BF16), b_e2[:, None, :], layer_idx * N_EXPERTS).reshape(bsz, rows, d)
    gate2 = ml[5] if not ctx_out else jnp.concatenate(
        [jnp.broadcast_to(ml[5], (bsz, seq, d)), jnp.broadcast_to(mc[5][None], (bsz, t - seq, d))], axis=1)
    return xu + gate2 * y


def kernel(x, c, ctx, c_ctx, w_mod, b_mod, norm1, w_in, da_lambda, da_subln, gm_ln_g, gm_ln_b, gm_ws, gm_bs,
           dn_conv, dn_a_log, dn_dt_bias, dn_norm, b_gate, w_branch, w_out, norm2, w_router, b_router,
           w_e1, b_e1, w_e2, b_e2, norm_f):
    bsz, seq, d = x.shape
    n_ctx = ctx.shape[1]
    t = seq + n_ctx
    depth = w_mod.shape[0]
    xu = jnp.concatenate([x, ctx], axis=1)
    r = -(-(bsz + 1) // 8) * 8
    cond = jnp.zeros((r, d), F32).at[:bsz].set(c).at[bsz].set(c_ctx)
    mod = _modulation(cond, w_mod, b_mod).reshape(depth, r, 6, d)
    cos_t, sin_t = _rope_tables(seq, t)
    w1_all = _w1_prep(w_e1)
    for l in range(depth):
        xu = _layer(xu, mod[l, :bsz], mod[l, bsz], seq, l, l < depth - 1, cos_t, sin_t, norm1[l], w_in[l],
                    da_lambda[l], da_subln[l], gm_ln_g[l], gm_ln_b[l], gm_ws[l], gm_bs[l], dn_conv[l],
                    dn_a_log[l], dn_dt_bias[l], dn_norm[l], b_gate[l], w_branch[l], w_out[l], norm2[l],
                    w_router[l], b_router[l], w1_all, b_e1[l], w_e2[l], b_e2[l])
    return _final_norm(xu, norm_f, seq)
```

```python
import functools
import math

import jax
import jax.numpy as jnp
from jax import lax
from jax.experimental import pallas as pl
from jax.experimental.pallas import tpu as pltpu

F32 = jnp.float32
BF16 = jnp.bfloat16

D_MODEL = 2048
GRID_W = 64
RMS_EPS = 1e-6
BRANCH_W = D_MODEL // 2
N_BRANCH = 3
DA_HD = 64
DA_HEADS = BRANCH_W // (2 * DA_HD)
ROPE_THETA = 10000.0
ROPE_PAIRS_AXIS = DA_HD // 4
GM_CHUNK = 128
GM_GW = 128
GM_GROUPS = BRANCH_W // GM_GW
DN_HD = 128
DN_HEADS = BRANCH_W // DN_HD
DN_CHUNK = 64
DN_CONV = 5
N_EXPERTS = 32
TOP_K = 4
EXPERT_FF = D_MODEL // 2
SWIGLU_LIMIT = 7.0
SWIGLU_ALPHA = 1.702

LANES = 128
VMEM_LIMIT = 56 * 1024 * 1024

OFF_DA_Q = 0
OFF_DA_K = 1024
OFF_DA_V = 2048
OFF_GM_U = 3072
OFF_GM_V = 4096
OFF_DN_Q = 5120
OFF_DN_K = 6144
OFF_DN_V = 7168
OFF_DN_Z = 8192
OFF_SMALL = 9216
OFF_GATE = 9248
N_MAIN = 9216 + N_BRANCH * D_MODEL
OFF_GATE_MAIN = 9216

MOE_TM = 512


def _cparams(sem):
    return pltpu.CompilerParams(dimension_semantics=sem, vmem_limit_bytes=VMEM_LIMIT)


def _pick(n, cands):
    for c in cands:
        if n % c == 0:
            return c
    raise ValueError(f"no tile for {n} in {cands}")


def _sigmoid(x):
    return jax.nn.sigmoid(x)


def _silu(x):
    return x * _sigmoid(x)


def _gelu_tanh(x):
    return x * (0.5 * (1.0 + jnp.tanh(0.7978845608028654 * (x + 0.044715 * (x * x * x)))))


def _bdot(a, b):
    return jnp.dot(a.astype(BF16), b.astype(BF16), preferred_element_type=F32)


def _split(a):
    hi = a.astype(BF16)
    lo = (a - hi.astype(F32)).astype(BF16)
    return hi, lo


def _dot3(a, b):
    ah, al = _split(a)
    bh, bl = _split(b)
    return (jnp.dot(ah, bh, preferred_element_type=F32)
            + (jnp.dot(al, bh, preferred_element_type=F32)
               + jnp.dot(ah, bl, preferred_element_type=F32)))


def _mod_kernel(c_ref, w_ref, b_ref, o_ref):
    s = _silu(c_ref[...])
    o_ref[...] = _bdot(s, w_ref[...]) + b_ref[...]


def _modulation(cond, w_mod, b_mod):
    nl, d, n6 = w_mod.shape
    r = cond.shape[0]
    tn = 1024
    return pl.pallas_call(
        _mod_kernel,
        grid=(nl, n6 // tn),
        in_specs=[
            pl.BlockSpec((r, d), lambda l, j: (0, 0)),
            pl.BlockSpec((None, d, tn), lambda l, j: (l, 0, j)),
            pl.BlockSpec((None, 1, tn), lambda l, j: (l, 0, j)),
        ],
        out_specs=pl.BlockSpec((None, r, tn), lambda l, j: (l, 0, j)),
        out_shape=jax.ShapeDtypeStruct((nl, r, n6), F32),
        compiler_params=_cparams(("arbitrary", "arbitrary")),
        name="modulation",
    )(cond, w_mod, b_mod.reshape(nl, 1, n6))


def _adaln_tile(x, g, shl, scl, shc, scc, row0, seq):
    tm = x.shape[0]
    y = x * lax.rsqrt(jnp.mean(x * x, axis=-1, keepdims=True) + RMS_EPS) * g
    row = row0 + lax.broadcasted_iota(jnp.int32, (tm, 1), 0)
    is_ctx = row >= seq
    scale = jnp.where(is_ctx, scc, scl)
    shift = jnp.where(is_ctx, shc, shl)
    return y * (1.0 + scale) + shift


def _adaln_kernel(x_ref, g_ref, shl_ref, scl_ref, shc_ref, scc_ref, o_ref, *, tm, seq):
    h = _adaln_tile(x_ref[...], g_ref[...], shl_ref[...], scl_ref[...], shc_ref[...], scc_ref[...],
                    pl.program_id(1) * tm, seq)
    o_ref[...] = h.astype(o_ref.dtype)


def _mod_specs(d):
    return [
        pl.BlockSpec((1, d), lambda b, i: (0, 0)),
        pl.BlockSpec((None, 1, d), lambda b, i: (b, 0, 0)),
        pl.BlockSpec((None, 1, d), lambda b, i: (b, 0, 0)),
        pl.BlockSpec((1, d), lambda b, i: (0, 0)),
        pl.BlockSpec((1, d), lambda b, i: (0, 0)),
    ]


def _adaln(xu, g, shl, scl, shc, scc, seq):
    bsz, t, d = xu.shape
    tm = _pick(t, (768, 384, 256, 128))
    return pl.pallas_call(
        functools.partial(_adaln_kernel, tm=tm, seq=seq),
        grid=(bsz, t // tm),
        in_specs=[pl.BlockSpec((None, tm, d), lambda b, i: (b, i, 0))] + _mod_specs(d),
        out_specs=pl.BlockSpec((None, tm, d), lambda b, i: (b, i, 0)),
        out_shape=jax.ShapeDtypeStruct((bsz, t, d), BF16),
        compiler_params=_cparams(("parallel", "parallel")),
        name="adaln",
    )(xu, g, shl, scl, shc, scc)


def _adaln_router_kernel(x_ref, g_ref, shl_ref, scl_ref, shc_ref, scc_ref, wr_ref, br_ref,
                         h_ref, idx_ref, wt_ref, *, tm, seq):
    h = _adaln_tile(x_ref[...], g_ref[...], shl_ref[...], scl_ref[...], shc_ref[...], scc_ref[...],
                    pl.program_id(1) * tm, seq)
    h_ref[...] = h.astype(h_ref.dtype)
    logits = _dot3(h, wr_ref[...]) + br_ref[...]
    lane = lax.broadcasted_iota(jnp.int32, logits.shape, 1).astype(F32)
    vals, idxs = [], []
    cur = logits
    for _ in range(TOP_K):
        m = jnp.max(cur, axis=-1, keepdims=True)
        am = jnp.min(jnp.where(cur == m, lane, float(LANES)), axis=-1, keepdims=True)
        vals.append(m)
        idxs.append(am)
        cur = jnp.where(lane == am, -jnp.inf, cur)
    es = [jnp.exp(v - vals[0]) for v in vals]
    tot = es[0] + es[1] + es[2] + es[3]
    wt = jnp.zeros(logits.shape, F32)
    ix = jnp.zeros(logits.shape, F32)
    for k in range(TOP_K):
        wt = jnp.where(lane == k, es[k] / tot, wt)
        ix = jnp.where(lane == k, idxs[k], ix)
    idx_ref[...] = ix.astype(jnp.int32)
    wt_ref[...] = wt


def _adaln_router(xu, g, shl, scl, shc, scc, w_router, b_router, seq, rows):
    bsz, t, d = xu.shape
    tm = _pick(rows, (768, 512, 384, 256, 128))
    wr = jnp.zeros((d, LANES), F32).at[:, :N_EXPERTS].set(w_router)
    br = jnp.full((1, LANES), -1e30, F32).at[0, :N_EXPERTS].set(b_router)
    return pl.pallas_call(
        functools.partial(_adaln_router_kernel, tm=tm, seq=seq),
        grid=(bsz, rows // tm),
        in_specs=[pl.BlockSpec((None, tm, d), lambda b, i: (b, i, 0))] + _mod_specs(d) + [
            pl.BlockSpec((d, LANES), lambda b, i: (0, 0)),
            pl.BlockSpec((1, LANES), lambda b, i: (0, 0)),
        ],
        out_specs=[
            pl.BlockSpec((None, tm, d), lambda b, i: (b, i, 0)),
            pl.BlockSpec((None, tm, LANES), lambda b, i: (b, i, 0)),
            pl.BlockSpec((None, tm, LANES), lambda b, i: (b, i, 0)),
        ],
        out_shape=[
            jax.ShapeDtypeStruct((bsz, rows, d), BF16),
            jax.ShapeDtypeStruct((bsz, rows, LANES), jnp.int32),
            jax.ShapeDtypeStruct((bsz, rows, LANES), F32),
        ],
        compiler_params=_cparams(("parallel", "parallel")),
        name="adaln_router",
    )(xu, g, shl, scl, shc, scc, wr, br)


def _inproj_kernel(h_ref, w_ref, cos_ref, sin_ref, o_ref, *, tn, n_rope_tiles, n_q_tiles):
    j = pl.program_id(1)
    acc = jnp.dot(h_ref[...], w_ref[...], preferred_element_type=F32)

    @pl.when(j >= n_rope_tiles)
    def _():
        o_ref[...] = acc.astype(o_ref.dtype)

    @pl.when(j < n_rope_tiles)
    def _():
        scale = jnp.where(j < n_q_tiles, DA_HD ** -0.5, 1.0).astype(F32)
        cos = cos_ref[...] * scale
        sin = sin_ref[...] * scale
        lane = lax.broadcasted_iota(jnp.int32, cos.shape, 1)
        first = (lane % DA_HD) < (DA_HD // 2)
        for c in range(tn // LANES):
            a = acc[:, c * LANES:(c + 1) * LANES]
            sw = jnp.where(first, pltpu.roll(a, LANES - DA_HD // 2, 1), pltpu.roll(a, DA_HD // 2, 1))
            o_ref[:, c * LANES:(c + 1) * LANES] = (a * cos + sw * sin).astype(o_ref.dtype)


def _in_proj_main(h, w_main, cos_t, sin_t):
    bsz, t, d = h.shape
    n = w_main.shape[1]
    tn = 512
    return pl.pallas_call(
        functools.partial(_inproj_kernel, tn=tn, n_rope_tiles=OFF_DA_V // tn, n_q_tiles=OFF_DA_K // tn),
        grid=(bsz, n // tn),
        in_specs=[
            pl.BlockSpec((None, t, d), lambda b, j: (b, 0, 0)),
            pl.BlockSpec((d, tn), lambda b, j: (0, j)),
            pl.BlockSpec((t, LANES), lambda b, j: (0, 0)),
            pl.BlockSpec((t, LANES), lambda b, j: (0, 0)),
        ],
        out_specs=pl.BlockSpec((None, t, tn), lambda b, j: (b, 0, j)),
        out_shape=jax.ShapeDtypeStruct((bsz, t, n), BF16),
        compiler_params=_cparams(("parallel", "arbitrary")),
        name="in_proj",
    )(h, w_main, cos_t, sin_t)


def _mm_kernel(x_ref, w_ref, o_ref):
    o_ref[...] = jnp.dot(x_ref[...], w_ref[...], preferred_element_type=F32).astype(o_ref.dtype)


def _in_proj_small(h, w_small):
    bsz, t, d = h.shape
    return pl.pallas_call(
        _mm_kernel,
        grid=(bsz,),
        in_specs=[
            pl.BlockSpec((None, t, d), lambda b: (b, 0, 0)),
            pl.BlockSpec((d, LANES), lambda b: (0, 0)),
        ],
        out_specs=pl.BlockSpec((None, t, LANES), lambda b: (b, 0, 0)),
        out_shape=jax.ShapeDtypeStruct((bsz, t, LANES), F32),
        compiler_params=_cparams(("parallel",)),
        name="in_proj_small",
    )(h, w_small)


def _rope_tables(seq, t):
    rows = seq // GRID_W
    row = jnp.repeat(jnp.arange(rows, dtype=F32), GRID_W)
    col = jnp.tile(jnp.arange(GRID_W, dtype=F32), rows)
    inv = ROPE_THETA ** (-jnp.arange(ROPE_PAIRS_AXIS, dtype=F32) / ROPE_PAIRS_AXIS)
    ang = jnp.concatenate([row[:, None] * inv, col[:, None] * inv], axis=-1)
    cos, sin = jnp.cos(ang), jnp.sin(ang)
    cos_t = jnp.tile(cos, (1, LANES // (DA_HD // 2)))
    sin_t = jnp.tile(jnp.concatenate([-sin, sin], axis=-1), (1, LANES // DA_HD))
    pad = t - seq
    cos_t = jnp.concatenate([cos_t, jnp.ones((pad, LANES), F32)], axis=0)
    sin_t = jnp.concatenate([sin_t, jnp.zeros((pad, LANES), F32)], axis=0)
    return cos_t, sin_t


ATTN_ROW_GROUPS = 4


def _attn_kernel(lam_ref, g_ref, q_ref, k_ref, v_ref, o_ref, *, seq, tq, lam_init):
    qi = pl.program_id(2)
    lp = lam_ref[...]
    l1 = jnp.sum(lp[0:1] * lp[1:2], axis=-1, keepdims=True)
    l2 = jnp.sum(lp[2:3] * lp[3:4], axis=-1, keepdims=True)
    lam = jnp.exp(l1) - jnp.exp(l2) + lam_init
    q = q_ref[...].astype(F32)
    lane = lax.broadcasted_iota(jnp.int32, q.shape, 1)
    qq = jnp.concatenate([jnp.where(lane < DA_HD, q, 0.0), jnp.where(lane >= DA_HD, q, 0.0)],
                         axis=0).astype(BF16)

    def core(k, v):
        rs = 2 * tq // ATTN_ROW_GROUPS
        scores = [lax.dot_general(qq[i * rs:(i + 1) * rs], k, (((1,), (1,)), ((), ())),
                                  preferred_element_type=F32) for i in range(ATTN_ROW_GROUPS)]
        outs = []
        for s in scores:
            m = jnp.max(s, axis=-1, keepdims=True)
            p = jnp.exp(s - m)
            den = jnp.sum(p, axis=-1, keepdims=True)
            outs.append(jnp.dot(p.astype(BF16), v, preferred_element_type=F32) / den)
        o = jnp.concatenate(outs, axis=0)
        o = o[:tq] - lam * o[tq:]
        y = o * lax.rsqrt(jnp.mean(o * o, axis=-1, keepdims=True) + RMS_EPS) * g_ref[...]
        o_ref[...] = (y * (1.0 - lam_init)).astype(o_ref.dtype)

    @pl.when(qi * tq < seq)
    def _():
        core(k_ref[...], v_ref[...])

    @pl.when(qi * tq >= seq)
    def _():
        core(k_ref[seq:, :], v_ref[seq:, :])


def _diff_attention(p, lam_params, subln_g, seq, rows, lam_init):
    bsz, t, _ = p.shape
    tq = _pick(math.gcd(seq, t - seq), (256, 128))
    cq, ck, cv = OFF_DA_Q // LANES, OFF_DA_K // LANES, OFF_DA_V // LANES
    return pl.pallas_call(
        functools.partial(_attn_kernel, seq=seq, tq=tq, lam_init=lam_init),
        grid=(bsz, DA_HEADS, rows // tq),
        in_specs=[
            pl.BlockSpec((4, DA_HD), lambda b, h, i: (0, 0)),
            pl.BlockSpec((1, 2 * DA_HD), lambda b, h, i: (0, 0)),
            pl.BlockSpec((None, tq, LANES), lambda b, h, i: (b, i, cq + h)),
            pl.BlockSpec((None, t, LANES), lambda b, h, i: (b, 0, ck + h)),
            pl.BlockSpec((None, t, LANES), lambda b, h, i: (b, 0, cv + h)),
        ],
        out_specs=pl.BlockSpec((None, tq, LANES), lambda b, h, i: (b, i, h)),
        out_shape=jax.ShapeDtypeStruct((bsz, rows, BRANCH_W), BF16),
        compiler_params=_cparams(("parallel", "parallel", "arbitrary")),
        name="diff_attention",
    )(lam_params, subln_g.reshape(1, -1), p, p, p)


def _gmlp_kernel(u_ref, v_ref, lng_ref, lnb_ref, ws_ref, bs_ref, o_ref, *, nchunks):
    for c in range(nchunks):
        r0 = c * GM_CHUNK
        u = _gelu_tanh(u_ref[r0:r0 + GM_CHUNK, :].astype(F32))
        v = _gelu_tanh(v_ref[r0:r0 + GM_CHUNK, :].astype(F32))
        xc = v - jnp.mean(v, axis=-1, keepdims=True)
        var = jnp.mean(xc * xc, axis=-1, keepdims=True)
        vn = (xc * lax.rsqrt(var + RMS_EPS) * lng_ref[...] + lnb_ref[...]).astype(BF16)
        for g in range(GM_GROUPS):
            cs = slice(g * GM_GW, (g + 1) * GM_GW)
            s = jnp.dot(ws_ref[g], vn[:, cs], preferred_element_type=F32) + bs_ref[g]
            o_ref[r0:r0 + GM_CHUNK, cs] = (u[:, cs] * s).astype(o_ref.dtype)


def _spatial_gating(p, ln_g, ln_b, ws, bs, rows):
    bsz, t, _ = p.shape
    tm = _pick(rows, (768, 512, 384, 256, 128))
    cu, cv = OFF_GM_U // BRANCH_W, OFF_GM_V // BRANCH_W
    bs_b = jnp.broadcast_to(bs[:, :, None], (GM_GROUPS, GM_CHUNK, GM_GW)).astype(F32)
    return pl.pallas_call(
        functools.partial(_gmlp_kernel, nchunks=tm // GM_CHUNK),
        grid=(bsz, rows // tm),
        in_specs=[
            pl.BlockSpec((None, tm, BRANCH_W), lambda b, i: (b, i, cu)),
            pl.BlockSpec((None, tm, BRANCH_W), lambda b, i: (b, i, cv)),
            pl.BlockSpec((1, BRANCH_W), lambda b, i: (0, 0)),
            pl.BlockSpec((1, BRANCH_W), lambda b, i: (0, 0)),
            pl.BlockSpec((GM_GROUPS, GM_CHUNK, GM_CHUNK), lambda b, i: (0, 0, 0)),
            pl.BlockSpec((GM_GROUPS, GM_CHUNK, GM_GW), lambda b, i: (0, 0, 0)),
        ],
        out_specs=pl.BlockSpec((None, tm, BRANCH_W), lambda b, i: (b, i, 0)),
        out_shape=jax.ShapeDtypeStruct((bsz, rows, BRANCH_W), BF16),
        compiler_params=_cparams(("parallel", "parallel")),
        name="spatial_gating",
    )(p, p, ln_g.reshape(1, -1), ln_b.reshape(1, -1), ws.astype(BF16), bs_b)


DN_BASE = 8
DN_PREP_GROUPS = (4, 6, 3, 2, 1)
DN_HEAD_GROUP = 2


def _dn_kernel(alog_ref, dtb_ref, q_ref, k_ref, v_ref, z_ref, sm_ref, ar_ref, cw_ref, ng_ref, o_ref,
               qn_ref, kn_ref, vn_ref, rowg_ref, u_ref, wq_ref, qk2_ref, cd_ref, oacc_ref, st_ref,
               *, seq, t):
    hg = DN_HEAD_GROUP
    hblk = pl.program_id(1)
    nc = t // DN_CHUNK
    n_lat = seq // DN_CHUNK
    n_ctx = nc - n_lat
    hw = DN_HD
    cw = DN_CHUNK

    row = lax.broadcasted_iota(jnp.int32, (t, 1), 0)
    seg_lo = jnp.where(row < seq, 0, seq)
    seg_hi = jnp.where(row < seq, seq, t)

    def conv_silu(x_ref, w):
        x = x_ref[...].astype(F32)
        acc = x * w[DN_CONV // 2:DN_CONV // 2 + 1, :]
        for s in (-2, -1, 1, 2):
            xs = pltpu.roll(x, (-s) % t, 0)
            rs = row + s
            ok = (rs >= seg_lo) & (rs < seg_hi)
            acc = acc + jnp.where(ok, xs, 0.0) * w[DN_CONV // 2 + s:DN_CONV // 2 + s + 1, :]
        return _silu(acc)

    def l2n(x):
        return x * lax.rsqrt(jnp.sum(x * x, axis=-1, keepdims=True) + RMS_EPS)

    qc = conv_silu(q_ref, cw_ref[0])
    kc = conv_silu(k_ref, cw_ref[1])
    vn_ref[...] = conv_silu(v_ref, cw_ref[2])
    for j in range(hg):
        cs = slice(j * hw, (j + 1) * hw)
        qn_ref[:, cs] = l2n(qc[:, cs]) * (DN_HD ** -0.5)
        kn_ref[:, cs] = l2n(kc[:, cs])

    def softplus(x):
        return jnp.maximum(x, 0.0) + jnp.log1p(jnp.exp(-jnp.abs(x)))

    rw = 2 * hg * cw
    lane_r = lax.broadcasted_iota(jnp.int32, (1, rw), 1)
    chain_r = lane_r // cw
    pos_r = lane_r % cw
    alog_r = jnp.zeros((1, rw), F32)
    dt_r = jnp.zeros((1, rw), F32)
    for d in range(2):
        for j in range(hg):
            alog_r = jnp.where(chain_r == hg * d + j, alog_ref[d, hblk * hg + j], alog_r)
            dt_r = jnp.where(chain_r == hg * d + j, dtb_ref[d, hblk * hg + j], dt_r)
    g_all = -jnp.exp(alog_r) * softplus(ar_ref[...].reshape(nc * 8, rw) + dt_r)
    pre = g_all
    suf = g_all
    sh = 1
    while sh < cw:
        pre = pre + jnp.where(pos_r >= sh, pltpu.roll(pre, sh, 1), 0.0)
        suf = suf + jnp.where(pos_r < cw - sh, pltpu.roll(suf, rw - sh, 1), 0.0)
        sh *= 2
    run = jnp.where(lane_r >= hg * cw, suf, pre).reshape(nc, 8, rw)
    tot = (pre + suf - g_all).reshape(nc, 8, rw)
    sub = lax.broadcasted_iota(jnp.int32, (nc, 8, rw), 1)
    both = jnp.where(sub == 0, run, tot)
    for d in range(2):
        rowg_ref[d] = both[:, :, d * hg * cw:(d + 1) * hg * cw]

    st_ref[...] = jnp.zeros(st_ref.shape, F32)

    pshape = (cw, hg * cw)
    ii = lax.broadcasted_iota(jnp.int32, pshape, 0)
    lp = lax.broadcasted_iota(jnp.int32, pshape, 1)
    jl = lp % cw
    left = lp < cw
    diag = ii == jl
    eye_p = jnp.where(diag, 1.0, 0.0).astype(F32)
    blk_base = (ii // DN_BASE) == (jl // DN_BASE)
    incl = [ii >= jl, ii <= jl]
    strict = [ii > jl, ii < jl]
    half = [jnp.where(left, 1.0, 0.0).astype(BF16), jnp.where(left, 0.0, 1.0).astype(BF16)]
    left_sq = lax.broadcasted_iota(jnp.int32, (LANES, LANES), 1) < cw

    def blockdiag(b16):
        return jnp.concatenate([b16 * half[0], b16 * half[1]], axis=0)

    def pprod(a, b):
        return jnp.dot(a.astype(BF16), blockdiag(b.astype(BF16)), preferred_element_type=F32)

    def tri_inverse(lmats):
        ms = [jnp.where(blk_base, -l, 0.0) for l in lmats]
        xs = [eye_p + m for m in ms]
        pws = [pprod(m, m) for m in ms]
        span = 4
        while span <= DN_BASE:
            tts = [pprod(jnp.concatenate([x, pw], axis=0), pw) for x, pw in zip(xs, pws)]
            xs = [x + tt[:cw] for x, tt in zip(xs, tts)]
            pws = [tt[cw:] for tt in tts]
            span *= 2
        bs = DN_BASE
        while bs < cw:
            off = ((ii // (2 * bs)) == (jl // (2 * bs))) & ((ii // bs) != (jl // bs))
            cmats = [jnp.where(off, l, 0.0) for l in lmats]
            ys = [pprod(x, c) for x, c in zip(xs, cmats)]
            zs = [pprod(y, x) for y, x in zip(ys, xs)]
            xs = [x - z for x, z in zip(xs, zs)]
            bs *= 2
        return xs

    def prep_load(c):
        rows = pl.ds(pl.multiple_of(c * cw, cw), cw)
        kk = [kn_ref[rows, j * hw:(j + 1) * hw] for j in range(hg)]
        qq = [qn_ref[rows, j * hw:(j + 1) * hw] for j in range(hg)]
        vv = [vn_ref[rows, j * hw:(j + 1) * hw] for j in range(hg)]
        return kk, qq, vv, sm_ref[rows, :], [rowg_ref[d, c] for d in range(2)]

    def prep_compute(loaded):
        n = len(loaded)
        gram, qk, ktp = [], [], []
        for kk, qq, vv, sm, rgs in loaded:
            gq, kt = [], []
            for j in range(hg):
                kb = kk[j].astype(BF16)
                gq.append(lax.dot_general(jnp.concatenate([kb, qq[j].astype(BF16)], axis=0),
                                          jnp.concatenate([kb, kb], axis=0), (((1,), (1,)), ((), ())),
                                          preferred_element_type=F32))
                kt.append(jnp.concatenate([kk[j], kk[j]], axis=0).T)
            pair = jnp.where(left_sq, gq[0], gq[1])
            gram.append(pair[:cw])
            qk.append(pair[cw:])
            ktp.append(jnp.where(left_sq, kt[0], kt[1]))
        pre = []
        for ci, (kk, qq, vv, sm, rgs) in enumerate(loaded):
            lane_c = lax.broadcasted_iota(jnp.int32, sm.shape, 1)
            for d in range(2):
                bcol = [_sigmoid(jnp.sum(jnp.where(lane_c == d * DN_HEADS + hblk * hg + j, sm, 0.0),
                                         axis=1, keepdims=True)) for j in range(hg)]
                gc_row, g_tot = rgs[d][0:1, :], rgs[d][1:2, :]
                gdiag = jnp.where(diag, gc_row, 0.0)
                gcol = [jnp.sum(jnp.where(left, gdiag, 0.0), axis=1, keepdims=True),
                        jnp.sum(jnp.where(left, 0.0, gdiag), axis=1, keepdims=True)]
                gc = jnp.where(left, gcol[0], gcol[1])
                beta = jnp.where(left, bcol[0], bcol[1])
                dec = jnp.exp(jnp.where(incl[d], gc - gc_row, -jnp.inf))
                lmat = jnp.where(strict[d], beta * gram[ci] * dec, 0.0)
                pre.append((ci, d, bcol, gcol, gc_row, g_tot, dec, lmat))
        tinvs = tri_inverse([p[-1] for p in pre])
        rhss, egs = [], []
        for ci, d, bcol, gcol, gc_row, g_tot, dec, lmat in pre:
            kk, qq, vv = loaded[ci][:3]
            eg = [jnp.exp(gcol[j]) for j in range(hg)]
            egs.append(eg)
            rhss.append(jnp.concatenate(
                [jnp.concatenate([vv[j] * bcol[j], kk[j] * (bcol[j] * eg[j])], axis=1) for j in range(hg)],
                axis=0).astype(BF16))
        sols = []
        for tinv, rhs in zip(tinvs, rhss):
            t16 = tinv.astype(BF16)
            sols.append(jnp.dot(jnp.concatenate([t16 * half[0], t16 * half[1]], axis=0), rhs,
                                preferred_element_type=F32))
        outs = [[None, None] for _ in range(n)]
        for (ci, d, bcol, gcol, gc_row, g_tot, dec, lmat), sol, eg in zip(pre, sols, egs):
            qq = loaded[ci][1]
            e_tot = jnp.exp(g_tot)
            q_intra = jnp.where(incl[d], qk[ci] * dec, 0.0)
            outs[ci][d] = dict(
                u=[sol[j * cw:(j + 1) * cw, :hw] for j in range(hg)],
                wq=[jnp.concatenate([sol[j * cw:(j + 1) * cw, hw:], qq[j] * eg[j]], axis=0).astype(BF16)
                    for j in range(hg)],
                cd=[jnp.broadcast_to(e_tot[:, j * cw:j * cw + 1], (1, hw)) for j in range(hg)],
                qk2=jnp.concatenate([q_intra, ktp[ci] * jnp.exp(g_tot - gc_row)], axis=0).astype(BF16))
        return outs

    def prep_store(c, outs):
        rows = pl.ds(pl.multiple_of(c * cw, cw), cw)
        for d in range(2):
            for j in range(hg):
                u_ref[hg * d + j, rows, :] = outs[d]["u"][j]
                wq_ref[hg * d + j, c] = outs[d]["wq"][j]
                cd_ref[hg * d + j, c] = outs[d]["cd"][j]
            qk2_ref[d, c] = outs[d]["qk2"]

    group = _pick(nc, DN_PREP_GROUPS)

    def prep_body(g, carry):
        cs = [g * group + cc for cc in range(group)]
        loaded = [prep_load(c) for c in cs]
        outs = prep_compute(loaded)
        for c, o in zip(cs, outs):
            prep_store(c, o)
        return carry

    lax.fori_loop(0, nc // group, prep_body, 0)

    def scan_body(i, carry):
        cf = jnp.where(i < n_ctx, i + n_lat, i - n_ctx)
        cb = nc - 1 - i
        dirs = ((0, cf), (1, cb))
        rows = [pl.ds(pl.multiple_of(c * cw, cw), cw) for _, c in dirs]
        state = [st_ref[s] for s in range(2 * hg)]
        wq = [wq_ref[hg * d + j, c] for d, c in dirs for j in range(hg)]
        u = [u_ref[hg * d + j, rows[d], :] for d, _ in dirs for j in range(hg)]
        cd = [cd_ref[hg * d + j, c] for d, c in dirs for j in range(hg)]
        qk2 = [qk2_ref[d, c] for d, c in dirs]
        ws = [jnp.dot(wq[s], state[s].astype(BF16), preferred_element_type=F32)
              for s in range(2 * hg)]
        vb = [(u[s] - ws[s][:cw]).astype(BF16) for s in range(2 * hg)]
        zero = jnp.zeros_like(vb[0])
        ov = [jnp.dot(qk2[d], jnp.concatenate([jnp.concatenate([vb[hg * d], zero], axis=1),
                                               jnp.concatenate([zero, vb[hg * d + 1]], axis=1)], axis=0),
                      preferred_element_type=F32) for d in range(2)]
        o_new = [jnp.concatenate([ws[hg * d + j][cw:] + ov[d][:cw, j * hw:(j + 1) * hw] for j in range(hg)], axis=1)
                 for d in range(2)]
        st_new = [state[hg * d + j] * cd[hg * d + j] + ov[d][cw:, j * hw:(j + 1) * hw]
                  for d in range(2) for j in range(hg)]
        for d, _ in dirs:
            oacc_ref[d, rows[d], :] = o_new[d]
        for s in range(2 * hg):
            st_ref[s] = st_new[s]
        return carry

    lax.fori_loop(0, nc, scan_body, 0)

    o = oacc_ref[0] + oacc_ref[1]
    for j in range(hg):
        cs = slice(j * hw, (j + 1) * hw)
        oj = o[:, cs]
        y = oj * lax.rsqrt(jnp.mean(oj * oj, axis=-1, keepdims=True) + RMS_EPS) * ng_ref[...]
        o_ref[:, cs] = (y * _silu(z_ref[:, cs].astype(F32))).astype(o_ref.dtype)


def _gated_deltanet(p, small, conv_w, a_log, dt_bias, norm_g, seq):
    bsz, t, _ = p.shape
    nc = t // DN_CHUNK
    hg = DN_HEAD_GROUP
    assert hg == 2
    w = hg * DN_HD
    rw = 2 * hg * DN_CHUNK
    a = small[..., 2 * DN_HEADS:4 * DN_HEADS].reshape(bsz, nc, DN_CHUNK, 2, DN_HEADS // hg, hg)
    a_row = jnp.transpose(a, (0, 4, 1, 3, 5, 2)).reshape(bsz, DN_HEADS // hg, nc, 1, rw)
    a_row = jnp.broadcast_to(a_row, (bsz, DN_HEADS // hg, nc, 8, rw))
    cq, ck, cv, cz = (OFF_DN_Q // w, OFF_DN_K // w, OFF_DN_V // w, OFF_DN_Z // w)
    slab = lambda c0: pl.BlockSpec((None, t, w), lambda b, h: (b, 0, c0 + h))
    smem = pl.BlockSpec(memory_space=pltpu.SMEM)
    return pl.pallas_call(
        functools.partial(_dn_kernel, seq=seq, t=t),
        grid=(bsz, DN_HEADS // hg),
        in_specs=[
            smem, smem,
            slab(cq), slab(ck), slab(cv), slab(cz),
            pl.BlockSpec((None, t, LANES), lambda b, h: (b, 0, 0)),
            pl.BlockSpec((None, None, nc, 8, rw), lambda b, h: (b, h, 0, 0, 0)),
            pl.BlockSpec((3, DN_CONV, w), lambda b, h: (0, 0, h)),
            pl.BlockSpec((1, DN_HD), lambda b, h: (0, 0)),
        ],
        out_specs=pl.BlockSpec((None, t, w), lambda b, h: (b, 0, h)),
        out_shape=jax.ShapeDtypeStruct((bsz, t, BRANCH_W), BF16),
        scratch_shapes=[
            pltpu.VMEM((t, w), F32), pltpu.VMEM((t, w), F32), pltpu.VMEM((t, w), F32),
            pltpu.VMEM((2, nc, 8, hg * DN_CHUNK), F32),
            pltpu.VMEM((2 * hg, t, DN_HD), F32),
            pltpu.VMEM((2 * hg, nc, 2 * DN_CHUNK, DN_HD), BF16),
            pltpu.VMEM((2, nc, DN_CHUNK + DN_HD, hg * DN_CHUNK), BF16),
            pltpu.VMEM((2 * hg, nc, 1, DN_HD), F32),
            pltpu.VMEM((2, t, w), F32),
            pltpu.VMEM((2 * hg, DN_HD, DN_HD), F32),
        ],
        compiler_params=_cparams(("parallel", "parallel")),
        name="gated_deltanet",
    )(a_log, dt_bias, p, p, p, p, small, a_row, conv_w, norm_g.reshape(1, -1))


def _merge_kernel(ya_ref, yg_ref, yd_ref, ga_ref, gg_ref, gd_ref, wb_ref, bg_ref, o_ref):
    acc = None
    for i, (y_ref, g_ref) in enumerate(((ya_ref, ga_ref), (yg_ref, gg_ref), (yd_ref, gd_ref))):
        gate = _sigmoid(g_ref[...].astype(F32) + bg_ref[i])
        term = gate * jnp.dot(y_ref[...], wb_ref[i], preferred_element_type=F32)
        acc = term if acc is None else acc + term
    o_ref[...] = acc.astype(o_ref.dtype)


def _merge(ya, yg, yd, p, w_branch, b_gate, rows):
    bsz = p.shape[0]
    d = D_MODEL
    tm = _pick(rows, (768, 512, 384, 256, 128))
    tn = 512
    g0 = OFF_GATE_MAIN // tn
    y_spec = pl.BlockSpec((None, tm, BRANCH_W), lambda b, i, j: (b, i, 0))
    gate_spec = lambda k: pl.BlockSpec((None, tm, tn), lambda b, i, j: (b, i, g0 + k * (d // tn) + j))
    return pl.pallas_call(
        _merge_kernel,
        grid=(bsz, rows // tm, d // tn),
        in_specs=[y_spec, y_spec, y_spec, gate_spec(0), gate_spec(1), gate_spec(2),
                  pl.BlockSpec((N_BRANCH, BRANCH_W, tn), lambda b, i, j: (0, 0, j)),
                  pl.BlockSpec((N_BRANCH, 1, tn), lambda b, i, j: (0, 0, j))],
        out_specs=pl.BlockSpec((None, tm, tn), lambda b, i, j: (b, i, j)),
        out_shape=jax.ShapeDtypeStruct((bsz, rows, d), BF16),
        compiler_params=_cparams(("parallel", "parallel", "arbitrary")),
        name="merge_branches",
    )(ya, yg, yd, p, p, p, w_branch, b_gate.reshape(N_BRANCH, 1, d))


def _outproj_kernel(z_ref, w_ref, x_ref, gl_ref, gc_ref, o_ref, *, tm, seq):
    acc = jnp.dot(z_ref[...], w_ref[...], preferred_element_type=F32)
    row = pl.program_id(1) * tm + lax.broadcasted_iota(jnp.int32, (tm, 1), 0)
    gate = jnp.where(row >= seq, gc_ref[...], gl_ref[...])
    o_ref[...] = x_ref[...] + gate * acc


def _out_proj_residual(z, w_out, xu, gate_l, gate_c, seq, rows):
    bsz, t, d = xu.shape
    tm = _pick(rows, (768, 512, 384, 256, 128))
    tn = 512
    return pl.pallas_call(
        functools.partial(_outproj_kernel, tm=tm, seq=seq),
        grid=(bsz, rows // tm, d // tn),
        in_specs=[
            pl.BlockSpec((None, tm, d), lambda b, i, j: (b, i, 0)),
            pl.BlockSpec((d, tn), lambda b, i, j: (0, j)),
            pl.BlockSpec((None, tm, tn), lambda b, i, j: (b, i, j)),
            pl.BlockSpec((None, 1, tn), lambda b, i, j: (b, 0, j)),
            pl.BlockSpec((1, tn), lambda b, i, j: (0, j)),
        ],
        out_specs=pl.BlockSpec((None, tm, tn), lambda b, i, j: (b, i, j)),
        out_shape=jax.ShapeDtypeStruct((bsz, rows, d), F32),
        compiler_params=_cparams(("parallel", "parallel", "arbitrary")),
        name="out_proj_residual",
    )(z, w_out, xu, gate_l, gate_c)


W1_BLOCK = 2 * LANES


def _w1_prep_kernel(w_ref, perm_ref, o_ref):
    w = w_ref[...].astype(BF16)
    for blk in range(w.shape[1] // W1_BLOCK):
        cs = slice(blk * W1_BLOCK, (blk + 1) * W1_BLOCK)
        o_ref[:, cs] = jnp.dot(w[:, cs], perm_ref[...], preferred_element_type=F32).astype(o_ref.dtype)


def _w1_prep(w_e1):
    nl, ne, d, n = w_e1.shape
    tk = 512
    j = jnp.arange(W1_BLOCK)
    src = jnp.where(j < LANES, 2 * j, 2 * (j - LANES) + 1)
    perm = (jnp.arange(W1_BLOCK)[:, None] == src[None, :]).astype(BF16)
    return pl.pallas_call(
        _w1_prep_kernel,
        grid=(nl * ne, d // tk),
        in_specs=[pl.BlockSpec((None, tk, n), lambda e, k: (e, k, 0)),
                  pl.BlockSpec((W1_BLOCK, W1_BLOCK), lambda e, k: (0, 0))],
        out_specs=pl.BlockSpec((None, tk, n), lambda e, k: (e, k, 0)),
        out_shape=jax.ShapeDtypeStruct((nl * ne, d, n), BF16),
        compiler_params=_cparams(("parallel", "parallel")),
        name="expert_w1_prep",
    )(w_e1.reshape(nl * ne, d, n), perm)


def _regroup_bias(b_e1):
    ne, n = b_e1.shape
    return jnp.transpose(b_e1.reshape(ne, n // W1_BLOCK, LANES, 2), (0, 1, 3, 2)).reshape(ne, 1, n)


def _expert_kernel(be_ref, bv_ref, x_ref, w1_ref, b1_ref, w2_ref, b2_ref, o_ref, hid_ref):
    i = pl.program_id(0)

    @pl.when(bv_ref[i] > 0)
    def _():
        hgl = jnp.dot(x_ref[...], w1_ref[...], preferred_element_type=F32) + b1_ref[...]
        for blk in range(hgl.shape[1] // W1_BLOCK):
            xg = jnp.minimum(hgl[:, blk * W1_BLOCK:blk * W1_BLOCK + LANES], SWIGLU_LIMIT)
            xl = jnp.clip(hgl[:, blk * W1_BLOCK + LANES:(blk + 1) * W1_BLOCK], -SWIGLU_LIMIT, SWIGLU_LIMIT)
            hid_ref[:, blk * LANES:(blk + 1) * LANES] = (
                xg * _sigmoid(SWIGLU_ALPHA * xg) * (xl + 1.0)).astype(hid_ref.dtype)
        y = jnp.dot(hid_ref[...], w2_ref[...], preferred_element_type=F32) + b2_ref[...]
        o_ref[...] = y.astype(o_ref.dtype)


def _experts(xs, blk_e, blk_valid, w1, b1, w2, b2, e0):
    n_rows, d = xs.shape
    tm = MOE_TM
    ff = EXPERT_FF
    grid_spec = pltpu.PrefetchScalarGridSpec(
        num_scalar_prefetch=2,
        grid=(n_rows // tm,),
        in_specs=[
            pl.BlockSpec((tm, d), lambda i, be, bv: (i, 0)),
            pl.BlockSpec((None, d, 2 * ff), lambda i, be, bv: (e0 + be[i], 0, 0)),
            pl.BlockSpec((None, 1, 2 * ff), lambda i, be, bv: (be[i], 0, 0)),
            pl.BlockSpec((None, ff, d), lambda i, be, bv: (be[i], 0, 0)),
            pl.BlockSpec((None, 1, d), lambda i, be, bv: (be[i], 0, 0)),
        ],
        out_specs=pl.BlockSpec((tm, d), lambda i, be, bv: (i, 0)),
        scratch_shapes=[pltpu.VMEM((tm, ff), BF16)],
    )
    return pl.pallas_call(
        _expert_kernel,
        grid_spec=grid_spec,
        out_shape=jax.ShapeDtypeStruct((n_rows, d), BF16),
        compiler_params=_cparams(("arbitrary",)),
        name="moe_experts",
    )(blk_e, blk_valid, xs, w1, b1, w2, b2)


def _moe(h2, top_i, top_w, w1, b1, w2, b2, e0):
    n_tok, d = h2.shape
    tm = MOE_TM
    n_assign = n_tok * TOP_K
    flat_e = top_i.reshape(n_assign)
    order = jnp.argsort(flat_e).astype(jnp.int32)
    rank = jnp.argsort(order).astype(jnp.int32)
    onehot = flat_e[:, None] == jnp.arange(N_EXPERTS, dtype=flat_e.dtype)[None, :]
    counts = jnp.sum(onehot, axis=0, dtype=jnp.int32)
    padded = (counts + tm - 1) // tm * tm
    pad_end = jnp.cumsum(padded)
    start = jnp.cumsum(counts) - counts
    shift = (pad_end - padded) - start
    pos = rank + jnp.sum(jnp.where(onehot, shift[None, :], 0), axis=1)
    n_blocks = -(-n_assign // tm) + N_EXPERTS
    blk_start = jnp.arange(n_blocks, dtype=jnp.int32) * tm
    blk_valid = (blk_start < pad_end[-1]).astype(jnp.int32)
    blk_e = jnp.minimum(jnp.searchsorted(pad_end, blk_start, side='right'), N_EXPERTS - 1).astype(jnp.int32)
    blk_e = jnp.where(blk_valid > 0, blk_e, blk_e[jnp.maximum(pad_end[-1] // tm - 1, 0)])
    src = blk_start[:, None] + jnp.arange(tm, dtype=jnp.int32)[None, :] - shift[blk_e][:, None]
    lo = start[blk_e][:, None]
    live = (src >= lo) & (src < lo + counts[blk_e][:, None]) & (blk_valid[:, None] > 0)
    row_tok = jnp.where(live, order[jnp.clip(src, 0, n_assign - 1)] // TOP_K, 0).reshape(n_blocks * tm)
    xs = h2[row_tok]
    y = _experts(xs, blk_e, blk_valid, w1, b1, w2, b2, e0)
    return jnp.sum(y[pos].astype(F32).reshape(n_tok, TOP_K, d) * top_w[:, :, None], axis=1)


def _final_kernel(x_ref, g_ref, o_ref):
    x = x_ref[...]
    o_ref[...] = x * lax.rsqrt(jnp.mean(x * x, axis=-1, keepdims=True) + RMS_EPS) * g_ref[...]


def _final_norm(xu, g, seq):
    bsz, t, d = xu.shape
    tm = _pick(seq, (512, 256, 128))
    return pl.pallas_call(
        _final_kernel,
        grid=(bsz, seq // tm),
        in_specs=[pl.BlockSpec((None, tm, d), lambda b, i: (b, i, 0)),
                  pl.BlockSpec((1, d), lambda b, i: (0, 0))],
        out_specs=pl.BlockSpec((None, tm, d), lambda b, i: (b, i, 0)),
        out_shape=jax.ShapeDtypeStruct((bsz, seq, d), F32),
        compiler_params=_cparams(("parallel", "parallel")),
        name="final_norm",
    )(xu, g.reshape(1, d))


def _layer(xu, mod_l, mod_c, seq, layer_idx, ctx_out, cos_t, sin_t, norm1, w_in, da_lambda, da_subln,
           gm_ln_g, gm_ln_b, gm_ws, gm_bs, dn_conv, dn_a_log, dn_dt_bias, dn_norm, b_gate, w_branch,
           w_out, norm2, w_router, b_router, w1_all, b_e1, w_e2, b_e2):
    bsz, t, d = xu.shape
    rows = t if ctx_out else seq
    lam_init = 0.8 - 0.6 * math.exp(-0.3 * layer_idx)
    ml = [mod_l[:, k:k + 1, :] for k in range(6)]
    mc = [mod_c[k:k + 1, :] for k in range(6)]

    h = _adaln(xu, norm1.reshape(1, d), ml[0], ml[1], mc[0], mc[1], seq)
    w_main = jnp.concatenate([w_in[:, :OFF_SMALL], w_in[:, OFF_GATE:]], axis=1).astype(BF16)
    w_small = jnp.zeros((d, LANES), BF16).at[:, :OFF_GATE - OFF_SMALL].set(
        w_in[:, OFF_SMALL:OFF_GATE].astype(BF16))
    p = _in_proj_main(h, w_main, cos_t, sin_t)
    small = _in_proj_small(h, w_small)

    ya = _diff_attention(p, da_lambda, da_subln, seq, rows, lam_init)
    yg = _spatial_gating(p, gm_ln_g, gm_ln_b, gm_ws, gm_bs, rows)
    yd = _gated_deltanet(p, small, dn_conv, dn_a_log, dn_dt_bias, dn_norm, seq)
    z = _merge(ya, yg, yd, p, w_branch.astype(BF16), b_gate, rows)
    xu = _out_proj_residual(z, w_out.astype(BF16), xu, ml[2], mc[2], seq, rows)

    h2, top_i, top_w = _adaln_router(xu, norm2.reshape(1, d), ml[3], ml[4], mc[3], mc[4],
                                     w_router, b_router, seq, rows)
    n_tok = bsz * rows
    y = _moe(h2.reshape(n_tok, d), top_i.reshape(n_tok, LANES)[:, :TOP_K],
             top_w.reshape(n_tok, LANES)[:, :TOP_K], w1_all, _regroup_bias(b_e1),
             w_e2.astype(BF16), b_e2[:, None, :], layer_idx * N_EXPERTS).reshape(bsz, rows, d)
    gate2 = ml[5] if not ctx_out else jnp.concatenate(
        [jnp.broadcast_to(ml[5], (bsz, seq, d)), jnp.broadcast_to(mc[5][None], (bsz, t - seq, d))], axis=1)
    return xu + gate2 * y


def kernel(x, c, ctx, c_ctx, w_mod, b_mod, norm1, w_in, da_lambda, da_subln, gm_ln_g, gm_ln_b, gm_ws, gm_bs,
           dn_conv, dn_a_log, dn_dt_bias, dn_norm, b_gate, w_branch, w_out, norm2, w_router, b_router,
           w_e1, b_e1, w_e2, b_e2, norm_f):
    bsz, seq, d = x.shape
    n_ctx = ctx.shape[1]
    t = seq + n_ctx
    depth = w_mod.shape[0]
    xu = jnp.concatenate([x, ctx], axis=1)
    r = -(-(bsz + 1) // 8) * 8
    cond = jnp.zeros((r, d), F32).at[:bsz].set(c).at[bsz].set(c_ctx)
    mod = _modulation(cond, w_mod, b_mod).reshape(depth, r, 6, d)
    cos_t, sin_t = _rope_tables(seq, t)
    w1_all = _w1_prep(w_e1)
    for l in range(depth):
        xu = _layer(xu, mod[l, :bsz], mod[l, bsz], seq, l, l < depth - 1, cos_t, sin_t, norm1[l], w_in[l],
                    da_lambda[l], da_subln[l], gm_ln_g[l], gm_ln_b[l], gm_ws[l], gm_bs[l], dn_conv[l],
                    dn_a_log[l], dn_dt_bias[l], dn_norm[l], b_gate[l], w_branch[l], w_out[l], norm2[l],
                    w_router[l], b_router[l], w1_all, b_e1[l], w_e2[l], b_e2[l])
    return _final_norm(xu, norm_f, seq)
```

```python
import functools
import math

import jax
import jax.numpy as jnp
from jax import lax
from jax.experimental import pallas as pl
from jax.experimental.pallas import tpu as pltpu

F32 = jnp.float32
BF16 = jnp.bfloat16

D_MODEL = 2048
GRID_W = 64
RMS_EPS = 1e-6
BRANCH_W = D_MODEL // 2
N_BRANCH = 3
DA_HD = 64
DA_HEADS = BRANCH_W // (2 * DA_HD)
ROPE_THETA = 10000.0
ROPE_PAIRS_AXIS = DA_HD // 4
GM_CHUNK = 128
GM_GW = 128
GM_GROUPS = BRANCH_W // GM_GW
DN_HD = 128
DN_HEADS = BRANCH_W // DN_HD
DN_CHUNK = 64
DN_CONV = 5
N_EXPERTS = 32
TOP_K = 4
EXPERT_FF = D_MODEL // 2
SWIGLU_LIMIT = 7.0
SWIGLU_ALPHA = 1.702

LANES = 128
VMEM_LIMIT = 56 * 1024 * 1024

OFF_DA_Q = 0
OFF_DA_K = 1024
OFF_DA_V = 2048
OFF_GM_U = 3072
OFF_GM_V = 4096
OFF_DN_Q = 5120
OFF_DN_K = 6144
OFF_DN_V = 7168
OFF_DN_Z = 8192
OFF_SMALL = 9216
OFF_GATE = 9248
N_MAIN = 9216 + N_BRANCH * D_MODEL
OFF_GATE_MAIN = 9216

MOE_TM = 512


def _cparams(sem):
    return pltpu.CompilerParams(dimension_semantics=sem, vmem_limit_bytes=VMEM_LIMIT)


def _pick(n, cands):
    for c in cands:
        if n % c == 0:
            return c
    raise ValueError(f"no tile for {n} in {cands}")


def _sigmoid(x):
    return jax.nn.sigmoid(x)


def _silu(x):
    return x * _sigmoid(x)


def _gelu_tanh(x):
    return x * (0.5 * (1.0 + jnp.tanh(0.7978845608028654 * (x + 0.044715 * (x * x * x)))))


def _bdot(a, b):
    return jnp.dot(a.astype(BF16), b.astype(BF16), preferred_element_type=F32)


def _split(a):
    hi = a.astype(BF16)
    lo = (a - hi.astype(F32)).astype(BF16)
    return hi, lo


def _dot3(a, b):
    ah, al = _split(a)
    bh, bl = _split(b)
    return (jnp.dot(ah, bh, preferred_element_type=F32)
            + (jnp.dot(al, bh, preferred_element_type=F32)
               + jnp.dot(ah, bl, preferred_element_type=F32)))


def _mod_kernel(c_ref, w_ref, b_ref, o_ref):
    s = _silu(c_ref[...])
    o_ref[...] = _bdot(s, w_ref[...]) + b_ref[...]


def _modulation(cond, w_mod, b_mod):
    nl, d, n6 = w_mod.shape
    r = cond.shape[0]
    tn = 1024
    return pl.pallas_call(
        _mod_kernel,
        grid=(nl, n6 // tn),
        in_specs=[
            pl.BlockSpec((r, d), lambda l, j: (0, 0)),
            pl.BlockSpec((None, d, tn), lambda l, j: (l, 0, j)),
            pl.BlockSpec((None, 1, tn), lambda l, j: (l, 0, j)),
        ],
        out_specs=pl.BlockSpec((None, r, tn), lambda l, j: (l, 0, j)),
        out_shape=jax.ShapeDtypeStruct((nl, r, n6), F32),
        compiler_params=_cparams(("arbitrary", "arbitrary")),
        name="modulation",
    )(cond, w_mod, b_mod.reshape(nl, 1, n6))


def _adaln_tile(x, g, shl, scl, shc, scc, row0, seq):
    tm = x.shape[0]
    y = x * lax.rsqrt(jnp.mean(x * x, axis=-1, keepdims=True) + RMS_EPS) * g
    row = row0 + lax.broadcasted_iota(jnp.int32, (tm, 1), 0)
    is_ctx = row >= seq
    scale = jnp.where(is_ctx, scc, scl)
    shift = jnp.where(is_ctx, shc, shl)
    return y * (1.0 + scale) + shift


def _adaln_kernel(x_ref, g_ref, shl_ref, scl_ref, shc_ref, scc_ref, o_ref, *, tm, seq):
    h = _adaln_tile(x_ref[...], g_ref[...], shl_ref[...], scl_ref[...], shc_ref[...], scc_ref[...],
                    pl.program_id(1) * tm, seq)
    o_ref[...] = h.astype(o_ref.dtype)


def _mod_specs(d):
    return [
        pl.BlockSpec((1, d), lambda b, i: (0, 0)),
        pl.BlockSpec((None, 1, d), lambda b, i: (b, 0, 0)),
        pl.BlockSpec((None, 1, d), lambda b, i: (b, 0, 0)),
        pl.BlockSpec((1, d), lambda b, i: (0, 0)),
        pl.BlockSpec((1, d), lambda b, i: (0, 0)),
    ]


def _adaln(xu, g, shl, scl, shc, scc, seq):
    bsz, t, d = xu.shape
    tm = _pick(t, (768, 384, 256, 128))
    return pl.pallas_call(
        functools.partial(_adaln_kernel, tm=tm, seq=seq),
        grid=(bsz, t // tm),
        in_specs=[pl.BlockSpec((None, tm, d), lambda b, i: (b, i, 0))] + _mod_specs(d),
        out_specs=pl.BlockSpec((None, tm, d), lambda b, i: (b, i, 0)),
        out_shape=jax.ShapeDtypeStruct((bsz, t, d), BF16),
        compiler_params=_cparams(("parallel", "parallel")),
        name="adaln",
    )(xu, g, shl, scl, shc, scc)


def _adaln_router_kernel(x_ref, g_ref, shl_ref, scl_ref, shc_ref, scc_ref, wr_ref, br_ref,
                         h_ref, idx_ref, wt_ref, *, tm, seq):
    h = _adaln_tile(x_ref[...], g_ref[...], shl_ref[...], scl_ref[...], shc_ref[...], scc_ref[...],
                    pl.program_id(1) * tm, seq)
    h_ref[...] = h.astype(h_ref.dtype)
    logits = _dot3(h, wr_ref[...]) + br_ref[...]
    lane = lax.broadcasted_iota(jnp.int32, logits.shape, 1).astype(F32)
    vals, idxs = [], []
    cur = logits
    for _ in range(TOP_K):
        m = jnp.max(cur, axis=-1, keepdims=True)
        am = jnp.min(jnp.where(cur == m, lane, float(LANES)), axis=-1, keepdims=True)
        vals.append(m)
        idxs.append(am)
        cur = jnp.where(lane == am, -jnp.inf, cur)
    es = [jnp.exp(v - vals[0]) for v in vals]
    tot = es[0] + es[1] + es[2] + es[3]
    wt = jnp.zeros(logits.shape, F32)
    ix = jnp.zeros(logits.shape, F32)
    for k in range(TOP_K):
        wt = jnp.where(lane == k, es[k] / tot, wt)
        ix = jnp.where(lane == k, idxs[k], ix)
    idx_ref[...] = ix.astype(jnp.int32)
    wt_ref[...] = wt


def _adaln_router(xu, g, shl, scl, shc, scc, w_router, b_router, seq, rows):
    bsz, t, d = xu.shape
    tm = _pick(rows, (768, 512, 384, 256, 128))
    wr = jnp.zeros((d, LANES), F32).at[:, :N_EXPERTS].set(w_router)
    br = jnp.full((1, LANES), -1e30, F32).at[0, :N_EXPERTS].set(b_router)
    return pl.pallas_call(
        functools.partial(_adaln_router_kernel, tm=tm, seq=seq),
        grid=(bsz, rows // tm),
        in_specs=[pl.BlockSpec((None, tm, d), lambda b, i: (b, i, 0))] + _mod_specs(d) + [
            pl.BlockSpec((d, LANES), lambda b, i: (0, 0)),
            pl.BlockSpec((1, LANES), lambda b, i: (0, 0)),
        ],
        out_specs=[
            pl.BlockSpec((None, tm, d), lambda b, i: (b, i, 0)),
            pl.BlockSpec((None, tm, LANES), lambda b, i: (b, i, 0)),
            pl.BlockSpec((None, tm, LANES), lambda b, i: (b, i, 0)),
        ],
        out_shape=[
            jax.ShapeDtypeStruct((bsz, rows, d), BF16),
            jax.ShapeDtypeStruct((bsz, rows, LANES), jnp.int32),
            jax.ShapeDtypeStruct((bsz, rows, LANES), F32),
        ],
        compiler_params=_cparams(("parallel", "parallel")),
        name="adaln_router",
    )(xu, g, shl, scl, shc, scc, wr, br)


def _inproj_kernel(h_ref, w_ref, cos_ref, sin_ref, o_ref, *, tn, n_rope_tiles, n_q_tiles):
    j = pl.program_id(1)
    acc = jnp.dot(h_ref[...], w_ref[...], preferred_element_type=F32)

    @pl.when(j >= n_rope_tiles)
    def _():
        o_ref[...] = acc.astype(o_ref.dtype)

    @pl.when(j < n_rope_tiles)
    def _():
        scale = jnp.where(j < n_q_tiles, DA_HD ** -0.5, 1.0).astype(F32)
        cos = cos_ref[...] * scale
        sin = sin_ref[...] * scale
        lane = lax.broadcasted_iota(jnp.int32, cos.shape, 1)
        first = (lane % DA_HD) < (DA_HD // 2)
        for c in range(tn // LANES):
            a = acc[:, c * LANES:(c + 1) * LANES]
            sw = jnp.where(first, pltpu.roll(a, LANES - DA_HD // 2, 1), pltpu.roll(a, DA_HD // 2, 1))
            o_ref[:, c * LANES:(c + 1) * LANES] = (a * cos + sw * sin).astype(o_ref.dtype)


def _in_proj_main(h, w_main, cos_t, sin_t):
    bsz, t, d = h.shape
    n = w_main.shape[1]
    tn = 512
    return pl.pallas_call(
        functools.partial(_inproj_kernel, tn=tn, n_rope_tiles=OFF_DA_V // tn, n_q_tiles=OFF_DA_K // tn),
        grid=(bsz, n // tn),
        in_specs=[
            pl.BlockSpec((None, t, d), lambda b, j: (b, 0, 0)),
            pl.BlockSpec((d, tn), lambda b, j: (0, j)),
            pl.BlockSpec((t, LANES), lambda b, j: (0, 0)),
            pl.BlockSpec((t, LANES), lambda b, j: (0, 0)),
        ],
        out_specs=pl.BlockSpec((None, t, tn), lambda b, j: (b, 0, j)),
        out_shape=jax.ShapeDtypeStruct((bsz, t, n), BF16),
        compiler_params=_cparams(("parallel", "arbitrary")),
        name="in_proj",
    )(h, w_main, cos_t, sin_t)


def _mm_kernel(x_ref, w_ref, o_ref):
    o_ref[...] = jnp.dot(x_ref[...], w_ref[...], preferred_element_type=F32).astype(o_ref.dtype)


def _in_proj_small(h, w_small):
    bsz, t, d = h.shape
    return pl.pallas_call(
        _mm_kernel,
        grid=(bsz,),
        in_specs=[
            pl.BlockSpec((None, t, d), lambda b: (b, 0, 0)),
            pl.BlockSpec((d, LANES), lambda b: (0, 0)),
        ],
        out_specs=pl.BlockSpec((None, t, LANES), lambda b: (b, 0, 0)),
        out_shape=jax.ShapeDtypeStruct((bsz, t, LANES), F32),
        compiler_params=_cparams(("parallel",)),
        name="in_proj_small",
    )(h, w_small)


def _rope_tables(seq, t):
    rows = seq // GRID_W
    row = jnp.repeat(jnp.arange(rows, dtype=F32), GRID_W)
    col = jnp.tile(jnp.arange(GRID_W, dtype=F32), rows)
    inv = ROPE_THETA ** (-jnp.arange(ROPE_PAIRS_AXIS, dtype=F32) / ROPE_PAIRS_AXIS)
    ang = jnp.concatenate([row[:, None] * inv, col[:, None] * inv], axis=-1)
    cos, sin = jnp.cos(ang), jnp.sin(ang)
    cos_t = jnp.tile(cos, (1, LANES // (DA_HD // 2)))
    sin_t = jnp.tile(jnp.concatenate([-sin, sin], axis=-1), (1, LANES // DA_HD))
    pad = t - seq
    cos_t = jnp.concatenate([cos_t, jnp.ones((pad, LANES), F32)], axis=0)
    sin_t = jnp.concatenate([sin_t, jnp.zeros((pad, LANES), F32)], axis=0)
    return cos_t, sin_t


ATTN_ROW_GROUPS = 4


def _attn_kernel(lam_ref, g_ref, q_ref, k_ref, v_ref, o_ref, *, seq, tq, lam_init):
    qi = pl.program_id(2)
    lp = lam_ref[...]
    l1 = jnp.sum(lp[0:1] * lp[1:2], axis=-1, keepdims=True)
    l2 = jnp.sum(lp[2:3] * lp[3:4], axis=-1, keepdims=True)
    lam = jnp.exp(l1) - jnp.exp(l2) + lam_init
    q = q_ref[...].astype(F32)
    lane = lax.broadcasted_iota(jnp.int32, q.shape, 1)
    qq = jnp.concatenate([jnp.where(lane < DA_HD, q, 0.0), jnp.where(lane >= DA_HD, q, 0.0)],
                         axis=0).astype(BF16)

    def core(k, v):
        rs = 2 * tq // ATTN_ROW_GROUPS
        scores = [lax.dot_general(qq[i * rs:(i + 1) * rs], k, (((1,), (1,)), ((), ())),
                                  preferred_element_type=F32) for i in range(ATTN_ROW_GROUPS)]
        outs = []
        for s in scores:
            m = jnp.max(s, axis=-1, keepdims=True)
            p = jnp.exp(s - m)
            den = jnp.sum(p, axis=-1, keepdims=True)
            outs.append(jnp.dot(p.astype(BF16), v, preferred_element_type=F32) / den)
        o = jnp.concatenate(outs, axis=0)
        o = o[:tq] - lam * o[tq:]
        y = o * lax.rsqrt(jnp.mean(o * o, axis=-1, keepdims=True) + RMS_EPS) * g_ref[...]
        o_ref[...] = (y * (1.0 - lam_init)).astype(o_ref.dtype)

    @pl.when(qi * tq < seq)
    def _():
        core(k_ref[...], v_ref[...])

    @pl.when(qi * tq >= seq)
    def _():
        core(k_ref[seq:, :], v_ref[seq:, :])


def _diff_attention(p, lam_params, subln_g, seq, rows, lam_init):
    bsz, t, _ = p.shape
    tq = _pick(math.gcd(seq, t - seq), (256, 128))
    cq, ck, cv = OFF_DA_Q // LANES, OFF_DA_K // LANES, OFF_DA_V // LANES
    return pl.pallas_call(
        functools.partial(_attn_kernel, seq=seq, tq=tq, lam_init=lam_init),
        grid=(bsz, DA_HEADS, rows // tq),
        in_specs=[
            pl.BlockSpec((4, DA_HD), lambda b, h, i: (0, 0)),
            pl.BlockSpec((1, 2 * DA_HD), lambda b, h, i: (0, 0)),
            pl.BlockSpec((None, tq, LANES), lambda b, h, i: (b, i, cq + h)),
            pl.BlockSpec((None, t, LANES), lambda b, h, i: (b, 0, ck + h)),
            pl.BlockSpec((None, t, LANES), lambda b, h, i: (b, 0, cv + h)),
        ],
        out_specs=pl.BlockSpec((None, tq, LANES), lambda b, h, i: (b, i, h)),
        out_shape=jax.ShapeDtypeStruct((bsz, rows, BRANCH_W), BF16),
        compiler_params=_cparams(("parallel", "parallel", "arbitrary")),
        name="diff_attention",
    )(lam_params, subln_g.reshape(1, -1), p, p, p)


def _gmlp_kernel(u_ref, v_ref, lng_ref, lnb_ref, ws_ref, bs_ref, o_ref, *, nchunks):
    for c in range(nchunks):
        r0 = c * GM_CHUNK
        u = _gelu_tanh(u_ref[r0:r0 + GM_CHUNK, :].astype(F32))
        v = _gelu_tanh(v_ref[r0:r0 + GM_CHUNK, :].astype(F32))
        xc = v - jnp.mean(v, axis=-1, keepdims=True)
        var = jnp.mean(xc * xc, axis=-1, keepdims=True)
        vn = (xc * lax.rsqrt(var + RMS_EPS) * lng_ref[...] + lnb_ref[...]).astype(BF16)
        for g in range(GM_GROUPS):
            cs = slice(g * GM_GW, (g + 1) * GM_GW)
            s = jnp.dot(ws_ref[g], vn[:, cs], preferred_element_type=F32) + bs_ref[g]
            o_ref[r0:r0 + GM_CHUNK, cs] = (u[:, cs] * s).astype(o_ref.dtype)


def _spatial_gating(p, ln_g, ln_b, ws, bs, rows):
    bsz, t, _ = p.shape
    tm = _pick(rows, (768, 512, 384, 256, 128))
    cu, cv = OFF_GM_U // BRANCH_W, OFF_GM_V // BRANCH_W
    bs_b = jnp.broadcast_to(bs[:, :, None], (GM_GROUPS, GM_CHUNK, GM_GW)).astype(F32)
    return pl.pallas_call(
        functools.partial(_gmlp_kernel, nchunks=tm // GM_CHUNK),
        grid=(bsz, rows // tm),
        in_specs=[
            pl.BlockSpec((None, tm, BRANCH_W), lambda b, i: (b, i, cu)),
            pl.BlockSpec((None, tm, BRANCH_W), lambda b, i: (b, i, cv)),
            pl.BlockSpec((1, BRANCH_W), lambda b, i: (0, 0)),
            pl.BlockSpec((1, BRANCH_W), lambda b, i: (0, 0)),
            pl.BlockSpec((GM_GROUPS, GM_CHUNK, GM_CHUNK), lambda b, i: (0, 0, 0)),
            pl.BlockSpec((GM_GROUPS, GM_CHUNK, GM_GW), lambda b, i: (0, 0, 0)),
        ],
        out_specs=pl.BlockSpec((None, tm, BRANCH_W), lambda b, i: (b, i, 0)),
        out_shape=jax.ShapeDtypeStruct((bsz, rows, BRANCH_W), BF16),
        compiler_params=_cparams(("parallel", "parallel")),
        name="spatial_gating",
    )(p, p, ln_g.reshape(1, -1), ln_b.reshape(1, -1), ws.astype(BF16), bs_b)


DN_BASE = 8
DN_PREP_GROUPS = (4, 6, 3, 2, 1)
DN_HEAD_GROUP = 2


def _dn_kernel(alog_ref, dtb_ref, q_ref, k_ref, v_ref, z_ref, sm_ref, ar_ref, cw_ref, ng_ref, o_ref,
               qn_ref, kn_ref, vn_ref, rowg_ref, u_ref, wq_ref, qk2_ref, cd_ref, oacc_ref, st_ref,
               *, seq, t):
    hg = DN_HEAD_GROUP
    hblk = pl.program_id(1)
    nc = t // DN_CHUNK
    n_lat = seq // DN_CHUNK
    n_ctx = nc - n_lat
    hw = DN_HD
    cw = DN_CHUNK

    row = lax.broadcasted_iota(jnp.int32, (t, 1), 0)
    seg_lo = jnp.where(row < seq, 0, seq)
    seg_hi = jnp.where(row < seq, seq, t)

    def conv_silu(x_ref, w):
        x = x_ref[...].astype(F32)
        acc = x * w[DN_CONV // 2:DN_CONV // 2 + 1, :]
        for s in (-2, -1, 1, 2):
            xs = pltpu.roll(x, (-s) % t, 0)
            rs = row + s
            ok = (rs >= seg_lo) & (rs < seg_hi)
            acc = acc + jnp.where(ok, xs, 0.0) * w[DN_CONV // 2 + s:DN_CONV // 2 + s + 1, :]
        return _silu(acc)

    def l2n(x):
        return x * lax.rsqrt(jnp.sum(x * x, axis=-1, keepdims=True) + RMS_EPS)

    qc = conv_silu(q_ref, cw_ref[0])
    kc = conv_silu(k_ref, cw_ref[1])
    vn_ref[...] = conv_silu(v_ref, cw_ref[2])
    for j in range(hg):
        cs = slice(j * hw, (j + 1) * hw)
        qn_ref[:, cs] = l2n(qc[:, cs]) * (DN_HD ** -0.5)
        kn_ref[:, cs] = l2n(kc[:, cs])

    def softplus(x):
        return jnp.maximum(x, 0.0) + jnp.log1p(jnp.exp(-jnp.abs(x)))

    rw = 2 * hg * cw
    lane_r = lax.broadcasted_iota(jnp.int32, (1, rw), 1)
    chain_r = lane_r // cw
    pos_r = lane_r % cw
    alog_r = jnp.zeros((1, rw), F32)
    dt_r = jnp.zeros((1, rw), F32)
    for d in range(2):
        for j in range(hg):
            alog_r = jnp.where(chain_r == hg * d + j, alog_ref[d, hblk * hg + j], alog_r)
            dt_r = jnp.where(chain_r == hg * d + j, dtb_ref[d, hblk * hg + j], dt_r)
    g_all = -jnp.exp(alog_r) * softplus(ar_ref[...].reshape(nc * 8, rw) + dt_r)
    pre = g_all
    suf = g_all
    sh = 1
    while sh < cw:
        pre = pre + jnp.where(pos_r >= sh, pltpu.roll(pre, sh, 1), 0.0)
        suf = suf + jnp.where(pos_r < cw - sh, pltpu.roll(suf, rw - sh, 1), 0.0)
        sh *= 2
    run = jnp.where(lane_r >= hg * cw, suf, pre).reshape(nc, 8, rw)
    tot = (pre + suf - g_all).reshape(nc, 8, rw)
    sub = lax.broadcasted_iota(jnp.int32, (nc, 8, rw), 1)
    both = jnp.where(sub == 0, run, tot)
    for d in range(2):
        rowg_ref[d] = both[:, :, d * hg * cw:(d + 1) * hg * cw]

    st_ref[...] = jnp.zeros(st_ref.shape, F32)

    pshape = (cw, hg * cw)
    ii = lax.broadcasted_iota(jnp.int32, pshape, 0)
    lp = lax.broadcasted_iota(jnp.int32, pshape, 1)
    jl = lp % cw
    left = lp < cw
    diag = ii == jl
    eye_p = jnp.where(diag, 1.0, 0.0).astype(F32)
    blk_base = (ii // DN_BASE) == (jl // DN_BASE)
    incl = [ii >= jl, ii <= jl]
    strict = [ii > jl, ii < jl]
    half = [jnp.where(left, 1.0, 0.0).astype(BF16), jnp.where(left, 0.0, 1.0).astype(BF16)]
    left_sq = lax.broadcasted_iota(jnp.int32, (LANES, LANES), 1) < cw

    def blockdiag(b16):
        return jnp.concatenate([b16 * half[0], b16 * half[1]], axis=0)

    def pprod(a, b):
        return jnp.dot(a.astype(BF16), blockdiag(b.astype(BF16)), preferred_element_type=F32)

    def tri_inverse(lmats):
        ms = [jnp.where(blk_base, -l, 0.0) for l in lmats]
        xs = [eye_p + m for m in ms]
        pws = [pprod(m, m) for m in ms]
        span = 4
        while span <= DN_BASE:
            tts = [pprod(jnp.concatenate([x, pw], axis=0), pw) for x, pw in zip(xs, pws)]
            xs = [x + tt[:cw] for x, tt in zip(xs, tts)]
            pws = [tt[cw:] for tt in tts]
            span *= 2
        bs = DN_BASE
        while bs < cw:
            off = ((ii // (2 * bs)) == (jl // (2 * bs))) & ((ii // bs) != (jl // bs))
            cmats = [jnp.where(off, l, 0.0) for l in lmats]
            ys = [pprod(x, c) for x, c in zip(xs, cmats)]
            zs = [pprod(y, x) for y, x in zip(ys, xs)]
            xs = [x - z for x, z in zip(xs, zs)]
            bs *= 2
        return xs

    def prep_load(c):
        rows = pl.ds(pl.multiple_of(c * cw, cw), cw)
        kk = [kn_ref[rows, j * hw:(j + 1) * hw] for j in range(hg)]
        qq = [qn_ref[rows, j * hw:(j + 1) * hw] for j in range(hg)]
        vv = [vn_ref[rows, j * hw:(j + 1) * hw] for j in range(hg)]
        return kk, qq, vv, sm_ref[rows, :], [rowg_ref[d, c] for d in range(2)]

    def prep_compute(loaded):
        n = len(loaded)
        gram, qk, ktp = [], [], []
        for kk, qq, vv, sm, rgs in loaded:
            gq, kt = [], []
            for j in range(hg):
                kb = kk[j].astype(BF16)
                gq.append(lax.dot_general(jnp.concatenate([kb, qq[j].astype(BF16)], axis=0),
                                          jnp.concatenate([kb, kb], axis=0), (((1,), (1,)), ((), ())),
                                          preferred_element_type=F32))
                kt.append(jnp.concatenate([kk[j], kk[j]], axis=0).T)
            pair = jnp.where(left_sq, gq[0], gq[1])
            gram.append(pair[:cw])
            qk.append(pair[cw:])
            ktp.append(jnp.where(left_sq, kt[0], kt[1]))
        pre = []
        for ci, (kk, qq, vv, sm, rgs) in enumerate(loaded):
            lane_c = lax.broadcasted_iota(jnp.int32, sm.shape, 1)
            for d in range(2):
                bcol = [_sigmoid(jnp.sum(jnp.where(lane_c == d * DN_HEADS + hblk * hg + j, sm, 0.0),
                                         axis=1, keepdims=True)) for j in range(hg)]
                gc_row, g_tot = rgs[d][0:1, :], rgs[d][1:2, :]
                gdiag = jnp.where(diag, gc_row, 0.0)
                gcol = [jnp.sum(jnp.where(left, gdiag, 0.0), axis=1, keepdims=True),
                        jnp.sum(jnp.where(left, 0.0, gdiag), axis=1, keepdims=True)]
                gc = jnp.where(left, gcol[0], gcol[1])
                beta = jnp.where(left, bcol[0], bcol[1])
                dec = jnp.exp(jnp.where(incl[d], gc - gc_row, -jnp.inf))
                lmat = jnp.where(strict[d], beta * gram[ci] * dec, 0.0)
                pre.append((ci, d, bcol, gcol, gc_row, g_tot, dec, lmat))
        tinvs = tri_inverse([p[-1] for p in pre])
        rhss, egs = [], []
        for ci, d, bcol, gcol, gc_row, g_tot, dec, lmat in pre:
            kk, qq, vv = loaded[ci][:3]
            eg = [jnp.exp(gcol[j]) for j in range(hg)]
            egs.append(eg)
            rhss.append(jnp.concatenate(
                [jnp.concatenate([vv[j] * bcol[j], kk[j] * (bcol[j] * eg[j])], axis=1) for j in range(hg)],
                axis=0).astype(BF16))
        sols = []
        for tinv, rhs in zip(tinvs, rhss):
            t16 = tinv.astype(BF16)
            sols.append(jnp.dot(jnp.concatenate([t16 * half[0], t16 * half[1]], axis=0), rhs,
                                preferred_element_type=F32))
        outs = [[None, None] for _ in range(n)]
        for (ci, d, bcol, gcol, gc_row, g_tot, dec, lmat), sol, eg in zip(pre, sols, egs):
            qq = loaded[ci][1]
            e_tot = jnp.exp(g_tot)
            q_intra = jnp.where(incl[d], qk[ci] * dec, 0.0)
            outs[ci][d] = dict(
                u=[sol[j * cw:(j + 1) * cw, :hw] for j in range(hg)],
                wq=[jnp.concatenate([sol[j * cw:(j + 1) * cw, hw:], qq[j] * eg[j]], axis=0).astype(BF16)
                    for j in range(hg)],
                cd=[jnp.broadcast_to(e_tot[:, j * cw:j * cw + 1], (1, hw)) for j in range(hg)],
                qk2=jnp.concatenate([q_intra, ktp[ci] * jnp.exp(g_tot - gc_row)], axis=0).astype(BF16))
        return outs

    def prep_store(c, outs):
        rows = pl.ds(pl.multiple_of(c * cw, cw), cw)
        for d in range(2):
            for j in range(hg):
                u_ref[hg * d + j, rows, :] = outs[d]["u"][j]
                wq_ref[hg * d + j, c] = outs[d]["wq"][j]
                cd_ref[hg * d + j, c] = outs[d]["cd"][j]
            qk2_ref[d, c] = outs[d]["qk2"]

    group = _pick(nc, DN_PREP_GROUPS)

    def prep_body(g, carry):
        cs = [g * group + cc for cc in range(group)]
        loaded = [prep_load(c) for c in cs]
        outs = prep_compute(loaded)
        for c, o in zip(cs, outs):
            prep_store(c, o)
        return carry

    lax.fori_loop(0, nc // group, prep_body, 0)

    def scan_body(i, carry):
        cf = jnp.where(i < n_ctx, i + n_lat, i - n_ctx)
        cb = nc - 1 - i
        dirs = ((0, cf), (1, cb))
        rows = [pl.ds(pl.multiple_of(c * cw, cw), cw) for _, c in dirs]
        state = [st_ref[s] for s in range(2 * hg)]
        wq = [wq_ref[hg * d + j, c] for d, c in dirs for j in range(hg)]
        u = [u_ref[hg * d + j, rows[d], :] for d, _ in dirs for j in range(hg)]
        cd = [cd_ref[hg * d + j, c] for d, c in dirs for j in range(hg)]
        qk2 = [qk2_ref[d, c] for d, c in dirs]
        ws = [jnp.dot(wq[s], state[s].astype(BF16), preferred_element_type=F32)
              for s in range(2 * hg)]
        vb = [(u[s] - ws[s][:cw]).astype(BF16) for s in range(2 * hg)]
        zero = jnp.zeros_like(vb[0])
        ov = [jnp.dot(qk2[d], jnp.concatenate([jnp.concatenate([vb[hg * d], zero], axis=1),
                                               jnp.concatenate([zero, vb[hg * d + 1]], axis=1)], axis=0),
                      preferred_element_type=F32) for d in range(2)]
        o_new = [jnp.concatenate([ws[hg * d + j][cw:] + ov[d][:cw, j * hw:(j + 1) * hw] for j in range(hg)], axis=1)
                 for d in range(2)]
        st_new = [state[hg * d + j] * cd[hg * d + j] + ov[d][cw:, j * hw:(j + 1) * hw]
                  for d in range(2) for j in range(hg)]
        for d, _ in dirs:
            oacc_ref[d, rows[d], :] = o_new[d]
        for s in range(2 * hg):
            st_ref[s] = st_new[s]
        return carry

    lax.fori_loop(0, nc, scan_body, 0)

    o = oacc_ref[0] + oacc_ref[1]
    for j in range(hg):
        cs = slice(j * hw, (j + 1) * hw)
        oj = o[:, cs]
        y = oj * lax.rsqrt(jnp.mean(oj * oj, axis=-1, keepdims=True) + RMS_EPS) * ng_ref[...]
        o_ref[:, cs] = (y * _silu(z_ref[:, cs].astype(F32))).astype(o_ref.dtype)


def _gated_deltanet(p, small, conv_w, a_log, dt_bias, norm_g, seq):
    bsz, t, _ = p.shape
    nc = t // DN_CHUNK
    hg = DN_HEAD_GROUP
    assert hg == 2
    w = hg * DN_HD
    rw = 2 * hg * DN_CHUNK
    a = small[..., 2 * DN_HEADS:4 * DN_HEADS].reshape(bsz, nc, DN_CHUNK, 2, DN_HEADS // hg, hg)
    a_row = jnp.transpose(a, (0, 4, 1, 3, 5, 2)).reshape(bsz, DN_HEADS // hg, nc, 1, rw)
    a_row = jnp.broadcast_to(a_row, (bsz, DN_HEADS // hg, nc, 8, rw))
    cq, ck, cv, cz = (OFF_DN_Q // w, OFF_DN_K // w, OFF_DN_V // w, OFF_DN_Z // w)
    slab = lambda c0: pl.BlockSpec((None, t, w), lambda b, h: (b, 0, c0 + h))
    smem = pl.BlockSpec(memory_space=pltpu.SMEM)
    return pl.pallas_call(
        functools.partial(_dn_kernel, seq=seq, t=t),
        grid=(bsz, DN_HEADS // hg),
        in_specs=[
            smem, smem,
            slab(cq), slab(ck), slab(cv), slab(cz),
            pl.BlockSpec((None, t, LANES), lambda b, h: (b, 0, 0)),
            pl.BlockSpec((None, None, nc, 8, rw), lambda b, h: (b, h, 0, 0, 0)),
            pl.BlockSpec((3, DN_CONV, w), lambda b, h: (0, 0, h)),
            pl.BlockSpec((1, DN_HD), lambda b, h: (0, 0)),
        ],
        out_specs=pl.BlockSpec((None, t, w), lambda b, h: (b, 0, h)),
        out_shape=jax.ShapeDtypeStruct((bsz, t, BRANCH_W), BF16),
        scratch_shapes=[
            pltpu.VMEM((t, w), F32), pltpu.VMEM((t, w), F32), pltpu.VMEM((t, w), F32),
            pltpu.VMEM((2, nc, 8, hg * DN_CHUNK), F32),
            pltpu.VMEM((2 * hg, t, DN_HD), F32),
            pltpu.VMEM((2 * hg, nc, 2 * DN_CHUNK, DN_HD), BF16),
            pltpu.VMEM((2, nc, DN_CHUNK + DN_HD, hg * DN_CHUNK), BF16),
            pltpu.VMEM((2 * hg, nc, 1, DN_HD), F32),
            pltpu.VMEM((2, t, w), F32),
            pltpu.VMEM((2 * hg, DN_HD, DN_HD), F32),
        ],
        compiler_params=_cparams(("parallel", "parallel")),
        name="gated_deltanet",
    )(a_log, dt_bias, p, p, p, p, small, a_row, conv_w, norm_g.reshape(1, -1))


def _merge_kernel(ya_ref, yg_ref, yd_ref, ga_ref, gg_ref, gd_ref, wb_ref, bg_ref, o_ref):
    acc = None
    for i, (y_ref, g_ref) in enumerate(((ya_ref, ga_ref), (yg_ref, gg_ref), (yd_ref, gd_ref))):
        gate = _sigmoid(g_ref[...].astype(F32) + bg_ref[i])
        term = gate * jnp.dot(y_ref[...], wb_ref[i], preferred_element_type=F32)
        acc = term if acc is None else acc + term
    o_ref[...] = acc.astype(o_ref.dtype)


def _merge(ya, yg, yd, p, w_branch, b_gate, rows):
    bsz = p.shape[0]
    d = D_MODEL
    tm = _pick(rows, (768, 512, 384, 256, 128))
    tn = 512
    g0 = OFF_GATE_MAIN // tn
    y_spec = pl.BlockSpec((None, tm, BRANCH_W), lambda b, i, j: (b, i, 0))
    gate_spec = lambda k: pl.BlockSpec((None, tm, tn), lambda b, i, j: (b, i, g0 + k * (d // tn) + j))
    return pl.pallas_call(
        _merge_kernel,
        grid=(bsz, rows // tm, d // tn),
        in_specs=[y_spec, y_spec, y_spec, gate_spec(0), gate_spec(1), gate_spec(2),
                  pl.BlockSpec((N_BRANCH, BRANCH_W, tn), lambda b, i, j: (0, 0, j)),
                  pl.BlockSpec((N_BRANCH, 1, tn), lambda b, i, j: (0, 0, j))],
        out_specs=pl.BlockSpec((None, tm, tn), lambda b, i, j: (b, i, j)),
        out_shape=jax.ShapeDtypeStruct((bsz, rows, d), BF16),
        compiler_params=_cparams(("parallel", "parallel", "arbitrary")),
        name="merge_branches",
    )(ya, yg, yd, p, p, p, w_branch, b_gate.reshape(N_BRANCH, 1, d))


def _outproj_kernel(z_ref, w_ref, x_ref, gl_ref, gc_ref, o_ref, *, tm, seq):
    acc = jnp.dot(z_ref[...], w_ref[...], preferred_element_type=F32)
    row = pl.program_id(1) * tm + lax.broadcasted_iota(jnp.int32, (tm, 1), 0)
    gate = jnp.where(row >= seq, gc_ref[...], gl_ref[...])
    o_ref[...] = x_ref[...] + gate * acc


def _out_proj_residual(z, w_out, xu, gate_l, gate_c, seq, rows):
    bsz, t, d = xu.shape
    tm = _pick(rows, (768, 512, 384, 256, 128))
    tn = 512
    return pl.pallas_call(
        functools.partial(_outproj_kernel, tm=tm, seq=seq),
        grid=(bsz, rows // tm, d // tn),
        in_specs=[
            pl.BlockSpec((None, tm, d), lambda b, i, j: (b, i, 0)),
            pl.BlockSpec((d, tn), lambda b, i, j: (0, j)),
            pl.BlockSpec((None, tm, tn), lambda b, i, j: (b, i, j)),
            pl.BlockSpec((None, 1, tn), lambda b, i, j: (b, 0, j)),
            pl.BlockSpec((1, tn), lambda b, i, j: (0, j)),
        ],
        out_specs=pl.BlockSpec((None, tm, tn), lambda b, i, j: (b, i, j)),
        out_shape=jax.ShapeDtypeStruct((bsz, rows, d), F32),
        compiler_params=_cparams(("parallel", "parallel", "arbitrary")),
        name="out_proj_residual",
    )(z, w_out, xu, gate_l, gate_c)


W1_BLOCK = 2 * LANES


def _w1_prep_kernel(w_ref, perm_ref, o_ref):
    w = w_ref[...].astype(BF16)
    for blk in range(w.shape[1] // W1_BLOCK):
        cs = slice(blk * W1_BLOCK, (blk + 1) * W1_BLOCK)
        o_ref[:, cs] = jnp.dot(w[:, cs], perm_ref[...], preferred_element_type=F32).astype(o_ref.dtype)


def _w1_prep(w_e1):
    nl, ne, d, n = w_e1.shape
    tk = 1024
    j = jnp.arange(W1_BLOCK)
    src = jnp.where(j < LANES, 2 * j, 2 * (j - LANES) + 1)
    perm = (jnp.arange(W1_BLOCK)[:, None] == src[None, :]).astype(BF16)
    return pl.pallas_call(
        _w1_prep_kernel,
        grid=(nl * ne, d // tk),
        in_specs=[pl.BlockSpec((None, tk, n), lambda e, k: (e, k, 0)),
                  pl.BlockSpec((W1_BLOCK, W1_BLOCK), lambda e, k: (0, 0))],
        out_specs=pl.BlockSpec((None, tk, n), lambda e, k: (e, k, 0)),
        out_shape=jax.ShapeDtypeStruct((nl * ne, d, n), BF16),
        compiler_params=_cparams(("parallel", "parallel")),
        name="expert_w1_prep",
    )(w_e1.reshape(nl * ne, d, n), perm)


def _regroup_bias(b_e1):
    ne, n = b_e1.shape
    return jnp.transpose(b_e1.reshape(ne, n // W1_BLOCK, LANES, 2), (0, 1, 3, 2)).reshape(ne, 1, n)


def _expert_kernel(be_ref, bv_ref, x_ref, w1_ref, b1_ref, w2_ref, b2_ref, o_ref, hid_ref):
    i = pl.program_id(0)

    @pl.when(bv_ref[i] > 0)
    def _():
        hgl = jnp.dot(x_ref[...], w1_ref[...], preferred_element_type=F32) + b1_ref[...]
        for blk in range(hgl.shape[1] // W1_BLOCK):
            xg = jnp.minimum(hgl[:, blk * W1_BLOCK:blk * W1_BLOCK + LANES], SWIGLU_LIMIT)
            xl = jnp.clip(hgl[:, blk * W1_BLOCK + LANES:(blk + 1) * W1_BLOCK], -SWIGLU_LIMIT, SWIGLU_LIMIT)
            hid_ref[:, blk * LANES:(blk + 1) * LANES] = (
                xg * _sigmoid(SWIGLU_ALPHA * xg) * (xl + 1.0)).astype(hid_ref.dtype)
        y = jnp.dot(hid_ref[...], w2_ref[...], preferred_element_type=F32) + b2_ref[...]
        o_ref[...] = y.astype(o_ref.dtype)


def _experts(xs, blk_e, blk_valid, w1, b1, w2, b2, e0):
    n_rows, d = xs.shape
    tm = MOE_TM
    ff = EXPERT_FF
    grid_spec = pltpu.PrefetchScalarGridSpec(
        num_scalar_prefetch=2,
        grid=(n_rows // tm,),
        in_specs=[
            pl.BlockSpec((tm, d), lambda i, be, bv: (i, 0)),
            pl.BlockSpec((None, d, 2 * ff), lambda i, be, bv: (e0 + be[i], 0, 0)),
            pl.BlockSpec((None, 1, 2 * ff), lambda i, be, bv: (be[i], 0, 0)),
            pl.BlockSpec((None, ff, d), lambda i, be, bv: (be[i], 0, 0)),
            pl.BlockSpec((None, 1, d), lambda i, be, bv: (be[i], 0, 0)),
        ],
        out_specs=pl.BlockSpec((tm, d), lambda i, be, bv: (i, 0)),
        scratch_shapes=[pltpu.VMEM((tm, ff), BF16)],
    )
    return pl.pallas_call(
        _expert_kernel,
        grid_spec=grid_spec,
        out_shape=jax.ShapeDtypeStruct((n_rows, d), BF16),
        compiler_params=_cparams(("arbitrary",)),
        name="moe_experts",
    )(blk_e, blk_valid, xs, w1, b1, w2, b2)


def _moe(h2, top_i, w1, b1, w2, b2, e0):
    n_tok, d = h2.shape
    tm = MOE_TM
    n_assign = n_tok * TOP_K
    flat_e = top_i.reshape(n_assign)
    order = jnp.argsort(flat_e).astype(jnp.int32)
    rank = jnp.argsort(order).astype(jnp.int32)
    onehot = flat_e[:, None] == jnp.arange(N_EXPERTS, dtype=flat_e.dtype)[None, :]
    counts = jnp.sum(onehot, axis=0, dtype=jnp.int32)
    padded = (counts + tm - 1) // tm * tm
    pad_end = jnp.cumsum(padded)
    start = jnp.cumsum(counts) - counts
    shift = (pad_end - padded) - start
    pos = rank + jnp.sum(jnp.where(onehot, shift[None, :], 0), axis=1)
    n_blocks = -(-n_assign // tm) + N_EXPERTS
    blk_start = jnp.arange(n_blocks, dtype=jnp.int32) * tm
    blk_valid = (blk_start < pad_end[-1]).astype(jnp.int32)
    blk_e = jnp.minimum(jnp.searchsorted(pad_end, blk_start, side='right'), N_EXPERTS - 1).astype(jnp.int32)
    blk_e = jnp.where(blk_valid > 0, blk_e, blk_e[jnp.maximum(pad_end[-1] // tm - 1, 0)])
    src = blk_start[:, None] + jnp.arange(tm, dtype=jnp.int32)[None, :] - shift[blk_e][:, None]
    lo = start[blk_e][:, None]
    live = (src >= lo) & (src < lo + counts[blk_e][:, None]) & (blk_valid[:, None] > 0)
    row_tok = jnp.where(live, order[jnp.clip(src, 0, n_assign - 1)] // TOP_K, 0).reshape(n_blocks * tm)
    xs = h2[row_tok]
    y = _experts(xs, blk_e, blk_valid, w1, b1, w2, b2, e0)
    return y[pos.reshape(n_tok, TOP_K).T.reshape(n_assign)].reshape(TOP_K, n_tok, d)


def _combine_kernel(y_ref, w_ref, x_ref, gl_ref, gc_ref, o_ref, *, tm, seq):
    w = w_ref[...]
    acc = y_ref[0].astype(F32) * w[:, 0:1]
    for k in range(1, TOP_K):
        acc = acc + y_ref[k].astype(F32) * w[:, k:k + 1]
    row = pl.program_id(1) * tm + lax.broadcasted_iota(jnp.int32, (tm, 1), 0)
    gate = jnp.where(row >= seq, gc_ref[...], gl_ref[...])
    o_ref[...] = x_ref[...] + gate * acc


def _moe_combine(yk, top_w, xu, gate_l, gate_c, seq):
    bsz, rows, d = xu.shape
    tm = _pick(rows, (512, 384, 256, 128))
    return pl.pallas_call(
        functools.partial(_combine_kernel, tm=tm, seq=seq),
        grid=(bsz, rows // tm),
        in_specs=[
            pl.BlockSpec((TOP_K, None, tm, d), lambda b, i: (0, b, i, 0)),
            pl.BlockSpec((None, tm, LANES), lambda b, i: (b, i, 0)),
            pl.BlockSpec((None, tm, d), lambda b, i: (b, i, 0)),
            pl.BlockSpec((None, 1, d), lambda b, i: (b, 0, 0)),
            pl.BlockSpec((1, d), lambda b, i: (0, 0)),
        ],
        out_specs=pl.BlockSpec((None, tm, d), lambda b, i: (b, i, 0)),
        out_shape=jax.ShapeDtypeStruct((bsz, rows, d), F32),
        compiler_params=_cparams(("parallel", "parallel")),
        name="moe_combine",
    )(yk, top_w, xu, gate_l, gate_c)


def _final_kernel(x_ref, g_ref, o_ref):
    x = x_ref[...]
    o_ref[...] = x * lax.rsqrt(jnp.mean(x * x, axis=-1, keepdims=True) + RMS_EPS) * g_ref[...]


def _final_norm(xu, g, seq):
    bsz, t, d = xu.shape
    tm = _pick(seq, (512, 256, 128))
    return pl.pallas_call(
        _final_kernel,
        grid=(bsz, seq // tm),
        in_specs=[pl.BlockSpec((None, tm, d), lambda b, i: (b, i, 0)),
                  pl.BlockSpec((1, d), lambda b, i: (0, 0))],
        out_specs=pl.BlockSpec((None, tm, d), lambda b, i: (b, i, 0)),
        out_shape=jax.ShapeDtypeStruct((bsz, seq, d), F32),
        compiler_params=_cparams(("parallel", "parallel")),
        name="final_norm",
    )(xu, g.reshape(1, d))


def _layer(xu, mod_l, mod_c, seq, layer_idx, ctx_out, cos_t, sin_t, norm1, w_in, da_lambda, da_subln,
           gm_ln_g, gm_ln_b, gm_ws, gm_bs, dn_conv, dn_a_log, dn_dt_bias, dn_norm, b_gate, w_branch,
           w_out, norm2, w_router, b_router, w1_all, b_e1, w_e2, b_e2):
    bsz, t, d = xu.shape
    rows = t if ctx_out else seq
    lam_init = 0.8 - 0.6 * math.exp(-0.3 * layer_idx)
    ml = [mod_l[:, k:k + 1, :] for k in range(6)]
    mc = [mod_c[k:k + 1, :] for k in range(6)]

    h = _adaln(xu, norm1.reshape(1, d), ml[0], ml[1], mc[0], mc[1], seq)
    w_main = jnp.concatenate([w_in[:, :OFF_SMALL], w_in[:, OFF_GATE:]], axis=1).astype(BF16)
    w_small = jnp.zeros((d, LANES), BF16).at[:, :OFF_GATE - OFF_SMALL].set(
        w_in[:, OFF_SMALL:OFF_GATE].astype(BF16))
    p = _in_proj_main(h, w_main, cos_t, sin_t)
    small = _in_proj_small(h, w_small)

    ya = _diff_attention(p, da_lambda, da_subln, seq, rows, lam_init)
    yg = _spatial_gating(p, gm_ln_g, gm_ln_b, gm_ws, gm_bs, rows)
    yd = _gated_deltanet(p, small, dn_conv, dn_a_log, dn_dt_bias, dn_norm, seq)
    z = _merge(ya, yg, yd, p, w_branch.astype(BF16), b_gate, rows)
    xu = _out_proj_residual(z, w_out.astype(BF16), xu, ml[2], mc[2], seq, rows)

    h2, top_i, top_w = _adaln_router(xu, norm2.reshape(1, d), ml[3], ml[4], mc[3], mc[4],
                                     w_router, b_router, seq, rows)
    n_tok = bsz * rows
    yk = _moe(h2.reshape(n_tok, d), top_i.reshape(n_tok, LANES)[:, :TOP_K], w1_all, _regroup_bias(b_e1),
              w_e2.astype(BF16), b_e2[:, None, :], layer_idx * N_EXPERTS).reshape(TOP_K, bsz, rows, d)
    return _moe_combine(yk, top_w, xu, ml[5], mc[5], seq)


def kernel(x, c, ctx, c_ctx, w_mod, b_mod, norm1, w_in, da_lambda, da_subln, gm_ln_g, gm_ln_b, gm_ws, gm_bs,
           dn_conv, dn_a_log, dn_dt_bias, dn_norm, b_gate, w_branch, w_out, norm2, w_router, b_router,
           w_e1, b_e1, w_e2, b_e2, norm_f):
    bsz, seq, d = x.shape
    n_ctx = ctx.shape[1]
    t = seq + n_ctx
    depth = w_mod.shape[0]
    xu = jnp.concatenate([x, ctx], axis=1)
    r = -(-(bsz + 1) // 8) * 8
    cond = jnp.zeros((r, d), F32).at[:bsz].set(c).at[bsz].set(c_ctx)
    mod = _modulation(cond, w_mod, b_mod).reshape(depth, r, 6, d)
    cos_t, sin_t = _rope_tables(seq, t)
    w1_all = _w1_prep(w_e1)
    for l in range(depth):
        xu = _layer(xu, mod[l, :bsz], mod[l, bsz], seq, l, l < depth - 1, cos_t, sin_t, norm1[l], w_in[l],
                    da_lambda[l], da_subln[l], gm_ln_g[l], gm_ln_b[l], gm_ws[l], gm_bs[l], dn_conv[l],
                    dn_a_log[l], dn_dt_bias[l], dn_norm[l], b_gate[l], w_branch[l], w_out[l], norm2[l],
                    w_router[l], b_router[l], w1_all, b_e1[l], w_e2[l], b_e2[l])
    return _final_norm(xu, norm_f, seq)
```

```python
import functools
import math

import jax
import jax.numpy as jnp
from jax import lax
from jax.experimental import pallas as pl
from jax.experimental.pallas import tpu as pltpu

F32 = jnp.float32
BF16 = jnp.bfloat16

D_MODEL = 2048
GRID_W = 64
RMS_EPS = 1e-6
BRANCH_W = D_MODEL // 2
N_BRANCH = 3
DA_HD = 64
DA_HEADS = BRANCH_W // (2 * DA_HD)
ROPE_THETA = 10000.0
ROPE_PAIRS_AXIS = DA_HD // 4
GM_CHUNK = 128
GM_GW = 128
GM_GROUPS = BRANCH_W // GM_GW
DN_HD = 128
DN_HEADS = BRANCH_W // DN_HD
DN_CHUNK = 64
DN_CONV = 5
N_EXPERTS = 32
TOP_K = 4
EXPERT_FF = D_MODEL // 2
SWIGLU_LIMIT = 7.0
SWIGLU_ALPHA = 1.702

LANES = 128
VMEM_LIMIT = 56 * 1024 * 1024

OFF_DA_Q = 0
OFF_DA_K = 1024
OFF_DA_V = 2048
OFF_GM_U = 3072
OFF_GM_V = 4096
OFF_DN_Q = 5120
OFF_DN_K = 6144
OFF_DN_V = 7168
OFF_DN_Z = 8192
OFF_SMALL = 9216
OFF_GATE = 9248
N_MAIN = 9216 + N_BRANCH * D_MODEL
OFF_GATE_MAIN = 9216

MOE_TM = 512
MOE_PARTS = 2


def _cparams(sem):
    return pltpu.CompilerParams(dimension_semantics=sem, vmem_limit_bytes=VMEM_LIMIT)


def _pick(n, cands):
    for c in cands:
        if n % c == 0:
            return c
    raise ValueError(f"no tile for {n} in {cands}")


def _sigmoid(x):
    return jax.nn.sigmoid(x)


def _silu(x):
    return x * _sigmoid(x)


def _gelu_tanh(x):
    return x * (0.5 * (1.0 + jnp.tanh(0.7978845608028654 * (x + 0.044715 * (x * x * x)))))


def _bdot(a, b):
    return jnp.dot(a.astype(BF16), b.astype(BF16), preferred_element_type=F32)


def _split(a):
    hi = a.astype(BF16)
    lo = (a - hi.astype(F32)).astype(BF16)
    return hi, lo


def _dot3(a, b):
    ah, al = _split(a)
    bh, bl = _split(b)
    return (jnp.dot(ah, bh, preferred_element_type=F32)
            + (jnp.dot(al, bh, preferred_element_type=F32)
               + jnp.dot(ah, bl, preferred_element_type=F32)))


def _mod_kernel(c_ref, w_ref, b_ref, o_ref):
    s = _silu(c_ref[...])
    o_ref[...] = _bdot(s, w_ref[...]) + b_ref[...]


def _modulation(cond, w_mod, b_mod):
    nl, d, n6 = w_mod.shape
    r = cond.shape[0]
    tn = 1024
    return pl.pallas_call(
        _mod_kernel,
        grid=(nl, n6 // tn),
        in_specs=[
            pl.BlockSpec((r, d), lambda l, j: (0, 0)),
            pl.BlockSpec((None, d, tn), lambda l, j: (l, 0, j)),
            pl.BlockSpec((None, 1, tn), lambda l, j: (l, 0, j)),
        ],
        out_specs=pl.BlockSpec((None, r, tn), lambda l, j: (l, 0, j)),
        out_shape=jax.ShapeDtypeStruct((nl, r, n6), F32),
        compiler_params=_cparams(("arbitrary", "arbitrary")),
        name="modulation",
    )(cond, w_mod, b_mod.reshape(nl, 1, n6))


def _adaln_tile(x, g, shl, scl, shc, scc, row0, seq):
    tm = x.shape[0]
    y = x * lax.rsqrt(jnp.mean(x * x, axis=-1, keepdims=True) + RMS_EPS) * g
    row = row0 + lax.broadcasted_iota(jnp.int32, (tm, 1), 0)
    is_ctx = row >= seq
    scale = jnp.where(is_ctx, scc, scl)
    shift = jnp.where(is_ctx, shc, shl)
    return y * (1.0 + scale) + shift


def _adaln_kernel(x_ref, g_ref, shl_ref, scl_ref, shc_ref, scc_ref, o_ref, *, tm, seq):
    h = _adaln_tile(x_ref[...], g_ref[...], shl_ref[...], scl_ref[...], shc_ref[...], scc_ref[...],
                    pl.program_id(1) * tm, seq)
    o_ref[...] = h.astype(o_ref.dtype)


def _mod_specs(d):
    return [
        pl.BlockSpec((1, d), lambda b, i: (0, 0)),
        pl.BlockSpec((None, 1, d), lambda b, i: (b, 0, 0)),
        pl.BlockSpec((None, 1, d), lambda b, i: (b, 0, 0)),
        pl.BlockSpec((1, d), lambda b, i: (0, 0)),
        pl.BlockSpec((1, d), lambda b, i: (0, 0)),
    ]


def _adaln(xu, g, shl, scl, shc, scc, seq):
    bsz, t, d = xu.shape
    tm = _pick(t, (768, 384, 256, 128))
    return pl.pallas_call(
        functools.partial(_adaln_kernel, tm=tm, seq=seq),
        grid=(bsz, t // tm),
        in_specs=[pl.BlockSpec((None, tm, d), lambda b, i: (b, i, 0))] + _mod_specs(d),
        out_specs=pl.BlockSpec((None, tm, d), lambda b, i: (b, i, 0)),
        out_shape=jax.ShapeDtypeStruct((bsz, t, d), BF16),
        compiler_params=_cparams(("parallel", "parallel")),
        name="adaln",
    )(xu, g, shl, scl, shc, scc)


def _adaln_router_kernel(x_ref, g_ref, shl_ref, scl_ref, shc_ref, scc_ref, wr_ref, br_ref,
                         h_ref, idx_ref, wt_ref, *, tm, seq):
    h = _adaln_tile(x_ref[...], g_ref[...], shl_ref[...], scl_ref[...], shc_ref[...], scc_ref[...],
                    pl.program_id(1) * tm, seq)
    h_ref[...] = h.astype(h_ref.dtype)
    logits = _dot3(h, wr_ref[...]) + br_ref[...]
    lane = lax.broadcasted_iota(jnp.int32, logits.shape, 1).astype(F32)
    vals, idxs = [], []
    cur = logits
    for _ in range(TOP_K):
        m = jnp.max(cur, axis=-1, keepdims=True)
        am = jnp.min(jnp.where(cur == m, lane, float(LANES)), axis=-1, keepdims=True)
        vals.append(m)
        idxs.append(am)
        cur = jnp.where(lane == am, -jnp.inf, cur)
    es = [jnp.exp(v - vals[0]) for v in vals]
    tot = es[0] + es[1] + es[2] + es[3]
    wt = jnp.zeros(logits.shape, F32)
    ix = jnp.zeros(logits.shape, F32)
    for k in range(TOP_K):
        wt = jnp.where(lane == k, es[k] / tot, wt)
        ix = jnp.where(lane == k, idxs[k], ix)
    idx_ref[...] = ix.astype(jnp.int32)
    wt_ref[...] = wt


def _adaln_router(xu, g, shl, scl, shc, scc, w_router, b_router, seq, rows):
    bsz, t, d = xu.shape
    tm = _pick(rows, (768, 512, 384, 256, 128))
    wr = jnp.zeros((d, LANES), F32).at[:, :N_EXPERTS].set(w_router)
    br = jnp.full((1, LANES), -1e30, F32).at[0, :N_EXPERTS].set(b_router)
    return pl.pallas_call(
        functools.partial(_adaln_router_kernel, tm=tm, seq=seq),
        grid=(bsz, rows // tm),
        in_specs=[pl.BlockSpec((None, tm, d), lambda b, i: (b, i, 0))] + _mod_specs(d) + [
            pl.BlockSpec((d, LANES), lambda b, i: (0, 0)),
            pl.BlockSpec((1, LANES), lambda b, i: (0, 0)),
        ],
        out_specs=[
            pl.BlockSpec((None, tm, d), lambda b, i: (b, i, 0)),
            pl.BlockSpec((None, tm, LANES), lambda b, i: (b, i, 0)),
            pl.BlockSpec((None, tm, LANES), lambda b, i: (b, i, 0)),
        ],
        out_shape=[
            jax.ShapeDtypeStruct((bsz, rows, d), BF16),
            jax.ShapeDtypeStruct((bsz, rows, LANES), jnp.int32),
            jax.ShapeDtypeStruct((bsz, rows, LANES), F32),
        ],
        compiler_params=_cparams(("parallel", "parallel")),
        name="adaln_router",
    )(xu, g, shl, scl, shc, scc, wr, br)


def _inproj_kernel(h_ref, w_ref, cos_ref, sin_ref, o_ref, *, tn, n_rope_tiles, n_q_tiles):
    j = pl.program_id(1)
    acc = jnp.dot(h_ref[...], w_ref[...], preferred_element_type=F32)

    @pl.when(j >= n_rope_tiles)
    def _():
        o_ref[...] = acc.astype(o_ref.dtype)

    @pl.when(j < n_rope_tiles)
    def _():
        scale = jnp.where(j < n_q_tiles, DA_HD ** -0.5, 1.0).astype(F32)
        cos = cos_ref[...] * scale
        sin = sin_ref[...] * scale
        lane = lax.broadcasted_iota(jnp.int32, cos.shape, 1)
        first = (lane % DA_HD) < (DA_HD // 2)
        for c in range(tn // LANES):
            a = acc[:, c * LANES:(c + 1) * LANES]
            sw = jnp.where(first, pltpu.roll(a, LANES - DA_HD // 2, 1), pltpu.roll(a, DA_HD // 2, 1))
            o_ref[:, c * LANES:(c + 1) * LANES] = (a * cos + sw * sin).astype(o_ref.dtype)


def _in_proj_main(h, w_main, cos_t, sin_t):
    bsz, t, d = h.shape
    n = w_main.shape[1]
    tn = 512
    return pl.pallas_call(
        functools.partial(_inproj_kernel, tn=tn, n_rope_tiles=OFF_DA_V // tn, n_q_tiles=OFF_DA_K // tn),
        grid=(bsz, n // tn),
        in_specs=[
            pl.BlockSpec((None, t, d), lambda b, j: (b, 0, 0)),
            pl.BlockSpec((d, tn), lambda b, j: (0, j)),
            pl.BlockSpec((t, LANES), lambda b, j: (0, 0)),
            pl.BlockSpec((t, LANES), lambda b, j: (0, 0)),
        ],
        out_specs=pl.BlockSpec((None, t, tn), lambda b, j: (b, 0, j)),
        out_shape=jax.ShapeDtypeStruct((bsz, t, n), BF16),
        compiler_params=_cparams(("parallel", "arbitrary")),
        name="in_proj",
    )(h, w_main, cos_t, sin_t)


def _mm_kernel(x_ref, w_ref, o_ref):
    o_ref[...] = jnp.dot(x_ref[...], w_ref[...], preferred_element_type=F32).astype(o_ref.dtype)


def _in_proj_small(h, w_small):
    bsz, t, d = h.shape
    return pl.pallas_call(
        _mm_kernel,
        grid=(bsz,),
        in_specs=[
            pl.BlockSpec((None, t, d), lambda b: (b, 0, 0)),
            pl.BlockSpec((d, LANES), lambda b: (0, 0)),
        ],
        out_specs=pl.BlockSpec((None, t, LANES), lambda b: (b, 0, 0)),
        out_shape=jax.ShapeDtypeStruct((bsz, t, LANES), F32),
        compiler_params=_cparams(("parallel",)),
        name="in_proj_small",
    )(h, w_small)


def _rope_tables(seq, t):
    rows = seq // GRID_W
    row = jnp.repeat(jnp.arange(rows, dtype=F32), GRID_W)
    col = jnp.tile(jnp.arange(GRID_W, dtype=F32), rows)
    inv = ROPE_THETA ** (-jnp.arange(ROPE_PAIRS_AXIS, dtype=F32) / ROPE_PAIRS_AXIS)
    ang = jnp.concatenate([row[:, None] * inv, col[:, None] * inv], axis=-1)
    cos, sin = jnp.cos(ang), jnp.sin(ang)
    cos_t = jnp.tile(cos, (1, LANES // (DA_HD // 2)))
    sin_t = jnp.tile(jnp.concatenate([-sin, sin], axis=-1), (1, LANES // DA_HD))
    pad = t - seq
    cos_t = jnp.concatenate([cos_t, jnp.ones((pad, LANES), F32)], axis=0)
    sin_t = jnp.concatenate([sin_t, jnp.zeros((pad, LANES), F32)], axis=0)
    return cos_t, sin_t


ATTN_ROW_GROUPS = 4


def _attn_kernel(lam_ref, g_ref, q_ref, k_ref, v_ref, o_ref, *, seq, tq, lam_init):
    qi = pl.program_id(2)
    lp = lam_ref[...]
    l1 = jnp.sum(lp[0:1] * lp[1:2], axis=-1, keepdims=True)
    l2 = jnp.sum(lp[2:3] * lp[3:4], axis=-1, keepdims=True)
    lam = jnp.exp(l1) - jnp.exp(l2) + lam_init
    q = q_ref[...].astype(F32)
    lane = lax.broadcasted_iota(jnp.int32, q.shape, 1)
    qq = jnp.concatenate([jnp.where(lane < DA_HD, q, 0.0), jnp.where(lane >= DA_HD, q, 0.0)],
                         axis=0).astype(BF16)

    def core(k, v):
        rs = 2 * tq // ATTN_ROW_GROUPS
        scores = [lax.dot_general(qq[i * rs:(i + 1) * rs], k, (((1,), (1,)), ((), ())),
                                  preferred_element_type=F32) for i in range(ATTN_ROW_GROUPS)]
        outs = []
        for s in scores:
            m = jnp.max(s, axis=-1, keepdims=True)
            p = jnp.exp(s - m)
            den = jnp.sum(p, axis=-1, keepdims=True)
            outs.append(jnp.dot(p.astype(BF16), v, preferred_element_type=F32) / den)
        o = jnp.concatenate(outs, axis=0)
        o = o[:tq] - lam * o[tq:]
        y = o * lax.rsqrt(jnp.mean(o * o, axis=-1, keepdims=True) + RMS_EPS) * g_ref[...]
        o_ref[...] = (y * (1.0 - lam_init)).astype(o_ref.dtype)

    @pl.when(qi * tq < seq)
    def _():
        core(k_ref[...], v_ref[...])

    @pl.when(qi * tq >= seq)
    def _():
        core(k_ref[seq:, :], v_ref[seq:, :])


def _diff_attention(p, lam_params, subln_g, seq, rows, lam_init):
    bsz, t, _ = p.shape
    tq = _pick(math.gcd(seq, t - seq), (256, 128))
    cq, ck, cv = OFF_DA_Q // LANES, OFF_DA_K // LANES, OFF_DA_V // LANES
    return pl.pallas_call(
        functools.partial(_attn_kernel, seq=seq, tq=tq, lam_init=lam_init),
        grid=(bsz, DA_HEADS, rows // tq),
        in_specs=[
            pl.BlockSpec((4, DA_HD), lambda b, h, i: (0, 0)),
            pl.BlockSpec((1, 2 * DA_HD), lambda b, h, i: (0, 0)),
            pl.BlockSpec((None, tq, LANES), lambda b, h, i: (b, i, cq + h)),
            pl.BlockSpec((None, t, LANES), lambda b, h, i: (b, 0, ck + h)),
            pl.BlockSpec((None, t, LANES), lambda b, h, i: (b, 0, cv + h)),
        ],
        out_specs=pl.BlockSpec((None, tq, LANES), lambda b, h, i: (b, i, h)),
        out_shape=jax.ShapeDtypeStruct((bsz, rows, BRANCH_W), BF16),
        compiler_params=_cparams(("parallel", "parallel", "arbitrary")),
        name="diff_attention",
    )(lam_params, subln_g.reshape(1, -1), p, p, p)


def _gmlp_kernel(u_ref, v_ref, lng_ref, lnb_ref, ws_ref, bs_ref, o_ref, *, nchunks):
    for c in range(nchunks):
        r0 = c * GM_CHUNK
        u = _gelu_tanh(u_ref[r0:r0 + GM_CHUNK, :].astype(F32))
        v = _gelu_tanh(v_ref[r0:r0 + GM_CHUNK, :].astype(F32))
        xc = v - jnp.mean(v, axis=-1, keepdims=True)
        var = jnp.mean(xc * xc, axis=-1, keepdims=True)
        vn = (xc * lax.rsqrt(var + RMS_EPS) * lng_ref[...] + lnb_ref[...]).astype(BF16)
        for g in range(GM_GROUPS):
            cs = slice(g * GM_GW, (g + 1) * GM_GW)
            s = jnp.dot(ws_ref[g], vn[:, cs], preferred_element_type=F32) + bs_ref[g]
            o_ref[r0:r0 + GM_CHUNK, cs] = (u[:, cs] * s).astype(o_ref.dtype)


def _spatial_gating(p, ln_g, ln_b, ws, bs, rows):
    bsz, t, _ = p.shape
    tm = _pick(rows, (768, 512, 384, 256, 128))
    cu, cv = OFF_GM_U // BRANCH_W, OFF_GM_V // BRANCH_W
    bs_b = jnp.broadcast_to(bs[:, :, None], (GM_GROUPS, GM_CHUNK, GM_GW)).astype(F32)
    return pl.pallas_call(
        functools.partial(_gmlp_kernel, nchunks=tm // GM_CHUNK),
        grid=(bsz, rows // tm),
        in_specs=[
            pl.BlockSpec((None, tm, BRANCH_W), lambda b, i: (b, i, cu)),
            pl.BlockSpec((None, tm, BRANCH_W), lambda b, i: (b, i, cv)),
            pl.BlockSpec((1, BRANCH_W), lambda b, i: (0, 0)),
            pl.BlockSpec((1, BRANCH_W), lambda b, i: (0, 0)),
            pl.BlockSpec((GM_GROUPS, GM_CHUNK, GM_CHUNK), lambda b, i: (0, 0, 0)),
            pl.BlockSpec((GM_GROUPS, GM_CHUNK, GM_GW), lambda b, i: (0, 0, 0)),
        ],
        out_specs=pl.BlockSpec((None, tm, BRANCH_W), lambda b, i: (b, i, 0)),
        out_shape=jax.ShapeDtypeStruct((bsz, rows, BRANCH_W), BF16),
        compiler_params=_cparams(("parallel", "parallel")),
        name="spatial_gating",
    )(p, p, ln_g.reshape(1, -1), ln_b.reshape(1, -1), ws.astype(BF16), bs_b)


DN_BASE = 8
DN_PREP_GROUPS = (4, 6, 3, 2, 1)
DN_HEAD_GROUP = 2


def _dn_kernel(alog_ref, dtb_ref, q_ref, k_ref, v_ref, z_ref, sm_ref, ar_ref, cw_ref, ng_ref, o_ref,
               qn_ref, kn_ref, vn_ref, rowg_ref, ac_ref, b_ref, d_ref, cd_ref, oacc_ref, st_ref,
               *, seq, t):
    hg = DN_HEAD_GROUP
    hblk = pl.program_id(1)
    nc = t // DN_CHUNK
    n_lat = seq // DN_CHUNK
    n_ctx = nc - n_lat
    hw = DN_HD
    cw = DN_CHUNK

    row = lax.broadcasted_iota(jnp.int32, (t, 1), 0)
    seg_lo = jnp.where(row < seq, 0, seq)
    seg_hi = jnp.where(row < seq, seq, t)

    def conv_silu(x_ref, w):
        x = x_ref[...].astype(F32)
        acc = x * w[DN_CONV // 2:DN_CONV // 2 + 1, :]
        for s in (-2, -1, 1, 2):
            xs = pltpu.roll(x, (-s) % t, 0)
            rs = row + s
            ok = (rs >= seg_lo) & (rs < seg_hi)
            acc = acc + jnp.where(ok, xs, 0.0) * w[DN_CONV // 2 + s:DN_CONV // 2 + s + 1, :]
        return _silu(acc)

    def l2n(x):
        return x * lax.rsqrt(jnp.sum(x * x, axis=-1, keepdims=True) + RMS_EPS)

    qc = conv_silu(q_ref, cw_ref[0])
    kc = conv_silu(k_ref, cw_ref[1])
    vn_ref[...] = conv_silu(v_ref, cw_ref[2])
    for j in range(hg):
        cs = slice(j * hw, (j + 1) * hw)
        qn_ref[:, cs] = l2n(qc[:, cs]) * (DN_HD ** -0.5)
        kn_ref[:, cs] = l2n(kc[:, cs])

    def softplus(x):
        return jnp.maximum(x, 0.0) + jnp.log1p(jnp.exp(-jnp.abs(x)))

    rw = 2 * hg * cw
    lane_r = lax.broadcasted_iota(jnp.int32, (1, rw), 1)
    chain_r = lane_r // cw
    pos_r = lane_r % cw
    alog_r = jnp.zeros((1, rw), F32)
    dt_r = jnp.zeros((1, rw), F32)
    for d in range(2):
        for j in range(hg):
            alog_r = jnp.where(chain_r == hg * d + j, alog_ref[d, hblk * hg + j], alog_r)
            dt_r = jnp.where(chain_r == hg * d + j, dtb_ref[d, hblk * hg + j], dt_r)
    g_all = -jnp.exp(alog_r) * softplus(ar_ref[...].reshape(nc * 8, rw) + dt_r)
    pre = g_all
    suf = g_all
    sh = 1
    while sh < cw:
        pre = pre + jnp.where(pos_r >= sh, pltpu.roll(pre, sh, 1), 0.0)
        suf = suf + jnp.where(pos_r < cw - sh, pltpu.roll(suf, rw - sh, 1), 0.0)
        sh *= 2
    run = jnp.where(lane_r >= hg * cw, suf, pre).reshape(nc, 8, rw)
    tot = (pre + suf - g_all).reshape(nc, 8, rw)
    sub = lax.broadcasted_iota(jnp.int32, (nc, 8, rw), 1)
    both = jnp.where(sub == 0, run, tot)
    for d in range(2):
        rowg_ref[d] = both[:, :, d * hg * cw:(d + 1) * hg * cw]

    st_ref[...] = jnp.zeros(st_ref.shape, F32)

    pshape = (cw, hg * cw)
    ii = lax.broadcasted_iota(jnp.int32, pshape, 0)
    lp = lax.broadcasted_iota(jnp.int32, pshape, 1)
    jl = lp % cw
    left = lp < cw
    diag = ii == jl
    eye_p = jnp.where(diag, 1.0, 0.0).astype(F32)
    blk_base = (ii // DN_BASE) == (jl // DN_BASE)
    incl = [ii >= jl, ii <= jl]
    strict = [ii > jl, ii < jl]
    half = [jnp.where(left, 1.0, 0.0).astype(BF16), jnp.where(left, 0.0, 1.0).astype(BF16)]
    left_sq = lax.broadcasted_iota(jnp.int32, (LANES, LANES), 1) < cw

    def blockdiag(b16):
        return jnp.concatenate([b16 * half[0], b16 * half[1]], axis=0)

    def pprod(a, b):
        return jnp.dot(a.astype(BF16), blockdiag(b.astype(BF16)), preferred_element_type=F32)

    def tri_inverse(lmats):
        ms = [jnp.where(blk_base, -l, 0.0) for l in lmats]
        xs = [eye_p + m for m in ms]
        pws = [pprod(m, m) for m in ms]
        span = 4
        while span <= DN_BASE:
            tts = [pprod(jnp.concatenate([x, pw], axis=0), pw) for x, pw in zip(xs, pws)]
            xs = [x + tt[:cw] for x, tt in zip(xs, tts)]
            pws = [tt[cw:] for tt in tts]
            span *= 2
        bs = DN_BASE
        while bs < cw:
            off = ((ii // (2 * bs)) == (jl // (2 * bs))) & ((ii // bs) != (jl // bs))
            cmats = [jnp.where(off, l, 0.0) for l in lmats]
            ys = [pprod(x, c) for x, c in zip(xs, cmats)]
            zs = [pprod(y, x) for y, x in zip(ys, xs)]
            xs = [x - z for x, z in zip(xs, zs)]
            bs *= 2
        return xs

    def prep_load(c):
        rows = pl.ds(pl.multiple_of(c * cw, cw), cw)
        kk = [kn_ref[rows, j * hw:(j + 1) * hw] for j in range(hg)]
        qq = [qn_ref[rows, j * hw:(j + 1) * hw] for j in range(hg)]
        vv = [vn_ref[rows, j * hw:(j + 1) * hw] for j in range(hg)]
        return kk, qq, vv, sm_ref[rows, :], [rowg_ref[d, c] for d in range(2)]

    def prep_compute(loaded):
        n = len(loaded)
        gram, qk, ktp = [], [], []
        for kk, qq, vv, sm, rgs in loaded:
            gq, kt = [], []
            for j in range(hg):
                kb = kk[j].astype(BF16)
                gq.append(lax.dot_general(jnp.concatenate([kb, qq[j].astype(BF16)], axis=0),
                                          jnp.concatenate([kb, kb], axis=0), (((1,), (1,)), ((), ())),
                                          preferred_element_type=F32))
                kt.append(jnp.concatenate([kk[j], kk[j]], axis=0).T)
            pair = jnp.where(left_sq, gq[0], gq[1])
            gram.append(pair[:cw])
            qk.append(pair[cw:])
            ktp.append(jnp.where(left_sq, kt[0], kt[1]))
        pre = []
        for ci, (kk, qq, vv, sm, rgs) in enumerate(loaded):
            lane_c = lax.broadcasted_iota(jnp.int32, sm.shape, 1)
            for d in range(2):
                bcol = [_sigmoid(jnp.sum(jnp.where(lane_c == d * DN_HEADS + hblk * hg + j, sm, 0.0),
                                         axis=1, keepdims=True)) for j in range(hg)]
                gc_row, g_tot = rgs[d][0:1, :], rgs[d][1:2, :]
                gdiag = jnp.where(diag, gc_row, 0.0)
                gcol = [jnp.sum(jnp.where(left, gdiag, 0.0), axis=1, keepdims=True),
                        jnp.sum(jnp.where(left, 0.0, gdiag), axis=1, keepdims=True)]
                gc = jnp.where(left, gcol[0], gcol[1])
                beta = jnp.where(left, bcol[0], bcol[1])
                dec = jnp.exp(jnp.where(incl[d], gc - gc_row, -jnp.inf))
                lmat = jnp.where(strict[d], beta * gram[ci] * dec, 0.0)
                pre.append((ci, d, bcol, gcol, gc_row, g_tot, dec, lmat))
        tinvs = tri_inverse([p[-1] for p in pre])
        rhss, egs = [], []
        for ci, d, bcol, gcol, gc_row, g_tot, dec, lmat in pre:
            kk, qq, vv = loaded[ci][:3]
            eg = [jnp.exp(gcol[j]) for j in range(hg)]
            egs.append(eg)
            rhss.append(jnp.concatenate(
                [jnp.concatenate([vv[j] * bcol[j], kk[j] * (bcol[j] * eg[j])], axis=1) for j in range(hg)],
                axis=0).astype(BF16))
        sols = []
        for tinv, rhs in zip(tinvs, rhss):
            t16 = tinv.astype(BF16)
            sols.append(jnp.dot(jnp.concatenate([t16 * half[0], t16 * half[1]], axis=0), rhs,
                                preferred_element_type=F32))
        xs = []
        for (ci, d, bcol, gcol, gc_row, g_tot, dec, lmat), sol in zip(pre, sols):
            q_intra = jnp.where(incl[d], qk[ci] * dec, 0.0)
            qk2 = jnp.concatenate([q_intra, ktp[ci] * jnp.exp(g_tot - gc_row)], axis=0).astype(BF16)
            s16 = sol.astype(BF16)
            zero = jnp.zeros((cw, 2 * hw), BF16)
            bd = jnp.concatenate([jnp.concatenate([s16[:cw], zero], axis=1),
                                  jnp.concatenate([zero, s16[cw:]], axis=1)], axis=0)
            xs.append(jnp.dot(qk2, bd, preferred_element_type=F32))
        outs = [[None, None] for _ in range(n)]
        for (ci, d, bcol, gcol, gc_row, g_tot, dec, lmat), x, eg in zip(pre, xs, egs):
            qq = loaded[ci][1]
            e_tot = jnp.exp(g_tot)
            outs[ci][d] = dict(
                ac=[jnp.concatenate([x[cw:, j * 2 * hw + hw:(j + 1) * 2 * hw],
                                     qq[j] * eg[j] - x[:cw, j * 2 * hw + hw:(j + 1) * 2 * hw]], axis=0).astype(BF16)
                    for j in range(hg)],
                b=[x[cw:, j * 2 * hw:j * 2 * hw + hw].astype(BF16) for j in range(hg)],
                dd=[x[:cw, j * 2 * hw:j * 2 * hw + hw].astype(BF16) for j in range(hg)],
                cd=[jnp.broadcast_to(e_tot[:, j * cw:j * cw + 1], (1, hw)) for j in range(hg)])
        return outs

    def prep_store(c, outs):
        rows = pl.ds(pl.multiple_of(c * cw, cw), cw)
        for d in range(2):
            for j in range(hg):
                ac_ref[hg * d + j, c] = outs[d]["ac"][j]
                b_ref[hg * d + j, c] = outs[d]["b"][j]
                d_ref[hg * d + j, rows, :] = outs[d]["dd"][j]
                cd_ref[hg * d + j, c] = outs[d]["cd"][j]

    group = _pick(nc, DN_PREP_GROUPS)

    def prep_body(g, carry):
        cs = [g * group + cc for cc in range(group)]
        loaded = [prep_load(c) for c in cs]
        outs = prep_compute(loaded)
        for c, o in zip(cs, outs):
            prep_store(c, o)
        return carry

    lax.fori_loop(0, nc // group, prep_body, 0)

    def scan_body(i, carry):
        cf = jnp.where(i < n_ctx, i + n_lat, i - n_ctx)
        cb = nc - 1 - i
        dirs = ((0, cf), (1, cb))
        rows = [pl.ds(pl.multiple_of(c * cw, cw), cw) for _, c in dirs]
        state = [st_ref[s] for s in range(2 * hg)]
        ac = [ac_ref[hg * d + j, c] for d, c in dirs for j in range(hg)]
        bb = [b_ref[hg * d + j, c] for d, c in dirs for j in range(hg)]
        dd = [d_ref[hg * d + j, rows[d], :] for d, _ in dirs for j in range(hg)]
        cd = [cd_ref[hg * d + j, c] for d, c in dirs for j in range(hg)]
        rs = [jnp.dot(ac[s], state[s].astype(BF16), preferred_element_type=F32) for s in range(2 * hg)]
        o_new = [jnp.concatenate([rs[hg * d + j][hw:] + dd[hg * d + j].astype(F32) for j in range(hg)], axis=1)
                 for d in range(2)]
        st_new = [state[s] * cd[s] - rs[s][:hw] + bb[s].astype(F32) for s in range(2 * hg)]
        for d, _ in dirs:
            oacc_ref[d, rows[d], :] = o_new[d]
        for s in range(2 * hg):
            st_ref[s] = st_new[s]
        return carry

    lax.fori_loop(0, nc, scan_body, 0)

    o = oacc_ref[0] + oacc_ref[1]
    for j in range(hg):
        cs = slice(j * hw, (j + 1) * hw)
        oj = o[:, cs]
        y = oj * lax.rsqrt(jnp.mean(oj * oj, axis=-1, keepdims=True) + RMS_EPS) * ng_ref[...]
        o_ref[:, cs] = (y * _silu(z_ref[:, cs].astype(F32))).astype(o_ref.dtype)


def _gated_deltanet(p, small, conv_w, a_log, dt_bias, norm_g, seq):
    bsz, t, _ = p.shape
    nc = t // DN_CHUNK
    hg = DN_HEAD_GROUP
    assert hg == 2
    w = hg * DN_HD
    rw = 2 * hg * DN_CHUNK
    a = small[..., 2 * DN_HEADS:4 * DN_HEADS].reshape(bsz, nc, DN_CHUNK, 2, DN_HEADS // hg, hg)
    a_row = jnp.transpose(a, (0, 4, 1, 3, 5, 2)).reshape(bsz, DN_HEADS // hg, nc, 1, rw)
    a_row = jnp.broadcast_to(a_row, (bsz, DN_HEADS // hg, nc, 8, rw))
    cq, ck, cv, cz = (OFF_DN_Q // w, OFF_DN_K // w, OFF_DN_V // w, OFF_DN_Z // w)
    slab = lambda c0: pl.BlockSpec((None, t, w), lambda b, h: (b, 0, c0 + h))
    smem = pl.BlockSpec(memory_space=pltpu.SMEM)
    return pl.pallas_call(
        functools.partial(_dn_kernel, seq=seq, t=t),
        grid=(bsz, DN_HEADS // hg),
        in_specs=[
            smem, smem,
            slab(cq), slab(ck), slab(cv), slab(cz),
            pl.BlockSpec((None, t, LANES), lambda b, h: (b, 0, 0)),
            pl.BlockSpec((None, None, nc, 8, rw), lambda b, h: (b, h, 0, 0, 0)),
            pl.BlockSpec((3, DN_CONV, w), lambda b, h: (0, 0, h)),
            pl.BlockSpec((1, DN_HD), lambda b, h: (0, 0)),
        ],
        out_specs=pl.BlockSpec((None, t, w), lambda b, h: (b, 0, h)),
        out_shape=jax.ShapeDtypeStruct((bsz, t, BRANCH_W), BF16),
        scratch_shapes=[
            pltpu.VMEM((t, w), F32), pltpu.VMEM((t, w), F32), pltpu.VMEM((t, w), F32),
            pltpu.VMEM((2, nc, 8, hg * DN_CHUNK), F32),
            pltpu.VMEM((2 * hg, nc, DN_HD + DN_CHUNK, DN_HD), BF16),
            pltpu.VMEM((2 * hg, nc, DN_HD, DN_HD), BF16),
            pltpu.VMEM((2 * hg, t, DN_HD), BF16),
            pltpu.VMEM((2 * hg, nc, 1, DN_HD), F32),
            pltpu.VMEM((2, t, w), F32),
            pltpu.VMEM((2 * hg, DN_HD, DN_HD), F32),
        ],
        compiler_params=_cparams(("parallel", "parallel")),
        name="gated_deltanet",
    )(a_log, dt_bias, p, p, p, p, small, a_row, conv_w, norm_g.reshape(1, -1))


def _merge_kernel(ya_ref, yg_ref, yd_ref, ga_ref, gg_ref, gd_ref, wb_ref, bg_ref, o_ref):
    acc = None
    for i, (y_ref, g_ref) in enumerate(((ya_ref, ga_ref), (yg_ref, gg_ref), (yd_ref, gd_ref))):
        gate = _sigmoid(g_ref[...].astype(F32) + bg_ref[i])
        term = gate * jnp.dot(y_ref[...], wb_ref[i], preferred_element_type=F32)
        acc = term if acc is None else acc + term
    o_ref[...] = acc.astype(o_ref.dtype)


def _merge(ya, yg, yd, p, w_branch, b_gate, rows):
    bsz = p.shape[0]
    d = D_MODEL
    tm = _pick(rows, (768, 512, 384, 256, 128))
    tn = 512
    g0 = OFF_GATE_MAIN // tn
    y_spec = pl.BlockSpec((None, tm, BRANCH_W), lambda b, i, j: (b, i, 0))
    gate_spec = lambda k: pl.BlockSpec((None, tm, tn), lambda b, i, j: (b, i, g0 + k * (d // tn) + j))
    return pl.pallas_call(
        _merge_kernel,
        grid=(bsz, rows // tm, d // tn),
        in_specs=[y_spec, y_spec, y_spec, gate_spec(0), gate_spec(1), gate_spec(2),
                  pl.BlockSpec((N_BRANCH, BRANCH_W, tn), lambda b, i, j: (0, 0, j)),
                  pl.BlockSpec((N_BRANCH, 1, tn), lambda b, i, j: (0, 0, j))],
        out_specs=pl.BlockSpec((None, tm, tn), lambda b, i, j: (b, i, j)),
        out_shape=jax.ShapeDtypeStruct((bsz, rows, d), BF16),
        compiler_params=_cparams(("parallel", "parallel", "arbitrary")),
        name="merge_branches",
    )(ya, yg, yd, p, p, p, w_branch, b_gate.reshape(N_BRANCH, 1, d))


def _outproj_kernel(z_ref, w_ref, x_ref, gl_ref, gc_ref, o_ref, *, tm, seq):
    acc = jnp.dot(z_ref[...], w_ref[...], preferred_element_type=F32)
    row = pl.program_id(1) * tm + lax.broadcasted_iota(jnp.int32, (tm, 1), 0)
    gate = jnp.where(row >= seq, gc_ref[...], gl_ref[...])
    o_ref[...] = x_ref[...] + gate * acc


def _out_proj_residual(z, w_out, xu, gate_l, gate_c, seq, rows):
    bsz, t, d = xu.shape
    tm = _pick(rows, (768, 512, 384, 256, 128))
    tn = 512
    return pl.pallas_call(
        functools.partial(_outproj_kernel, tm=tm, seq=seq),
        grid=(bsz, rows // tm, d // tn),
        in_specs=[
            pl.BlockSpec((None, tm, d), lambda b, i, j: (b, i, 0)),
            pl.BlockSpec((d, tn), lambda b, i, j: (0, j)),
            pl.BlockSpec((None, tm, tn), lambda b, i, j: (b, i, j)),
            pl.BlockSpec((None, 1, tn), lambda b, i, j: (b, 0, j)),
            pl.BlockSpec((1, tn), lambda b, i, j: (0, j)),
        ],
        out_specs=pl.BlockSpec((None, tm, tn), lambda b, i, j: (b, i, j)),
        out_shape=jax.ShapeDtypeStruct((bsz, rows, d), F32),
        compiler_params=_cparams(("parallel", "parallel", "arbitrary")),
        name="out_proj_residual",
    )(z, w_out, xu, gate_l, gate_c)


W1_BLOCK = 2 * LANES


def _w1_prep_kernel(w_ref, perm_ref, o_ref):
    w = w_ref[...].astype(BF16)
    for blk in range(w.shape[1] // W1_BLOCK):
        cs = slice(blk * W1_BLOCK, (blk + 1) * W1_BLOCK)
        o_ref[:, cs] = jnp.dot(w[:, cs], perm_ref[...], preferred_element_type=F32).astype(o_ref.dtype)


def _w1_prep(w_e1):
    nl, ne, d, n = w_e1.shape
    tk = 1024
    j = jnp.arange(W1_BLOCK)
    src = jnp.where(j < LANES, 2 * j, 2 * (j - LANES) + 1)
    perm = (jnp.arange(W1_BLOCK)[:, None] == src[None, :]).astype(BF16)
    return pl.pallas_call(
        _w1_prep_kernel,
        grid=(nl * ne, d // tk),
        in_specs=[pl.BlockSpec((None, tk, n), lambda e, k: (e, k, 0)),
                  pl.BlockSpec((W1_BLOCK, W1_BLOCK), lambda e, k: (0, 0))],
        out_specs=pl.BlockSpec((None, tk, n), lambda e, k: (e, k, 0)),
        out_shape=jax.ShapeDtypeStruct((nl * ne, d, n), BF16),
        compiler_params=_cparams(("parallel", "parallel")),
        name="expert_w1_prep",
    )(w_e1.reshape(nl * ne, d, n), perm)


def _regroup_bias(b_e1):
    ne, n = b_e1.shape
    return jnp.transpose(b_e1.reshape(ne, n // W1_BLOCK, LANES, 2), (0, 1, 3, 2)).reshape(ne, 1, n)


def _expert_kernel(be_ref, bv_ref, x_ref, w1_ref, b1_ref, w2_ref, b2_ref, o_ref, hid_ref):
    i = pl.program_id(0)

    @pl.when(bv_ref[i] > 0)
    def _():
        hgl = jnp.dot(x_ref[...], w1_ref[...], preferred_element_type=F32) + b1_ref[...]
        for blk in range(hgl.shape[1] // W1_BLOCK):
            xg = jnp.minimum(hgl[:, blk * W1_BLOCK:blk * W1_BLOCK + LANES], SWIGLU_LIMIT)
            xl = jnp.clip(hgl[:, blk * W1_BLOCK + LANES:(blk + 1) * W1_BLOCK], -SWIGLU_LIMIT, SWIGLU_LIMIT)
            hid_ref[:, blk * LANES:(blk + 1) * LANES] = (
                xg * _sigmoid(SWIGLU_ALPHA * xg) * (xl + 1.0)).astype(hid_ref.dtype)
        y = jnp.dot(hid_ref[...], w2_ref[...], preferred_element_type=F32) + b2_ref[...]
        o_ref[...] = y.astype(o_ref.dtype)


def _experts(xs, blk_e, blk_valid, w1, b1, w2, b2, e0):
    n_rows, d = xs.shape
    tm = MOE_TM
    ff = EXPERT_FF
    grid_spec = pltpu.PrefetchScalarGridSpec(
        num_scalar_prefetch=2,
        grid=(n_rows // tm,),
        in_specs=[
            pl.BlockSpec((tm, d), lambda i, be, bv: (i, 0)),
            pl.BlockSpec((None, d, 2 * ff), lambda i, be, bv: (e0 + be[i], 0, 0)),
            pl.BlockSpec((None, 1, 2 * ff), lambda i, be, bv: (be[i], 0, 0)),
            pl.BlockSpec((None, ff, d), lambda i, be, bv: (be[i], 0, 0)),
            pl.BlockSpec((None, 1, d), lambda i, be, bv: (be[i], 0, 0)),
        ],
        out_specs=pl.BlockSpec((tm, d), lambda i, be, bv: (i, 0)),
        scratch_shapes=[pltpu.VMEM((tm, ff), BF16)],
    )
    return pl.pallas_call(
        _expert_kernel,
        grid_spec=grid_spec,
        out_shape=jax.ShapeDtypeStruct((n_rows, d), BF16),
        compiler_params=_cparams(("arbitrary",)),
        name="moe_experts",
    )(blk_e, blk_valid, xs, w1, b1, w2, b2)


def _moe(h2, top_i, w1, b1, w2, b2, e0):
    n_tok, d = h2.shape
    tm = MOE_TM
    n_assign = n_tok * TOP_K
    flat_e = top_i.reshape(n_assign)
    order = jnp.argsort(flat_e).astype(jnp.int32)
    rank = jnp.argsort(order).astype(jnp.int32)
    onehot = flat_e[:, None] == jnp.arange(N_EXPERTS, dtype=flat_e.dtype)[None, :]
    counts = jnp.sum(onehot, axis=0, dtype=jnp.int32)
    padded = (counts + tm - 1) // tm * tm
    pad_end = jnp.cumsum(padded)
    start = jnp.cumsum(counts) - counts
    shift = (pad_end - padded) - start
    pos = rank + jnp.sum(jnp.where(onehot, shift[None, :], 0), axis=1)
    n_blocks = -(-n_assign // tm) + N_EXPERTS
    blk_start = jnp.arange(n_blocks, dtype=jnp.int32) * tm
    blk_valid = (blk_start < pad_end[-1]).astype(jnp.int32)
    blk_e = jnp.minimum(jnp.searchsorted(pad_end, blk_start, side='right'), N_EXPERTS - 1).astype(jnp.int32)
    blk_e = jnp.where(blk_valid > 0, blk_e, blk_e[jnp.maximum(pad_end[-1] // tm - 1, 0)])
    src = blk_start[:, None] + jnp.arange(tm, dtype=jnp.int32)[None, :] - shift[blk_e][:, None]
    lo = start[blk_e][:, None]
    live = (src >= lo) & (src < lo + counts[blk_e][:, None]) & (blk_valid[:, None] > 0)
    row_tok = jnp.where(live, order[jnp.clip(src, 0, n_assign - 1)] // TOP_K, 0).reshape(n_blocks * tm)
    xs = h2[row_tok]
    y = _experts(xs, blk_e, blk_valid, w1, b1, w2, b2, e0)
    return y[pos.reshape(n_tok, TOP_K).T.reshape(n_assign)].reshape(TOP_K, n_tok, d)


def _combine_kernel(y_ref, w_ref, x_ref, gl_ref, gc_ref, *rest, tm, seq):
    o_ref = rest[-1]
    w = w_ref[...]
    acc = y_ref[0].astype(F32) * w[:, 0:1]
    for k in range(1, TOP_K):
        acc = acc + y_ref[k].astype(F32) * w[:, k:k + 1]
    row = pl.program_id(1) * tm + lax.broadcasted_iota(jnp.int32, (tm, 1), 0)
    gate = jnp.where(row >= seq, gc_ref[...], gl_ref[...])
    o_ref[...] = x_ref[...] + gate * acc


def _moe_combine(yk, top_w, xu, gate_l, gate_c, seq, b0, prev):
    bsz, rows, d = xu.shape
    bp = yk.shape[1]
    tm = _pick(rows, (512, 384, 256, 128))
    in_specs = [
        pl.BlockSpec((TOP_K, None, tm, d), lambda b, i: (0, b, i, 0)),
        pl.BlockSpec((None, tm, LANES), lambda b, i: (b + b0, i, 0)),
        pl.BlockSpec((None, tm, d), lambda b, i: (b + b0, i, 0)),
        pl.BlockSpec((None, 1, d), lambda b, i: (b + b0, 0, 0)),
        pl.BlockSpec((1, d), lambda b, i: (0, 0)),
    ]
    args = [yk, top_w, xu, gate_l, gate_c]
    aliases = {}
    if prev is not None:
        in_specs.append(pl.BlockSpec(memory_space=pl.ANY))
        args.append(prev)
        aliases = {len(args) - 1: 0}
    return pl.pallas_call(
        functools.partial(_combine_kernel, tm=tm, seq=seq),
        grid=(bp, rows // tm),
        in_specs=in_specs,
        out_specs=pl.BlockSpec((None, tm, d), lambda b, i: (b + b0, i, 0)),
        out_shape=jax.ShapeDtypeStruct((bsz, rows, d), F32),
        input_output_aliases=aliases,
        compiler_params=_cparams(("parallel", "parallel")),
        name="moe_combine",
    )(*args)


def _final_kernel(x_ref, g_ref, o_ref):
    x = x_ref[...]
    o_ref[...] = x * lax.rsqrt(jnp.mean(x * x, axis=-1, keepdims=True) + RMS_EPS) * g_ref[...]


def _final_norm(xu, g, seq):
    bsz, t, d = xu.shape
    tm = _pick(seq, (512, 256, 128))
    return pl.pallas_call(
        _final_kernel,
        grid=(bsz, seq // tm),
        in_specs=[pl.BlockSpec((None, tm, d), lambda b, i: (b, i, 0)),
                  pl.BlockSpec((1, d), lambda b, i: (0, 0))],
        out_specs=pl.BlockSpec((None, tm, d), lambda b, i: (b, i, 0)),
        out_shape=jax.ShapeDtypeStruct((bsz, seq, d), F32),
        compiler_params=_cparams(("parallel", "parallel")),
        name="final_norm",
    )(xu, g.reshape(1, d))


def _layer(xu, mod_l, mod_c, seq, layer_idx, ctx_out, cos_t, sin_t, norm1, w_in, da_lambda, da_subln,
           gm_ln_g, gm_ln_b, gm_ws, gm_bs, dn_conv, dn_a_log, dn_dt_bias, dn_norm, b_gate, w_branch,
           w_out, norm2, w_router, b_router, w1_all, b_e1, w_e2, b_e2):
    bsz, t, d = xu.shape
    rows = t if ctx_out else seq
    lam_init = 0.8 - 0.6 * math.exp(-0.3 * layer_idx)
    ml = [mod_l[:, k:k + 1, :] for k in range(6)]
    mc = [mod_c[k:k + 1, :] for k in range(6)]

    h = _adaln(xu, norm1.reshape(1, d), ml[0], ml[1], mc[0], mc[1], seq)
    w_main = jnp.concatenate([w_in[:, :OFF_SMALL], w_in[:, OFF_GATE:]], axis=1).astype(BF16)
    w_small = jnp.zeros((d, LANES), BF16).at[:, :OFF_GATE - OFF_SMALL].set(
        w_in[:, OFF_SMALL:OFF_GATE].astype(BF16))
    p = _in_proj_main(h, w_main, cos_t, sin_t)
    small = _in_proj_small(h, w_small)

    ya = _diff_attention(p, da_lambda, da_subln, seq, rows, lam_init)
    yg = _spatial_gating(p, gm_ln_g, gm_ln_b, gm_ws, gm_bs, rows)
    yd = _gated_deltanet(p, small, dn_conv, dn_a_log, dn_dt_bias, dn_norm, seq)
    z = _merge(ya, yg, yd, p, w_branch.astype(BF16), b_gate, rows)
    xu = _out_proj_residual(z, w_out.astype(BF16), xu, ml[2], mc[2], seq, rows)

    h2, top_i, top_w = _adaln_router(xu, norm2.reshape(1, d), ml[3], ml[4], mc[3], mc[4],
                                     w_router, b_router, seq, rows)
    parts = MOE_PARTS if bsz % MOE_PARTS == 0 else 1
    bp = bsz // parts
    b1p, w2p, b2p = _regroup_bias(b_e1), w_e2.astype(BF16), b_e2[:, None, :]
    yks = [_moe(h2[i * bp:(i + 1) * bp].reshape(bp * rows, d),
                top_i[i * bp:(i + 1) * bp].reshape(bp * rows, LANES)[:, :TOP_K],
                w1_all, b1p, w2p, b2p, layer_idx * N_EXPERTS).reshape(TOP_K, bp, rows, d) for i in range(parts)]
    out = None
    for i in range(parts):
        out = _moe_combine(yks[i], top_w, xu, ml[5], mc[5], seq, i * bp, out)
    return out


def kernel(x, c, ctx, c_ctx, w_mod, b_mod, norm1, w_in, da_lambda, da_subln, gm_ln_g, gm_ln_b, gm_ws, gm_bs,
           dn_conv, dn_a_log, dn_dt_bias, dn_norm, b_gate, w_branch, w_out, norm2, w_router, b_router,
           w_e1, b_e1, w_e2, b_e2, norm_f):
    bsz, seq, d = x.shape
    n_ctx = ctx.shape[1]
    t = seq + n_ctx
    depth = w_mod.shape[0]
    xu = jnp.concatenate([x, ctx], axis=1)
    r = -(-(bsz + 1) // 8) * 8
    cond = jnp.zeros((r, d), F32).at[:bsz].set(c).at[bsz].set(c_ctx)
    mod = _modulation(cond, w_mod, b_mod).reshape(depth, r, 6, d)
    cos_t, sin_t = _rope_tables(seq, t)
    w1_all = _w1_prep(w_e1)
    for l in range(depth):
        xu = _layer(xu, mod[l, :bsz], mod[l, bsz], seq, l, l < depth - 1, cos_t, sin_t, norm1[l], w_in[l],
                    da_lambda[l], da_subln[l], gm_ln_g[l], gm_ln_b[l], gm_ws[l], gm_bs[l], dn_conv[l],
                    dn_a_log[l], dn_dt_bias[l], dn_norm[l], b_gate[l], w_branch[l], w_out[l], norm2[l],
                    w_router[l], b_router[l], w1_all, b_e1[l], w_e2[l], b_e2[l])
    return _final_norm(xu, norm_f, seq)
```

```python
import functools
import math

import jax
import jax.numpy as jnp
from jax import lax
from jax.experimental import pallas as pl
from jax.experimental.pallas import tpu as pltpu

F32 = jnp.float32
BF16 = jnp.bfloat16

D_MODEL = 2048
GRID_W = 64
RMS_EPS = 1e-6
BRANCH_W = D_MODEL // 2
N_BRANCH = 3
DA_HD = 64
DA_HEADS = BRANCH_W // (2 * DA_HD)
ROPE_THETA = 10000.0
ROPE_PAIRS_AXIS = DA_HD // 4
GM_CHUNK = 128
GM_GW = 128
GM_GROUPS = BRANCH_W // GM_GW
DN_HD = 128
DN_HEADS = BRANCH_W // DN_HD
DN_CHUNK = 64
DN_CONV = 5
N_EXPERTS = 32
TOP_K = 4
EXPERT_FF = D_MODEL // 2
SWIGLU_LIMIT = 7.0
SWIGLU_ALPHA = 1.702

LANES = 128
VMEM_LIMIT = 56 * 1024 * 1024

OFF_DA_Q = 0
OFF_DA_K = 1024
OFF_DA_V = 2048
OFF_GM_U = 3072
OFF_GM_V = 4096
OFF_DN_Q = 5120
OFF_DN_K = 6144
OFF_DN_V = 7168
OFF_DN_Z = 8192
OFF_SMALL = 9216
OFF_GATE = 9248
N_MAIN = 9216 + N_BRANCH * D_MODEL
OFF_GATE_MAIN = 9216

MOE_TM = 512
MOE_PARTS = 1


def _cparams(sem):
    return pltpu.CompilerParams(dimension_semantics=sem, vmem_limit_bytes=VMEM_LIMIT)


def _pick(n, cands):
    for c in cands:
        if n % c == 0:
            return c
    raise ValueError(f"no tile for {n} in {cands}")


def _sigmoid(x):
    return jax.nn.sigmoid(x)


def _silu(x):
    return x * _sigmoid(x)


def _gelu_tanh(x):
    return x * (0.5 * (1.0 + jnp.tanh(0.7978845608028654 * (x + 0.044715 * (x * x * x)))))


def _bdot(a, b):
    return jnp.dot(a.astype(BF16), b.astype(BF16), preferred_element_type=F32)


def _split(a):
    hi = a.astype(BF16)
    lo = (a - hi.astype(F32)).astype(BF16)
    return hi, lo


def _dot3(a, b):
    ah, al = _split(a)
    bh, bl = _split(b)
    return (jnp.dot(ah, bh, preferred_element_type=F32)
            + (jnp.dot(al, bh, preferred_element_type=F32)
               + jnp.dot(ah, bl, preferred_element_type=F32)))


def _mod_kernel(c_ref, w_ref, b_ref, o_ref):
    s = _silu(c_ref[...])
    o_ref[...] = _bdot(s, w_ref[...]) + b_ref[...]


def _modulation(cond, w_mod, b_mod):
    nl, d, n6 = w_mod.shape
    r = cond.shape[0]
    tn = 1024
    return pl.pallas_call(
        _mod_kernel,
        grid=(nl, n6 // tn),
        in_specs=[
            pl.BlockSpec((r, d), lambda l, j: (0, 0)),
            pl.BlockSpec((None, d, tn), lambda l, j: (l, 0, j)),
            pl.BlockSpec((None, 1, tn), lambda l, j: (l, 0, j)),
        ],
        out_specs=pl.BlockSpec((None, r, tn), lambda l, j: (l, 0, j)),
        out_shape=jax.ShapeDtypeStruct((nl, r, n6), F32),
        compiler_params=_cparams(("arbitrary", "arbitrary")),
        name="modulation",
    )(cond, w_mod, b_mod.reshape(nl, 1, n6))


def _adaln_tile(x, g, shl, scl, shc, scc, row0, seq):
    tm = x.shape[0]
    y = x * lax.rsqrt(jnp.mean(x * x, axis=-1, keepdims=True) + RMS_EPS) * g
    row = row0 + lax.broadcasted_iota(jnp.int32, (tm, 1), 0)
    is_ctx = row >= seq
    scale = jnp.where(is_ctx, scc, scl)
    shift = jnp.where(is_ctx, shc, shl)
    return y * (1.0 + scale) + shift


def _adaln_kernel(x_ref, g_ref, shl_ref, scl_ref, shc_ref, scc_ref, o_ref, *, tm, seq):
    h = _adaln_tile(x_ref[...], g_ref[...], shl_ref[...], scl_ref[...], shc_ref[...], scc_ref[...],
                    pl.program_id(1) * tm, seq)
    o_ref[...] = h.astype(o_ref.dtype)


def _mod_specs(d):
    return [
        pl.BlockSpec((1, d), lambda b, i: (0, 0)),
        pl.BlockSpec((None, 1, d), lambda b, i: (b, 0, 0)),
        pl.BlockSpec((None, 1, d), lambda b, i: (b, 0, 0)),
        pl.BlockSpec((1, d), lambda b, i: (0, 0)),
        pl.BlockSpec((1, d), lambda b, i: (0, 0)),
    ]


def _adaln(xu, g, shl, scl, shc, scc, seq):
    bsz, t, d = xu.shape
    tm = _pick(t, (768, 384, 256, 128))
    return pl.pallas_call(
        functools.partial(_adaln_kernel, tm=tm, seq=seq),
        grid=(bsz, t // tm),
        in_specs=[pl.BlockSpec((None, tm, d), lambda b, i: (b, i, 0))] + _mod_specs(d),
        out_specs=pl.BlockSpec((None, tm, d), lambda b, i: (b, i, 0)),
        out_shape=jax.ShapeDtypeStruct((bsz, t, d), BF16),
        compiler_params=_cparams(("parallel", "parallel")),
        name="adaln",
    )(xu, g, shl, scl, shc, scc)


def _adaln_router_kernel(x_ref, g_ref, shl_ref, scl_ref, shc_ref, scc_ref, wr_ref, br_ref,
                         h_ref, idx_ref, wt_ref, *, tm, seq):
    h = _adaln_tile(x_ref[...], g_ref[...], shl_ref[...], scl_ref[...], shc_ref[...], scc_ref[...],
                    pl.program_id(1) * tm, seq)
    h_ref[...] = h.astype(h_ref.dtype)
    logits = _dot3(h, wr_ref[...]) + br_ref[...]
    lane = lax.broadcasted_iota(jnp.int32, logits.shape, 1).astype(F32)
    vals, idxs = [], []
    cur = logits
    for _ in range(TOP_K):
        m = jnp.max(cur, axis=-1, keepdims=True)
        am = jnp.min(jnp.where(cur == m, lane, float(LANES)), axis=-1, keepdims=True)
        vals.append(m)
        idxs.append(am)
        cur = jnp.where(lane == am, -jnp.inf, cur)
    es = [jnp.exp(v - vals[0]) for v in vals]
    tot = es[0] + es[1] + es[2] + es[3]
    wt = jnp.zeros(logits.shape, F32)
    ix = jnp.zeros(logits.shape, F32)
    for k in range(TOP_K):
        wt = jnp.where(lane == k, es[k] / tot, wt)
        ix = jnp.where(lane == k, idxs[k], ix)
    idx_ref[...] = ix.astype(jnp.int32)
    wt_ref[...] = wt


def _adaln_router(xu, g, shl, scl, shc, scc, w_router, b_router, seq, rows):
    bsz, t, d = xu.shape
    tm = _pick(rows, (768, 512, 384, 256, 128))
    wr = jnp.zeros((d, LANES), F32).at[:, :N_EXPERTS].set(w_router)
    br = jnp.full((1, LANES), -1e30, F32).at[0, :N_EXPERTS].set(b_router)
    return pl.pallas_call(
        functools.partial(_adaln_router_kernel, tm=tm, seq=seq),
        grid=(bsz, rows // tm),
        in_specs=[pl.BlockSpec((None, tm, d), lambda b, i: (b, i, 0))] + _mod_specs(d) + [
            pl.BlockSpec((d, LANES), lambda b, i: (0, 0)),
            pl.BlockSpec((1, LANES), lambda b, i: (0, 0)),
        ],
        out_specs=[
            pl.BlockSpec((None, tm, d), lambda b, i: (b, i, 0)),
            pl.BlockSpec((None, tm, LANES), lambda b, i: (b, i, 0)),
            pl.BlockSpec((None, tm, LANES), lambda b, i: (b, i, 0)),
        ],
        out_shape=[
            jax.ShapeDtypeStruct((bsz, rows, d), BF16),
            jax.ShapeDtypeStruct((bsz, rows, LANES), jnp.int32),
            jax.ShapeDtypeStruct((bsz, rows, LANES), F32),
        ],
        compiler_params=_cparams(("parallel", "parallel")),
        name="adaln_router",
    )(xu, g, shl, scl, shc, scc, wr, br)


def _inproj_kernel(h_ref, w_ref, cos_ref, sin_ref, o_ref, *, tn, n_rope_tiles, n_q_tiles):
    j = pl.program_id(1)
    acc = jnp.dot(h_ref[...], w_ref[...], preferred_element_type=F32)

    @pl.when(j >= n_rope_tiles)
    def _():
        o_ref[...] = acc.astype(o_ref.dtype)

    @pl.when(j < n_rope_tiles)
    def _():
        scale = jnp.where(j < n_q_tiles, DA_HD ** -0.5, 1.0).astype(F32)
        cos = cos_ref[...] * scale
        sin = sin_ref[...] * scale
        lane = lax.broadcasted_iota(jnp.int32, cos.shape, 1)
        first = (lane % DA_HD) < (DA_HD // 2)
        for c in range(tn // LANES):
            a = acc[:, c * LANES:(c + 1) * LANES]
            sw = jnp.where(first, pltpu.roll(a, LANES - DA_HD // 2, 1), pltpu.roll(a, DA_HD // 2, 1))
            o_ref[:, c * LANES:(c + 1) * LANES] = (a * cos + sw * sin).astype(o_ref.dtype)


def _in_proj_main(h, w_main, cos_t, sin_t):
    bsz, t, d = h.shape
    n = w_main.shape[1]
    tn = 512
    return pl.pallas_call(
        functools.partial(_inproj_kernel, tn=tn, n_rope_tiles=OFF_DA_V // tn, n_q_tiles=OFF_DA_K // tn),
        grid=(bsz, n // tn),
        in_specs=[
            pl.BlockSpec((None, t, d), lambda b, j: (b, 0, 0)),
            pl.BlockSpec((d, tn), lambda b, j: (0, j)),
            pl.BlockSpec((t, LANES), lambda b, j: (0, 0)),
            pl.BlockSpec((t, LANES), lambda b, j: (0, 0)),
        ],
        out_specs=pl.BlockSpec((None, t, tn), lambda b, j: (b, 0, j)),
        out_shape=jax.ShapeDtypeStruct((bsz, t, n), BF16),
        compiler_params=_cparams(("parallel", "arbitrary")),
        name="in_proj",
    )(h, w_main, cos_t, sin_t)


def _mm_kernel(x_ref, w_ref, o_ref):
    o_ref[...] = jnp.dot(x_ref[...], w_ref[...], preferred_element_type=F32).astype(o_ref.dtype)


def _in_proj_small(h, w_small):
    bsz, t, d = h.shape
    return pl.pallas_call(
        _mm_kernel,
        grid=(bsz,),
        in_specs=[
            pl.BlockSpec((None, t, d), lambda b: (b, 0, 0)),
            pl.BlockSpec((d, LANES), lambda b: (0, 0)),
        ],
        out_specs=pl.BlockSpec((None, t, LANES), lambda b: (b, 0, 0)),
        out_shape=jax.ShapeDtypeStruct((bsz, t, LANES), F32),
        compiler_params=_cparams(("parallel",)),
        name="in_proj_small",
    )(h, w_small)


def _rope_tables(seq, t):
    rows = seq // GRID_W
    row = jnp.repeat(jnp.arange(rows, dtype=F32), GRID_W)
    col = jnp.tile(jnp.arange(GRID_W, dtype=F32), rows)
    inv = ROPE_THETA ** (-jnp.arange(ROPE_PAIRS_AXIS, dtype=F32) / ROPE_PAIRS_AXIS)
    ang = jnp.concatenate([row[:, None] * inv, col[:, None] * inv], axis=-1)
    cos, sin = jnp.cos(ang), jnp.sin(ang)
    cos_t = jnp.tile(cos, (1, LANES // (DA_HD // 2)))
    sin_t = jnp.tile(jnp.concatenate([-sin, sin], axis=-1), (1, LANES // DA_HD))
    pad = t - seq
    cos_t = jnp.concatenate([cos_t, jnp.ones((pad, LANES), F32)], axis=0)
    sin_t = jnp.concatenate([sin_t, jnp.zeros((pad, LANES), F32)], axis=0)
    return cos_t, sin_t


ATTN_ROW_GROUPS = 4
ATTN_HEADS_PER_STEP = 2


def _attn_kernel(lam_ref, g_ref, q_ref, k_ref, v_ref, o_ref, *, seq, tq, lam_init):
    qi = pl.program_id(2)
    nh = ATTN_HEADS_PER_STEP
    lp = lam_ref[...]
    l1 = jnp.sum(lp[0:1] * lp[1:2], axis=-1, keepdims=True)
    l2 = jnp.sum(lp[2:3] * lp[3:4], axis=-1, keepdims=True)
    lam = jnp.exp(l1) - jnp.exp(l2) + lam_init
    lane = lax.broadcasted_iota(jnp.int32, (tq, LANES), 1)
    qqs = []
    for h in range(nh):
        q = q_ref[:, h * LANES:(h + 1) * LANES].astype(F32)
        qqs.append(jnp.concatenate([jnp.where(lane < DA_HD, q, 0.0), jnp.where(lane >= DA_HD, q, 0.0)],
                                   axis=0).astype(BF16))

    def core(k_of, v_of):
        rs = 2 * tq // ATTN_ROW_GROUPS
        scores = [[lax.dot_general(qqs[h][i * rs:(i + 1) * rs], k_of(h), (((1,), (1,)), ((), ())),
                                   preferred_element_type=F32) for i in range(ATTN_ROW_GROUPS)] for h in range(nh)]
        for h in range(nh):
            outs = []
            for s in scores[h]:
                m = jnp.max(s, axis=-1, keepdims=True)
                p = jnp.exp(s - m)
                den = jnp.sum(p, axis=-1, keepdims=True)
                outs.append(jnp.dot(p.astype(BF16), v_of(h), preferred_element_type=F32) / den)
            o = jnp.concatenate(outs, axis=0)
            o = o[:tq] - lam * o[tq:]
            y = o * lax.rsqrt(jnp.mean(o * o, axis=-1, keepdims=True) + RMS_EPS) * g_ref[...]
            o_ref[:, h * LANES:(h + 1) * LANES] = (y * (1.0 - lam_init)).astype(o_ref.dtype)

    @pl.when(qi * tq < seq)
    def _():
        core(lambda h: k_ref[:, h * LANES:(h + 1) * LANES], lambda h: v_ref[:, h * LANES:(h + 1) * LANES])

    @pl.when(qi * tq >= seq)
    def _():
        core(lambda h: k_ref[seq:, h * LANES:(h + 1) * LANES], lambda h: v_ref[seq:, h * LANES:(h + 1) * LANES])


def _diff_attention(p, lam_params, subln_g, seq, rows, lam_init):
    bsz, t, _ = p.shape
    tq = _pick(math.gcd(seq, t - seq), (256, 128))
    nh = ATTN_HEADS_PER_STEP
    w = nh * LANES
    cq, ck, cv = OFF_DA_Q // w, OFF_DA_K // w, OFF_DA_V // w
    return pl.pallas_call(
        functools.partial(_attn_kernel, seq=seq, tq=tq, lam_init=lam_init),
        grid=(bsz, DA_HEADS // nh, rows // tq),
        in_specs=[
            pl.BlockSpec((4, DA_HD), lambda b, h, i: (0, 0)),
            pl.BlockSpec((1, 2 * DA_HD), lambda b, h, i: (0, 0)),
            pl.BlockSpec((None, tq, w), lambda b, h, i: (b, i, cq + h)),
            pl.BlockSpec((None, t, w), lambda b, h, i: (b, 0, ck + h)),
            pl.BlockSpec((None, t, w), lambda b, h, i: (b, 0, cv + h)),
        ],
        out_specs=pl.BlockSpec((None, tq, w), lambda b, h, i: (b, i, h)),
        out_shape=jax.ShapeDtypeStruct((bsz, rows, BRANCH_W), BF16),
        compiler_params=_cparams(("parallel", "parallel", "arbitrary")),
        name="diff_attention",
    )(lam_params, subln_g.reshape(1, -1), p, p, p)


def _gmlp_kernel(u_ref, v_ref, lng_ref, lnb_ref, ws_ref, bs_ref, o_ref, *, nchunks):
    for c in range(nchunks):
        r0 = c * GM_CHUNK
        u = _gelu_tanh(u_ref[r0:r0 + GM_CHUNK, :].astype(F32))
        v = _gelu_tanh(v_ref[r0:r0 + GM_CHUNK, :].astype(F32))
        xc = v - jnp.mean(v, axis=-1, keepdims=True)
        var = jnp.mean(xc * xc, axis=-1, keepdims=True)
        vn = (xc * lax.rsqrt(var + RMS_EPS) * lng_ref[...] + lnb_ref[...]).astype(BF16)
        for g in range(GM_GROUPS):
            cs = slice(g * GM_GW, (g + 1) * GM_GW)
            s = jnp.dot(ws_ref[g], vn[:, cs], preferred_element_type=F32) + bs_ref[g]
            o_ref[r0:r0 + GM_CHUNK, cs] = (u[:, cs] * s).astype(o_ref.dtype)


def _spatial_gating(p, ln_g, ln_b, ws, bs, rows):
    bsz, t, _ = p.shape
    tm = _pick(rows, (768, 512, 384, 256, 128))
    cu, cv = OFF_GM_U // BRANCH_W, OFF_GM_V // BRANCH_W
    bs_b = jnp.broadcast_to(bs[:, :, None], (GM_GROUPS, GM_CHUNK, GM_GW)).astype(F32)
    return pl.pallas_call(
        functools.partial(_gmlp_kernel, nchunks=tm // GM_CHUNK),
        grid=(bsz, rows // tm),
        in_specs=[
            pl.BlockSpec((None, tm, BRANCH_W), lambda b, i: (b, i, cu)),
            pl.BlockSpec((None, tm, BRANCH_W), lambda b, i: (b, i, cv)),
            pl.BlockSpec((1, BRANCH_W), lambda b, i: (0, 0)),
            pl.BlockSpec((1, BRANCH_W), lambda b, i: (0, 0)),
            pl.BlockSpec((GM_GROUPS, GM_CHUNK, GM_CHUNK), lambda b, i: (0, 0, 0)),
            pl.BlockSpec((GM_GROUPS, GM_CHUNK, GM_GW), lambda b, i: (0, 0, 0)),
        ],
        out_specs=pl.BlockSpec((None, tm, BRANCH_W), lambda b, i: (b, i, 0)),
        out_shape=jax.ShapeDtypeStruct((bsz, rows, BRANCH_W), BF16),
        compiler_params=_cparams(("parallel", "parallel")),
        name="spatial_gating",
    )(p, p, ln_g.reshape(1, -1), ln_b.reshape(1, -1), ws.astype(BF16), bs_b)


DN_BASE = 8
DN_PREP_GROUPS = (4, 6, 3, 2, 1)
DN_HEAD_GROUP = 2


def _dn_kernel(alog_ref, dtb_ref, q_ref, k_ref, v_ref, z_ref, sm_ref, ar_ref, cw_ref, ng_ref, o_ref,
               qn_ref, kn_ref, vn_ref, rowg_ref, ac_ref, b_ref, d_ref, cd_ref, oacc_ref, st_ref,
               *, seq, t):
    hg = DN_HEAD_GROUP
    hblk = pl.program_id(1)
    nc = t // DN_CHUNK
    n_lat = seq // DN_CHUNK
    n_ctx = nc - n_lat
    hw = DN_HD
    cw = DN_CHUNK

    row = lax.broadcasted_iota(jnp.int32, (t, 1), 0)
    seg_lo = jnp.where(row < seq, 0, seq)
    seg_hi = jnp.where(row < seq, seq, t)

    def conv_silu(x_ref, w):
        x = x_ref[...].astype(F32)
        acc = x * w[DN_CONV // 2:DN_CONV // 2 + 1, :]
        for s in (-2, -1, 1, 2):
            xs = pltpu.roll(x, (-s) % t, 0)
            rs = row + s
            ok = (rs >= seg_lo) & (rs < seg_hi)
            acc = acc + jnp.where(ok, xs, 0.0) * w[DN_CONV // 2 + s:DN_CONV // 2 + s + 1, :]
        return _silu(acc)

    def l2n(x):
        return x * lax.rsqrt(jnp.sum(x * x, axis=-1, keepdims=True) + RMS_EPS)

    qc = conv_silu(q_ref, cw_ref[0])
    kc = conv_silu(k_ref, cw_ref[1])
    vn_ref[...] = conv_silu(v_ref, cw_ref[2])
    for j in range(hg):
        cs = slice(j * hw, (j + 1) * hw)
        qn_ref[:, cs] = l2n(qc[:, cs]) * (DN_HD ** -0.5)
        kn_ref[:, cs] = l2n(kc[:, cs])

    def softplus(x):
        return jnp.maximum(x, 0.0) + jnp.log1p(jnp.exp(-jnp.abs(x)))

    rw = 2 * hg * cw
    lane_r = lax.broadcasted_iota(jnp.int32, (1, rw), 1)
    chain_r = lane_r // cw
    pos_r = lane_r % cw
    alog_r = jnp.zeros((1, rw), F32)
    dt_r = jnp.zeros((1, rw), F32)
    for d in range(2):
        for j in range(hg):
            alog_r = jnp.where(chain_r == hg * d + j, alog_ref[d, hblk * hg + j], alog_r)
            dt_r = jnp.where(chain_r == hg * d + j, dtb_ref[d, hblk * hg + j], dt_r)
    g_all = -jnp.exp(alog_r) * softplus(ar_ref[...].reshape(nc * 8, rw) + dt_r)
    pre = g_all
    suf = g_all
    sh = 1
    while sh < cw:
        pre = pre + jnp.where(pos_r >= sh, pltpu.roll(pre, sh, 1), 0.0)
        suf = suf + jnp.where(pos_r < cw - sh, pltpu.roll(suf, rw - sh, 1), 0.0)
        sh *= 2
    run = jnp.where(lane_r >= hg * cw, suf, pre).reshape(nc, 8, rw)
    tot = (pre + suf - g_all).reshape(nc, 8, rw)
    sub = lax.broadcasted_iota(jnp.int32, (nc, 8, rw), 1)
    both = jnp.where(sub == 0, run, tot)
    for d in range(2):
        rowg_ref[d] = both[:, :, d * hg * cw:(d + 1) * hg * cw]

    st_ref[...] = jnp.zeros(st_ref.shape, F32)

    pshape = (cw, hg * cw)
    ii = lax.broadcasted_iota(jnp.int32, pshape, 0)
    lp = lax.broadcasted_iota(jnp.int32, pshape, 1)
    jl = lp % cw
    left = lp < cw
    diag = ii == jl
    eye_p = jnp.where(diag, 1.0, 0.0).astype(F32)
    blk_base = (ii // DN_BASE) == (jl // DN_BASE)
    incl = [ii >= jl, ii <= jl]
    strict = [ii > jl, ii < jl]
    half = [jnp.where(left, 1.0, 0.0).astype(BF16), jnp.where(left, 0.0, 1.0).astype(BF16)]
    left_sq = lax.broadcasted_iota(jnp.int32, (LANES, LANES), 1) < cw

    def blockdiag(b16):
        return jnp.concatenate([b16 * half[0], b16 * half[1]], axis=0)

    def pprod(a, b):
        return jnp.dot(a.astype(BF16), blockdiag(b.astype(BF16)), preferred_element_type=F32)

    def tri_inverse(lmats):
        ms = [jnp.where(blk_base, -l, 0.0) for l in lmats]
        xs = [eye_p + m for m in ms]
        pws = [pprod(m, m) for m in ms]
        span = 4
        while span <= DN_BASE:
            tts = [pprod(jnp.concatenate([x, pw], axis=0), pw) for x, pw in zip(xs, pws)]
            xs = [x + tt[:cw] for x, tt in zip(xs, tts)]
            pws = [tt[cw:] for tt in tts]
            span *= 2
        bs = DN_BASE
        while bs < cw:
            off = ((ii // (2 * bs)) == (jl // (2 * bs))) & ((ii // bs) != (jl // bs))
            cmats = [jnp.where(off, l, 0.0) for l in lmats]
            ys = [pprod(x, c) for x, c in zip(xs, cmats)]
            zs = [pprod(y, x) for y, x in zip(ys, xs)]
            xs = [x - z for x, z in zip(xs, zs)]
            bs *= 2
        return xs

    def prep_load(c):
        rows = pl.ds(pl.multiple_of(c * cw, cw), cw)
        kk = [kn_ref[rows, j * hw:(j + 1) * hw] for j in range(hg)]
        qq = [qn_ref[rows, j * hw:(j + 1) * hw] for j in range(hg)]
        vv = [vn_ref[rows, j * hw:(j + 1) * hw] for j in range(hg)]
        return kk, qq, vv, sm_ref[rows, :], [rowg_ref[d, c] for d in range(2)]

    def prep_compute(loaded):
        n = len(loaded)
        gram, qk, ktp = [], [], []
        for kk, qq, vv, sm, rgs in loaded:
            gq, kt = [], []
            for j in range(hg):
                kb = kk[j].astype(BF16)
                gq.append(lax.dot_general(jnp.concatenate([kb, qq[j].astype(BF16)], axis=0),
                                          jnp.concatenate([kb, kb], axis=0), (((1,), (1,)), ((), ())),
                                          preferred_element_type=F32))
                kt.append(jnp.concatenate([kk[j], kk[j]], axis=0).T)
            pair = jnp.where(left_sq, gq[0], gq[1])
            gram.append(pair[:cw])
            qk.append(pair[cw:])
            ktp.append(jnp.where(left_sq, kt[0], kt[1]))
        pre = []
        for ci, (kk, qq, vv, sm, rgs) in enumerate(loaded):
            lane_c = lax.broadcasted_iota(jnp.int32, sm.shape, 1)
            for d in range(2):
                bcol = [_sigmoid(jnp.sum(jnp.where(lane_c == d * DN_HEADS + hblk * hg + j, sm, 0.0),
                                         axis=1, keepdims=True)) for j in range(hg)]
                gc_row, g_tot = rgs[d][0:1, :], rgs[d][1:2, :]
                gdiag = jnp.where(diag, gc_row, 0.0)
                gcol = [jnp.sum(jnp.where(left, gdiag, 0.0), axis=1, keepdims=True),
                        jnp.sum(jnp.where(left, 0.0, gdiag), axis=1, keepdims=True)]
                gc = jnp.where(left, gcol[0], gcol[1])
                beta = jnp.where(left, bcol[0], bcol[1])
                dec = jnp.exp(jnp.where(incl[d], gc - gc_row, -jnp.inf))
                lmat = jnp.where(strict[d], beta * gram[ci] * dec, 0.0)
                pre.append((ci, d, bcol, gcol, gc_row, g_tot, dec, lmat))
        tinvs = tri_inverse([p[-1] for p in pre])
        rhss, egs = [], []
        for ci, d, bcol, gcol, gc_row, g_tot, dec, lmat in pre:
            kk, qq, vv = loaded[ci][:3]
            eg = [jnp.exp(gcol[j]) for j in range(hg)]
            egs.append(eg)
            rhss.append(jnp.concatenate(
                [jnp.concatenate([vv[j] * bcol[j], kk[j] * (bcol[j] * eg[j])], axis=1) for j in range(hg)],
                axis=0).astype(BF16))
        sols = []
        for tinv, rhs in zip(tinvs, rhss):
            t16 = tinv.astype(BF16)
            sols.append(jnp.dot(jnp.concatenate([t16 * half[0], t16 * half[1]], axis=0), rhs,
                                preferred_element_type=F32))
        xs = []
        for (ci, d, bcol, gcol, gc_row, g_tot, dec, lmat), sol in zip(pre, sols):
            q_intra = jnp.where(incl[d], qk[ci] * dec, 0.0)
            qk2 = jnp.concatenate([q_intra, ktp[ci] * jnp.exp(g_tot - gc_row)], axis=0).astype(BF16)
            s16 = sol.astype(BF16)
            zero = jnp.zeros((cw, 2 * hw), BF16)
            bd = jnp.concatenate([jnp.concatenate([s16[:cw], zero], axis=1),
                                  jnp.concatenate([zero, s16[cw:]], axis=1)], axis=0)
            xs.append(jnp.dot(qk2, bd, preferred_element_type=F32))
        outs = [[None, None] for _ in range(n)]
        for (ci, d, bcol, gcol, gc_row, g_tot, dec, lmat), x, eg in zip(pre, xs, egs):
            qq = loaded[ci][1]
            e_tot = jnp.exp(g_tot)
            outs[ci][d] = dict(
                ac=[jnp.concatenate([x[cw:, j * 2 * hw + hw:(j + 1) * 2 * hw],
                                     qq[j] * eg[j] - x[:cw, j * 2 * hw + hw:(j + 1) * 2 * hw]], axis=0).astype(BF16)
                    for j in range(hg)],
                b=[x[cw:, j * 2 * hw:j * 2 * hw + hw].astype(BF16) for j in range(hg)],
                dd=[x[:cw, j * 2 * hw:j * 2 * hw + hw].astype(BF16) for j in range(hg)],
                cd=[jnp.broadcast_to(e_tot[:, j * cw:j * cw + 1], (1, hw)) for j in range(hg)])
        return outs

    def prep_store(c, outs):
        rows = pl.ds(pl.multiple_of(c * cw, cw), cw)
        for d in range(2):
            for j in range(hg):
                ac_ref[hg * d + j, c] = outs[d]["ac"][j]
                b_ref[hg * d + j, c] = outs[d]["b"][j]
                d_ref[hg * d + j, rows, :] = outs[d]["dd"][j]
                cd_ref[hg * d + j, c] = outs[d]["cd"][j]

    group = _pick(nc, DN_PREP_GROUPS)

    def prep_body(g, carry):
        cs = [g * group + cc for cc in range(group)]
        loaded = [prep_load(c) for c in cs]
        outs = prep_compute(loaded)
        for c, o in zip(cs, outs):
            prep_store(c, o)
        return carry

    lax.fori_loop(0, nc // group, prep_body, 0)

    def scan_body(i, carry):
        cf = jnp.where(i < n_ctx, i + n_lat, i - n_ctx)
        cb = nc - 1 - i
        dirs = ((0, cf), (1, cb))
        rows = [pl.ds(pl.multiple_of(c * cw, cw), cw) for _, c in dirs]
        state = [st_ref[s] for s in range(2 * hg)]
        ac = [ac_ref[hg * d + j, c] for d, c in dirs for j in range(hg)]
        bb = [b_ref[hg * d + j, c] for d, c in dirs for j in range(hg)]
        dd = [d_ref[hg * d + j, rows[d], :] for d, _ in dirs for j in range(hg)]
        cd = [cd_ref[hg * d + j, c] for d, c in dirs for j in range(hg)]
        rs = [jnp.dot(ac[s], state[s].astype(BF16), preferred_element_type=F32) for s in range(2 * hg)]
        o_new = [jnp.concatenate([rs[hg * d + j][hw:] + dd[hg * d + j].astype(F32) for j in range(hg)], axis=1)
                 for d in range(2)]
        st_new = [state[s] * cd[s] - rs[s][:hw] + bb[s].astype(F32) for s in range(2 * hg)]
        for d, _ in dirs:
            oacc_ref[d, rows[d], :] = o_new[d]
        for s in range(2 * hg):
            st_ref[s] = st_new[s]
        return carry

    lax.fori_loop(0, nc, scan_body, 0)

    o = oacc_ref[0] + oacc_ref[1]
    for j in range(hg):
        cs = slice(j * hw, (j + 1) * hw)
        oj = o[:, cs]
        y = oj * lax.rsqrt(jnp.mean(oj * oj, axis=-1, keepdims=True) + RMS_EPS) * ng_ref[...]
        o_ref[:, cs] = (y * _silu(z_ref[:, cs].astype(F32))).astype(o_ref.dtype)


def _gated_deltanet(p, small, conv_w, a_log, dt_bias, norm_g, seq):
    bsz, t, _ = p.shape
    nc = t // DN_CHUNK
    hg = DN_HEAD_GROUP
    assert hg == 2
    w = hg * DN_HD
    rw = 2 * hg * DN_CHUNK
    a = small[..., 2 * DN_HEADS:4 * DN_HEADS].reshape(bsz, nc, DN_CHUNK, 2, DN_HEADS // hg, hg)
    a_row = jnp.transpose(a, (0, 4, 1, 3, 5, 2)).reshape(bsz, DN_HEADS // hg, nc, 1, rw)
    a_row = jnp.broadcast_to(a_row, (bsz, DN_HEADS // hg, nc, 8, rw))
    cq, ck, cv, cz = (OFF_DN_Q // w, OFF_DN_K // w, OFF_DN_V // w, OFF_DN_Z // w)
    slab = lambda c0: pl.BlockSpec((None, t, w), lambda b, h: (b, 0, c0 + h))
    smem = pl.BlockSpec(memory_space=pltpu.SMEM)
    return pl.pallas_call(
        functools.partial(_dn_kernel, seq=seq, t=t),
        grid=(bsz, DN_HEADS // hg),
        in_specs=[
            smem, smem,
            slab(cq), slab(ck), slab(cv), slab(cz),
            pl.BlockSpec((None, t, LANES), lambda b, h: (b, 0, 0)),
            pl.BlockSpec((None, None, nc, 8, rw), lambda b, h: (b, h, 0, 0, 0)),
            pl.BlockSpec((3, DN_CONV, w), lambda b, h: (0, 0, h)),
            pl.BlockSpec((1, DN_HD), lambda b, h: (0, 0)),
        ],
        out_specs=pl.BlockSpec((None, t, w), lambda b, h: (b, 0, h)),
        out_shape=jax.ShapeDtypeStruct((bsz, t, BRANCH_W), BF16),
        scratch_shapes=[
            pltpu.VMEM((t, w), F32), pltpu.VMEM((t, w), F32), pltpu.VMEM((t, w), F32),
            pltpu.VMEM((2, nc, 8, hg * DN_CHUNK), F32),
            pltpu.VMEM((2 * hg, nc, DN_HD + DN_CHUNK, DN_HD), BF16),
            pltpu.VMEM((2 * hg, nc, DN_HD, DN_HD), BF16),
            pltpu.VMEM((2 * hg, t, DN_HD), BF16),
            pltpu.VMEM((2 * hg, nc, 1, DN_HD), F32),
            pltpu.VMEM((2, t, w), F32),
            pltpu.VMEM((2 * hg, DN_HD, DN_HD), F32),
        ],
        compiler_params=_cparams(("parallel", "parallel")),
        name="gated_deltanet",
    )(a_log, dt_bias, p, p, p, p, small, a_row, conv_w, norm_g.reshape(1, -1))


def _merge_kernel(ya_ref, yg_ref, yd_ref, ga_ref, gg_ref, gd_ref, wb_ref, bg_ref, o_ref):
    acc = None
    for i, (y_ref, g_ref) in enumerate(((ya_ref, ga_ref), (yg_ref, gg_ref), (yd_ref, gd_ref))):
        gate = _sigmoid(g_ref[...].astype(F32) + bg_ref[i])
        term = gate * jnp.dot(y_ref[...], wb_ref[i], preferred_element_type=F32)
        acc = term if acc is None else acc + term
    o_ref[...] = acc.astype(o_ref.dtype)


def _merge(ya, yg, yd, p, w_branch, b_gate, rows):
    bsz = p.shape[0]
    d = D_MODEL
    tm = _pick(rows, (768, 512, 384, 256, 128))
    tn = 512
    g0 = OFF_GATE_MAIN // tn
    y_spec = pl.BlockSpec((None, tm, BRANCH_W), lambda b, i, j: (b, i, 0))
    gate_spec = lambda k: pl.BlockSpec((None, tm, tn), lambda b, i, j: (b, i, g0 + k * (d // tn) + j))
    return pl.pallas_call(
        _merge_kernel,
        grid=(bsz, rows // tm, d // tn),
        in_specs=[y_spec, y_spec, y_spec, gate_spec(0), gate_spec(1), gate_spec(2),
                  pl.BlockSpec((N_BRANCH, BRANCH_W, tn), lambda b, i, j: (0, 0, j)),
                  pl.BlockSpec((N_BRANCH, 1, tn), lambda b, i, j: (0, 0, j))],
        out_specs=pl.BlockSpec((None, tm, tn), lambda b, i, j: (b, i, j)),
        out_shape=jax.ShapeDtypeStruct((bsz, rows, d), BF16),
        compiler_params=_cparams(("parallel", "parallel", "arbitrary")),
        name="merge_branches",
    )(ya, yg, yd, p, p, p, w_branch, b_gate.reshape(N_BRANCH, 1, d))


def _outproj_kernel(z_ref, w_ref, x_ref, gl_ref, gc_ref, o_ref, *, tm, seq):
    acc = jnp.dot(z_ref[...], w_ref[...], preferred_element_type=F32)
    row = pl.program_id(1) * tm + lax.broadcasted_iota(jnp.int32, (tm, 1), 0)
    gate = jnp.where(row >= seq, gc_ref[...], gl_ref[...])
    o_ref[...] = x_ref[...] + gate * acc


def _out_proj_residual(z, w_out, xu, gate_l, gate_c, seq, rows):
    bsz, t, d = xu.shape
    tm = _pick(rows, (768, 512, 384, 256, 128))
    tn = 512
    return pl.pallas_call(
        functools.partial(_outproj_kernel, tm=tm, seq=seq),
        grid=(bsz, rows // tm, d // tn),
        in_specs=[
            pl.BlockSpec((None, tm, d), lambda b, i, j: (b, i, 0)),
            pl.BlockSpec((d, tn), lambda b, i, j: (0, j)),
            pl.BlockSpec((None, tm, tn), lambda b, i, j: (b, i, j)),
            pl.BlockSpec((None, 1, tn), lambda b, i, j: (b, 0, j)),
            pl.BlockSpec((1, tn), lambda b, i, j: (0, j)),
        ],
        out_specs=pl.BlockSpec((None, tm, tn), lambda b, i, j: (b, i, j)),
        out_shape=jax.ShapeDtypeStruct((bsz, rows, d), F32),
        compiler_params=_cparams(("parallel", "parallel", "arbitrary")),
        name="out_proj_residual",
    )(z, w_out, xu, gate_l, gate_c)


W1_BLOCK = 2 * LANES


def _w1_prep_kernel(w_ref, perm_ref, o_ref):
    w = w_ref[...].astype(BF16)
    for blk in range(w.shape[1] // W1_BLOCK):
        cs = slice(blk * W1_BLOCK, (blk + 1) * W1_BLOCK)
        o_ref[:, cs] = jnp.dot(w[:, cs], perm_ref[...], preferred_element_type=F32).astype(o_ref.dtype)


def _w1_prep(w_e1):
    nl, ne, d, n = w_e1.shape
    tk = 1024
    j = jnp.arange(W1_BLOCK)
    src = jnp.where(j < LANES, 2 * j, 2 * (j - LANES) + 1)
    perm = (jnp.arange(W1_BLOCK)[:, None] == src[None, :]).astype(BF16)
    return pl.pallas_call(
        _w1_prep_kernel,
        grid=(nl * ne, d // tk),
        in_specs=[pl.BlockSpec((None, tk, n), lambda e, k: (e, k, 0)),
                  pl.BlockSpec((W1_BLOCK, W1_BLOCK), lambda e, k: (0, 0))],
        out_specs=pl.BlockSpec((None, tk, n), lambda e, k: (e, k, 0)),
        out_shape=jax.ShapeDtypeStruct((nl * ne, d, n), BF16),
        compiler_params=_cparams(("parallel", "parallel")),
        name="expert_w1_prep",
    )(w_e1.reshape(nl * ne, d, n), perm)


def _regroup_bias(b_e1):
    ne, n = b_e1.shape
    return jnp.transpose(b_e1.reshape(ne, n // W1_BLOCK, LANES, 2), (0, 1, 3, 2)).reshape(ne, 1, n)


def _expert_kernel(be_ref, bv_ref, x_ref, w1_ref, b1_ref, w2_ref, b2_ref, o_ref, hid_ref):
    i = pl.program_id(0)

    @pl.when(bv_ref[i] > 0)
    def _():
        hgl = jnp.dot(x_ref[...], w1_ref[...], preferred_element_type=F32) + b1_ref[...]
        for blk in range(hgl.shape[1] // W1_BLOCK):
            xg = jnp.minimum(hgl[:, blk * W1_BLOCK:blk * W1_BLOCK + LANES], SWIGLU_LIMIT)
            xl = jnp.clip(hgl[:, blk * W1_BLOCK + LANES:(blk + 1) * W1_BLOCK], -SWIGLU_LIMIT, SWIGLU_LIMIT)
            hid_ref[:, blk * LANES:(blk + 1) * LANES] = (
                xg * _sigmoid(SWIGLU_ALPHA * xg) * (xl + 1.0)).astype(hid_ref.dtype)
        y = jnp.dot(hid_ref[...], w2_ref[...], preferred_element_type=F32) + b2_ref[...]
        o_ref[...] = y.astype(o_ref.dtype)


def _experts(xs, blk_e, blk_valid, w1, b1, w2, b2, e0):
    n_rows, d = xs.shape
    tm = MOE_TM
    ff = EXPERT_FF
    grid_spec = pltpu.PrefetchScalarGridSpec(
        num_scalar_prefetch=2,
        grid=(n_rows // tm,),
        in_specs=[
            pl.BlockSpec((tm, d), lambda i, be, bv: (i, 0)),
            pl.BlockSpec((None, d, 2 * ff), lambda i, be, bv: (e0 + be[i], 0, 0)),
            pl.BlockSpec((None, 1, 2 * ff), lambda i, be, bv: (be[i], 0, 0)),
            pl.BlockSpec((None, ff, d), lambda i, be, bv: (be[i], 0, 0)),
            pl.BlockSpec((None, 1, d), lambda i, be, bv: (be[i], 0, 0)),
        ],
        out_specs=pl.BlockSpec((tm, d), lambda i, be, bv: (i, 0)),
        scratch_shapes=[pltpu.VMEM((tm, ff), BF16)],
    )
    return pl.pallas_call(
        _expert_kernel,
        grid_spec=grid_spec,
        out_shape=jax.ShapeDtypeStruct((n_rows, d), BF16),
        compiler_params=_cparams(("arbitrary",)),
        name="moe_experts",
    )(blk_e, blk_valid, xs, w1, b1, w2, b2)


def _moe(h2, top_i, w1, b1, w2, b2, e0):
    n_tok, d = h2.shape
    tm = MOE_TM
    n_assign = n_tok * TOP_K
    flat_e = top_i.reshape(n_assign)
    order = jnp.argsort(flat_e).astype(jnp.int32)
    rank = jnp.argsort(order).astype(jnp.int32)
    onehot = flat_e[:, None] == jnp.arange(N_EXPERTS, dtype=flat_e.dtype)[None, :]
    counts = jnp.sum(onehot, axis=0, dtype=jnp.int32)
    padded = (counts + tm - 1) // tm * tm
    pad_end = jnp.cumsum(padded)
    start = jnp.cumsum(counts) - counts
    shift = (pad_end - padded) - start
    pos = rank + jnp.sum(jnp.where(onehot, shift[None, :], 0), axis=1)
    n_blocks = -(-n_assign // tm) + N_EXPERTS
    blk_start = jnp.arange(n_blocks, dtype=jnp.int32) * tm
    blk_valid = (blk_start < pad_end[-1]).astype(jnp.int32)
    blk_e = jnp.sum(blk_start[:, None] >= pad_end[None, :], axis=1, dtype=jnp.int32)
    last_e = jnp.sum(pad_end[-1] - 1 >= pad_end, dtype=jnp.int32)
    blk_e = jnp.where(blk_valid > 0, blk_e, last_e)
    off = blk_start - shift[blk_e]
    lo = start[blk_e]
    order_pad = jnp.concatenate([order, jnp.zeros((tm,), jnp.int32)])
    run = jax.vmap(lambda o: lax.dynamic_slice(order_pad, (o,), (tm,)))(jnp.clip(off, 0, n_assign))
    src = off[:, None] + jnp.arange(tm, dtype=jnp.int32)[None, :]
    live = (src >= lo[:, None]) & (src < (lo + counts[blk_e])[:, None]) & (blk_valid[:, None] > 0)
    row_tok = jnp.where(live, run // TOP_K, 0).reshape(n_blocks * tm)
    xs = h2[row_tok]
    y = _experts(xs, blk_e, blk_valid, w1, b1, w2, b2, e0)
    return y[pos.reshape(n_tok, TOP_K).T.reshape(n_assign)].reshape(TOP_K, n_tok, d)


def _combine_kernel(y_ref, w_ref, x_ref, gl_ref, gc_ref, *rest, tm, seq):
    o_ref = rest[-1]
    w = w_ref[...]
    acc = y_ref[0].astype(F32) * w[:, 0:1]
    for k in range(1, TOP_K):
        acc = acc + y_ref[k].astype(F32) * w[:, k:k + 1]
    row = pl.program_id(1) * tm + lax.broadcasted_iota(jnp.int32, (tm, 1), 0)
    gate = jnp.where(row >= seq, gc_ref[...], gl_ref[...])
    o_ref[...] = x_ref[...] + gate * acc


def _moe_combine(yk, top_w, xu, gate_l, gate_c, seq, b0, prev):
    bsz, rows, d = xu.shape
    bp = yk.shape[1]
    tm = _pick(rows, (512, 384, 256, 128))
    in_specs = [
        pl.BlockSpec((TOP_K, None, tm, d), lambda b, i: (0, b, i, 0)),
        pl.BlockSpec((None, tm, LANES), lambda b, i: (b + b0, i, 0)),
        pl.BlockSpec((None, tm, d), lambda b, i: (b + b0, i, 0)),
        pl.BlockSpec((None, 1, d), lambda b, i: (b + b0, 0, 0)),
        pl.BlockSpec((1, d), lambda b, i: (0, 0)),
    ]
    args = [yk, top_w, xu, gate_l, gate_c]
    aliases = {}
    if prev is not None:
        in_specs.append(pl.BlockSpec(memory_space=pl.ANY))
        args.append(prev)
        aliases = {len(args) - 1: 0}
    return pl.pallas_call(
        functools.partial(_combine_kernel, tm=tm, seq=seq),
        grid=(bp, rows // tm),
        in_specs=in_specs,
        out_specs=pl.BlockSpec((None, tm, d), lambda b, i: (b + b0, i, 0)),
        out_shape=jax.ShapeDtypeStruct((bsz, rows, d), F32),
        input_output_aliases=aliases,
        compiler_params=_cparams(("parallel", "parallel")),
        name="moe_combine",
    )(*args)


def _final_kernel(x_ref, g_ref, o_ref):
    x = x_ref[...]
    o_ref[...] = x * lax.rsqrt(jnp.mean(x * x, axis=-1, keepdims=True) + RMS_EPS) * g_ref[...]


def _final_norm(xu, g, seq):
    bsz, t, d = xu.shape
    tm = _pick(seq, (512, 256, 128))
    return pl.pallas_call(
        _final_kernel,
        grid=(bsz, seq // tm),
        in_specs=[pl.BlockSpec((None, tm, d), lambda b, i: (b, i, 0)),
                  pl.BlockSpec((1, d), lambda b, i: (0, 0))],
        out_specs=pl.BlockSpec((None, tm, d), lambda b, i: (b, i, 0)),
        out_shape=jax.ShapeDtypeStruct((bsz, seq, d), F32),
        compiler_params=_cparams(("parallel", "parallel")),
        name="final_norm",
    )(xu, g.reshape(1, d))


def _layer(xu, mod_l, mod_c, seq, layer_idx, ctx_out, cos_t, sin_t, norm1, w_in, da_lambda, da_subln,
           gm_ln_g, gm_ln_b, gm_ws, gm_bs, dn_conv, dn_a_log, dn_dt_bias, dn_norm, b_gate, w_branch,
           w_out, norm2, w_router, b_router, w1_all, b_e1, w_e2, b_e2):
    bsz, t, d = xu.shape
    rows = t if ctx_out else seq
    lam_init = 0.8 - 0.6 * math.exp(-0.3 * layer_idx)
    ml = [mod_l[:, k:k + 1, :] for k in range(6)]
    mc = [mod_c[k:k + 1, :] for k in range(6)]

    h = _adaln(xu, norm1.reshape(1, d), ml[0], ml[1], mc[0], mc[1], seq)
    w_main = jnp.concatenate([w_in[:, :OFF_SMALL], w_in[:, OFF_GATE:]], axis=1).astype(BF16)
    w_small = jnp.zeros((d, LANES), BF16).at[:, :OFF_GATE - OFF_SMALL].set(
        w_in[:, OFF_SMALL:OFF_GATE].astype(BF16))
    p = _in_proj_main(h, w_main, cos_t, sin_t)
    small = _in_proj_small(h, w_small)

    ya = _diff_attention(p, da_lambda, da_subln, seq, rows, lam_init)
    yg = _spatial_gating(p, gm_ln_g, gm_ln_b, gm_ws, gm_bs, rows)
    yd = _gated_deltanet(p, small, dn_conv, dn_a_log, dn_dt_bias, dn_norm, seq)
    z = _merge(ya, yg, yd, p, w_branch.astype(BF16), b_gate, rows)
    xu = _out_proj_residual(z, w_out.astype(BF16), xu, ml[2], mc[2], seq, rows)

    h2, top_i, top_w = _adaln_router(xu, norm2.reshape(1, d), ml[3], ml[4], mc[3], mc[4],
                                     w_router, b_router, seq, rows)
    parts = MOE_PARTS if bsz % MOE_PARTS == 0 else 1
    bp = bsz // parts
    b1p, w2p, b2p = _regroup_bias(b_e1), w_e2.astype(BF16), b_e2[:, None, :]
    yks = [_moe(h2[i * bp:(i + 1) * bp].reshape(bp * rows, d),
                top_i[i * bp:(i + 1) * bp].reshape(bp * rows, LANES)[:, :TOP_K],
                w1_all, b1p, w2p, b2p, layer_idx * N_EXPERTS).reshape(TOP_K, bp, rows, d) for i in range(parts)]
    out = None
    for i in range(parts):
        out = _moe_combine(yks[i], top_w, xu, ml[5], mc[5], seq, i * bp, out)
    return out


def kernel(x, c, ctx, c_ctx, w_mod, b_mod, norm1, w_in, da_lambda, da_subln, gm_ln_g, gm_ln_b, gm_ws, gm_bs,
           dn_conv, dn_a_log, dn_dt_bias, dn_norm, b_gate, w_branch, w_out, norm2, w_router, b_router,
           w_e1, b_e1, w_e2, b_e2, norm_f):
    bsz, seq, d = x.shape
    n_ctx = ctx.shape[1]
    t = seq + n_ctx
    depth = w_mod.shape[0]
    xu = jnp.concatenate([x, ctx], axis=1)
    r = -(-(bsz + 1) // 8) * 8
    cond = jnp.zeros((r, d), F32).at[:bsz].set(c).at[bsz].set(c_ctx)
    mod = _modulation(cond, w_mod, b_mod).reshape(depth, r, 6, d)
    cos_t, sin_t = _rope_tables(seq, t)
    w1_all = _w1_prep(w_e1)
    for l in range(depth):
        xu = _layer(xu, mod[l, :bsz], mod[l, bsz], seq, l, l < depth - 1, cos_t, sin_t, norm1[l], w_in[l],
                    da_lambda[l], da_subln[l], gm_ln_g[l], gm_ln_b[l], gm_ws[l], gm_bs[l], dn_conv[l],
                    dn_a_log[l], dn_dt_bias[l], dn_norm[l], b_gate[l], w_branch[l], w_out[l], norm2[l],
                    w_router[l], b_router[l], w1_all, b_e1[l], w_e2[l], b_e2[l])
    return _final_norm(xu, norm_f, seq)
```

```python
import functools
import math

import jax
import jax.numpy as jnp
from jax import lax
from jax.experimental import pallas as pl
from jax.experimental.pallas import tpu as pltpu

F32 = jnp.float32
BF16 = jnp.bfloat16

D_MODEL = 2048
GRID_W = 64
RMS_EPS = 1e-6
BRANCH_W = D_MODEL // 2
N_BRANCH = 3
DA_HD = 64
DA_HEADS = BRANCH_W // (2 * DA_HD)
ROPE_THETA = 10000.0
ROPE_PAIRS_AXIS = DA_HD // 4
GM_CHUNK = 128
GM_GW = 128
GM_GROUPS = BRANCH_W // GM_GW
DN_HD = 128
DN_HEADS = BRANCH_W // DN_HD
DN_CHUNK = 64
DN_CONV = 5
N_EXPERTS = 32
TOP_K = 4
EXPERT_FF = D_MODEL // 2
SWIGLU_LIMIT = 7.0
SWIGLU_ALPHA = 1.702

LANES = 128
VMEM_LIMIT = 56 * 1024 * 1024

OFF_DA_Q = 0
OFF_DA_K = 1024
OFF_DA_V = 2048
OFF_GM_U = 3072
OFF_GM_V = 4096
OFF_DN_Q = 5120
OFF_DN_K = 6144
OFF_DN_V = 7168
OFF_DN_Z = 8192
OFF_SMALL = 9216
OFF_GATE = 9248
N_MAIN = 9216 + N_BRANCH * D_MODEL
OFF_GATE_MAIN = 9216

MOE_TM = 512
MOE_PARTS = 1


def _cparams(sem):
    return pltpu.CompilerParams(dimension_semantics=sem, vmem_limit_bytes=VMEM_LIMIT)


def _pick(n, cands):
    for c in cands:
        if n % c == 0:
            return c
    raise ValueError(f"no tile for {n} in {cands}")


def _sigmoid(x):
    return jax.nn.sigmoid(x)


def _silu(x):
    return x * _sigmoid(x)


def _gelu_tanh(x):
    return x * (0.5 * (1.0 + jnp.tanh(0.7978845608028654 * (x + 0.044715 * (x * x * x)))))


def _bdot(a, b):
    return jnp.dot(a.astype(BF16), b.astype(BF16), preferred_element_type=F32)


def _split(a):
    hi = a.astype(BF16)
    lo = (a - hi.astype(F32)).astype(BF16)
    return hi, lo


def _dot3(a, b):
    ah, al = _split(a)
    bh, bl = _split(b)
    return (jnp.dot(ah, bh, preferred_element_type=F32)
            + (jnp.dot(al, bh, preferred_element_type=F32)
               + jnp.dot(ah, bl, preferred_element_type=F32)))


def _mod_kernel(c_ref, w_ref, b_ref, o_ref):
    s = _silu(c_ref[...])
    o_ref[...] = _bdot(s, w_ref[...]) + b_ref[...]


def _modulation(cond, w_mod, b_mod):
    nl, d, n6 = w_mod.shape
    r = cond.shape[0]
    tn = 1024
    return pl.pallas_call(
        _mod_kernel,
        grid=(nl, n6 // tn),
        in_specs=[
            pl.BlockSpec((r, d), lambda l, j: (0, 0)),
            pl.BlockSpec((None, d, tn), lambda l, j: (l, 0, j)),
            pl.BlockSpec((None, 1, tn), lambda l, j: (l, 0, j)),
        ],
        out_specs=pl.BlockSpec((None, r, tn), lambda l, j: (l, 0, j)),
        out_shape=jax.ShapeDtypeStruct((nl, r, n6), F32),
        compiler_params=_cparams(("arbitrary", "arbitrary")),
        name="modulation",
    )(cond, w_mod, b_mod.reshape(nl, 1, n6))


def _adaln_tile(x, g, shl, scl, shc, scc, row0, seq):
    tm = x.shape[0]
    y = x * lax.rsqrt(jnp.mean(x * x, axis=-1, keepdims=True) + RMS_EPS) * g
    row = row0 + lax.broadcasted_iota(jnp.int32, (tm, 1), 0)
    is_ctx = row >= seq
    scale = jnp.where(is_ctx, scc, scl)
    shift = jnp.where(is_ctx, shc, shl)
    return y * (1.0 + scale) + shift


def _mod_specs(d):
    return [
        pl.BlockSpec((1, d), lambda b, i: (0, 0)),
        pl.BlockSpec((None, 1, d), lambda b, i: (b, 0, 0)),
        pl.BlockSpec((None, 1, d), lambda b, i: (b, 0, 0)),
        pl.BlockSpec((1, d), lambda b, i: (0, 0)),
        pl.BlockSpec((1, d), lambda b, i: (0, 0)),
    ]


def _adaln_router_kernel(x_ref, g_ref, shl_ref, scl_ref, shc_ref, scc_ref, wr_ref, br_ref,
                         h_ref, idx_ref, wt_ref, *, tm, seq):
    h = _adaln_tile(x_ref[...], g_ref[...], shl_ref[...], scl_ref[...], shc_ref[...], scc_ref[...],
                    pl.program_id(1) * tm, seq)
    h_ref[...] = h.astype(h_ref.dtype)
    logits = _dot3(h, wr_ref[...]) + br_ref[...]
    lane = lax.broadcasted_iota(jnp.int32, logits.shape, 1).astype(F32)
    vals, idxs = [], []
    cur = logits
    for _ in range(TOP_K):
        m = jnp.max(cur, axis=-1, keepdims=True)
        am = jnp.min(jnp.where(cur == m, lane, float(LANES)), axis=-1, keepdims=True)
        vals.append(m)
        idxs.append(am)
        cur = jnp.where(lane == am, -jnp.inf, cur)
    es = [jnp.exp(v - vals[0]) for v in vals]
    tot = es[0] + es[1] + es[2] + es[3]
    wt = jnp.zeros(logits.shape, F32)
    ix = jnp.zeros(logits.shape, F32)
    for k in range(TOP_K):
        wt = jnp.where(lane == k, es[k] / tot, wt)
        ix = jnp.where(lane == k, idxs[k], ix)
    idx_ref[...] = ix.astype(jnp.int32)
    wt_ref[...] = wt


def _adaln_router(xu, g, shl, scl, shc, scc, w_router, b_router, seq, rows):
    bsz, t, d = xu.shape
    tm = _pick(rows, (768, 512, 384, 256, 128))
    wr = jnp.zeros((d, LANES), F32).at[:, :N_EXPERTS].set(w_router)
    br = jnp.full((1, LANES), -1e30, F32).at[0, :N_EXPERTS].set(b_router)
    return pl.pallas_call(
        functools.partial(_adaln_router_kernel, tm=tm, seq=seq),
        grid=(bsz, rows // tm),
        in_specs=[pl.BlockSpec((None, tm, d), lambda b, i: (b, i, 0))] + _mod_specs(d) + [
            pl.BlockSpec((d, LANES), lambda b, i: (0, 0)),
            pl.BlockSpec((1, LANES), lambda b, i: (0, 0)),
        ],
        out_specs=[
            pl.BlockSpec((None, tm, d), lambda b, i: (b, i, 0)),
            pl.BlockSpec((None, tm, LANES), lambda b, i: (b, i, 0)),
            pl.BlockSpec((None, tm, LANES), lambda b, i: (b, i, 0)),
        ],
        out_shape=[
            jax.ShapeDtypeStruct((bsz, rows, d), BF16),
            jax.ShapeDtypeStruct((bsz, rows, LANES), jnp.int32),
            jax.ShapeDtypeStruct((bsz, rows, LANES), F32),
        ],
        compiler_params=_cparams(("parallel", "parallel")),
        name="adaln_router",
    )(xu, g, shl, scl, shc, scc, wr, br)


def _inproj_kernel(x_ref, g_ref, shl_ref, scl_ref, shc_ref, scc_ref, w_ref, ws_ref, cos_ref, sin_ref,
                   o_ref, small_ref, h_ref, *, tm, tn, seq, tiles_per_batch, n_rope_tiles, n_q_tiles):
    j = pl.program_id(1)

    @pl.when(j == 0)
    def _():
        h = _adaln_tile(x_ref[...], g_ref[...], shl_ref[...], scl_ref[...], shc_ref[...], scc_ref[...],
                        (pl.program_id(0) % tiles_per_batch) * tm, seq)
        h_ref[...] = h.astype(h_ref.dtype)
        small_ref[...] = jnp.dot(h_ref[...], ws_ref[...], preferred_element_type=F32)

    acc = jnp.dot(h_ref[...], w_ref[...], preferred_element_type=F32)

    @pl.when(j >= n_rope_tiles)
    def _():
        o_ref[...] = acc.astype(o_ref.dtype)

    @pl.when(j < n_rope_tiles)
    def _():
        scale = jnp.where(j < n_q_tiles, DA_HD ** -0.5, 1.0).astype(F32)
        cos = cos_ref[...] * scale
        sin = sin_ref[...] * scale
        lane = lax.broadcasted_iota(jnp.int32, cos.shape, 1)
        first = (lane % DA_HD) < (DA_HD // 2)
        for c in range(tn // LANES):
            a = acc[:, c * LANES:(c + 1) * LANES]
            sw = jnp.where(first, pltpu.roll(a, LANES - DA_HD // 2, 1), pltpu.roll(a, DA_HD // 2, 1))
            o_ref[:, c * LANES:(c + 1) * LANES] = (a * cos + sw * sin).astype(o_ref.dtype)


def _in_proj(xu, g, shl, scl, shc, scc, w_main, w_small, cos_t, sin_t, seq):
    bsz, t, d = xu.shape
    n = w_main.shape[1]
    tm = _pick(t, (1152, 768, 384, 256, 128))
    tpb = t // tm
    tn = 512
    row = lambda i, j: (i // tpb, i % tpb, 0)
    per_batch = pl.BlockSpec((None, 1, d), lambda i, j: (i // tpb, 0, 0))
    const = lambda shape: pl.BlockSpec(shape, lambda i, j: (0, 0))
    return pl.pallas_call(
        functools.partial(_inproj_kernel, tm=tm, tn=tn, seq=seq, tiles_per_batch=tpb,
                          n_rope_tiles=OFF_DA_V // tn, n_q_tiles=OFF_DA_K // tn),
        grid=(bsz * tpb, n // tn),
        in_specs=[
            pl.BlockSpec((None, tm, d), row),
            const((1, d)), per_batch, per_batch, const((1, d)), const((1, d)),
            pl.BlockSpec((d, tn), lambda i, j: (0, j)),
            const((d, LANES)),
            pl.BlockSpec((tm, LANES), lambda i, j: (i % tpb, 0)),
            pl.BlockSpec((tm, LANES), lambda i, j: (i % tpb, 0)),
        ],
        out_specs=[pl.BlockSpec((None, tm, tn), lambda i, j: (i // tpb, i % tpb, j)),
                   pl.BlockSpec((None, tm, LANES), row)],
        out_shape=[jax.ShapeDtypeStruct((bsz, t, n), BF16), jax.ShapeDtypeStruct((bsz, t, LANES), F32)],
        scratch_shapes=[pltpu.VMEM((tm, d), BF16)],
        compiler_params=_cparams(("parallel", "arbitrary")),
        name="in_proj",
    )(xu, g, shl, scl, shc, scc, w_main, w_small, cos_t, sin_t)


def _rope_tables(seq, t):
    rows = seq // GRID_W
    row = jnp.repeat(jnp.arange(rows, dtype=F32), GRID_W)
    col = jnp.tile(jnp.arange(GRID_W, dtype=F32), rows)
    inv = ROPE_THETA ** (-jnp.arange(ROPE_PAIRS_AXIS, dtype=F32) / ROPE_PAIRS_AXIS)
    ang = jnp.concatenate([row[:, None] * inv, col[:, None] * inv], axis=-1)
    cos, sin = jnp.cos(ang), jnp.sin(ang)
    cos_t = jnp.tile(cos, (1, LANES // (DA_HD // 2)))
    sin_t = jnp.tile(jnp.concatenate([-sin, sin], axis=-1), (1, LANES // DA_HD))
    pad = t - seq
    cos_t = jnp.concatenate([cos_t, jnp.ones((pad, LANES), F32)], axis=0)
    sin_t = jnp.concatenate([sin_t, jnp.zeros((pad, LANES), F32)], axis=0)
    return cos_t, sin_t


ATTN_ROW_GROUPS = 4
ATTN_HEADS_PER_STEP = 4


def _attn_kernel(lam_ref, g_ref, q_ref, k_ref, v_ref, o_ref, *, seq, tq, lam_init):
    qi = pl.program_id(2)
    nh = ATTN_HEADS_PER_STEP
    lp = lam_ref[...]
    l1 = jnp.sum(lp[0:1] * lp[1:2], axis=-1, keepdims=True)
    l2 = jnp.sum(lp[2:3] * lp[3:4], axis=-1, keepdims=True)
    lam = jnp.exp(l1) - jnp.exp(l2) + lam_init
    lane = lax.broadcasted_iota(jnp.int32, (tq, LANES), 1)
    qqs = []
    for h in range(nh):
        q = q_ref[:, h * LANES:(h + 1) * LANES].astype(F32)
        qqs.append(jnp.concatenate([jnp.where(lane < DA_HD, q, 0.0), jnp.where(lane >= DA_HD, q, 0.0)],
                                   axis=0).astype(BF16))

    def core(k_of, v_of):
        rs = 2 * tq // ATTN_ROW_GROUPS
        scores = [[lax.dot_general(qqs[h][i * rs:(i + 1) * rs], k_of(h), (((1,), (1,)), ((), ())),
                                   preferred_element_type=F32) for i in range(ATTN_ROW_GROUPS)] for h in range(nh)]
        for h in range(nh):
            outs = []
            for s in scores[h]:
                m = jnp.max(s, axis=-1, keepdims=True)
                p = jnp.exp(s - m)
                den = jnp.sum(p, axis=-1, keepdims=True)
                outs.append(jnp.dot(p.astype(BF16), v_of(h), preferred_element_type=F32) / den)
            o = jnp.concatenate(outs, axis=0)
            o = o[:tq] - lam * o[tq:]
            y = o * lax.rsqrt(jnp.mean(o * o, axis=-1, keepdims=True) + RMS_EPS) * g_ref[...]
            o_ref[:, h * LANES:(h + 1) * LANES] = (y * (1.0 - lam_init)).astype(o_ref.dtype)

    @pl.when(qi * tq < seq)
    def _():
        core(lambda h: k_ref[:, h * LANES:(h + 1) * LANES], lambda h: v_ref[:, h * LANES:(h + 1) * LANES])

    @pl.when(qi * tq >= seq)
    def _():
        core(lambda h: k_ref[seq:, h * LANES:(h + 1) * LANES], lambda h: v_ref[seq:, h * LANES:(h + 1) * LANES])


def _diff_attention(p, lam_params, subln_g, seq, rows, lam_init):
    bsz, t, _ = p.shape
    tq = _pick(math.gcd(seq, t - seq), (256, 128))
    nh = ATTN_HEADS_PER_STEP
    w = nh * LANES
    cq, ck, cv = OFF_DA_Q // w, OFF_DA_K // w, OFF_DA_V // w
    return pl.pallas_call(
        functools.partial(_attn_kernel, seq=seq, tq=tq, lam_init=lam_init),
        grid=(bsz, DA_HEADS // nh, rows // tq),
        in_specs=[
            pl.BlockSpec((4, DA_HD), lambda b, h, i: (0, 0)),
            pl.BlockSpec((1, 2 * DA_HD), lambda b, h, i: (0, 0)),
            pl.BlockSpec((None, tq, w), lambda b, h, i: (b, i, cq + h)),
            pl.BlockSpec((None, t, w), lambda b, h, i: (b, 0, ck + h)),
            pl.BlockSpec((None, t, w), lambda b, h, i: (b, 0, cv + h)),
        ],
        out_specs=pl.BlockSpec((None, tq, w), lambda b, h, i: (b, i, h)),
        out_shape=jax.ShapeDtypeStruct((bsz, rows, BRANCH_W), BF16),
        compiler_params=_cparams(("parallel", "parallel", "arbitrary")),
        name="diff_attention",
    )(lam_params, subln_g.reshape(1, -1), p, p, p)


def _gmlp_kernel(u_ref, v_ref, lng_ref, lnb_ref, ws_ref, bs_ref, o_ref, *, nchunks):
    for c in range(nchunks):
        r0 = c * GM_CHUNK
        u = _gelu_tanh(u_ref[r0:r0 + GM_CHUNK, :].astype(F32))
        v = _gelu_tanh(v_ref[r0:r0 + GM_CHUNK, :].astype(F32))
        xc = v - jnp.mean(v, axis=-1, keepdims=True)
        var = jnp.mean(xc * xc, axis=-1, keepdims=True)
        vn = (xc * lax.rsqrt(var + RMS_EPS) * lng_ref[...] + lnb_ref[...]).astype(BF16)
        for g in range(GM_GROUPS):
            cs = slice(g * GM_GW, (g + 1) * GM_GW)
            s = jnp.dot(ws_ref[g], vn[:, cs], preferred_element_type=F32) + bs_ref[g]
            o_ref[r0:r0 + GM_CHUNK, cs] = (u[:, cs] * s).astype(o_ref.dtype)


def _spatial_gating(p, ln_g, ln_b, ws, bs, rows):
    bsz, t, _ = p.shape
    tm = _pick(rows, (768, 512, 384, 256, 128))
    cu, cv = OFF_GM_U // BRANCH_W, OFF_GM_V // BRANCH_W
    bs_b = jnp.broadcast_to(bs[:, :, None], (GM_GROUPS, GM_CHUNK, GM_GW)).astype(F32)
    return pl.pallas_call(
        functools.partial(_gmlp_kernel, nchunks=tm // GM_CHUNK),
        grid=(bsz, rows // tm),
        in_specs=[
            pl.BlockSpec((None, tm, BRANCH_W), lambda b, i: (b, i, cu)),
            pl.BlockSpec((None, tm, BRANCH_W), lambda b, i: (b, i, cv)),
            pl.BlockSpec((1, BRANCH_W), lambda b, i: (0, 0)),
            pl.BlockSpec((1, BRANCH_W), lambda b, i: (0, 0)),
            pl.BlockSpec((GM_GROUPS, GM_CHUNK, GM_CHUNK), lambda b, i: (0, 0, 0)),
            pl.BlockSpec((GM_GROUPS, GM_CHUNK, GM_GW), lambda b, i: (0, 0, 0)),
        ],
        out_specs=pl.BlockSpec((None, tm, BRANCH_W), lambda b, i: (b, i, 0)),
        out_shape=jax.ShapeDtypeStruct((bsz, rows, BRANCH_W), BF16),
        compiler_params=_cparams(("parallel", "parallel")),
        name="spatial_gating",
    )(p, p, ln_g.reshape(1, -1), ln_b.reshape(1, -1), ws.astype(BF16), bs_b)


DN_BASE = 8
DN_PREP_GROUPS = (4, 6, 3, 2, 1)
DN_HEAD_GROUP = 2


def _dn_kernel(alog_ref, dtb_ref, q_ref, k_ref, v_ref, z_ref, sm_ref, ar_ref, cw_ref, ng_ref, o_ref,
               qn_ref, kn_ref, vn_ref, rowg_ref, ac_ref, b_ref, d_ref, cd_ref, oacc_ref, st_ref,
               *, seq, t):
    hg = DN_HEAD_GROUP
    hblk = pl.program_id(1)
    nc = t // DN_CHUNK
    n_lat = seq // DN_CHUNK
    n_ctx = nc - n_lat
    hw = DN_HD
    cw = DN_CHUNK

    row = lax.broadcasted_iota(jnp.int32, (t, 1), 0)
    seg_lo = jnp.where(row < seq, 0, seq)
    seg_hi = jnp.where(row < seq, seq, t)

    def conv_silu(x_ref, w):
        x = x_ref[...].astype(F32)
        acc = x * w[DN_CONV // 2:DN_CONV // 2 + 1, :]
        for s in (-2, -1, 1, 2):
            xs = pltpu.roll(x, (-s) % t, 0)
            rs = row + s
            ok = (rs >= seg_lo) & (rs < seg_hi)
            acc = acc + jnp.where(ok, xs, 0.0) * w[DN_CONV // 2 + s:DN_CONV // 2 + s + 1, :]
        return _silu(acc)

    def l2n(x):
        return x * lax.rsqrt(jnp.sum(x * x, axis=-1, keepdims=True) + RMS_EPS)

    qc = conv_silu(q_ref, cw_ref[0])
    kc = conv_silu(k_ref, cw_ref[1])
    vn_ref[...] = conv_silu(v_ref, cw_ref[2])
    for j in range(hg):
        cs = slice(j * hw, (j + 1) * hw)
        qn_ref[:, cs] = l2n(qc[:, cs]) * (DN_HD ** -0.5)
        kn_ref[:, cs] = l2n(kc[:, cs])

    def softplus(x):
        return jnp.maximum(x, 0.0) + jnp.log1p(jnp.exp(-jnp.abs(x)))

    rw = 2 * hg * cw
    lane_r = lax.broadcasted_iota(jnp.int32, (1, rw), 1)
    chain_r = lane_r // cw
    pos_r = lane_r % cw
    alog_r = jnp.zeros((1, rw), F32)
    dt_r = jnp.zeros((1, rw), F32)
    for d in range(2):
        for j in range(hg):
            alog_r = jnp.where(chain_r == hg * d + j, alog_ref[d, hblk * hg + j], alog_r)
            dt_r = jnp.where(chain_r == hg * d + j, dtb_ref[d, hblk * hg + j], dt_r)
    g_all = -jnp.exp(alog_r) * softplus(ar_ref[...].reshape(nc * 8, rw) + dt_r)
    pre = g_all
    suf = g_all
    sh = 1
    while sh < cw:
        pre = pre + jnp.where(pos_r >= sh, pltpu.roll(pre, sh, 1), 0.0)
        suf = suf + jnp.where(pos_r < cw - sh, pltpu.roll(suf, rw - sh, 1), 0.0)
        sh *= 2
    run = jnp.where(lane_r >= hg * cw, suf, pre).reshape(nc, 8, rw)
    tot = (pre + suf - g_all).reshape(nc, 8, rw)
    sub = lax.broadcasted_iota(jnp.int32, (nc, 8, rw), 1)
    both = jnp.where(sub == 0, run, tot)
    for d in range(2):
        rowg_ref[d] = both[:, :, d * hg * cw:(d + 1) * hg * cw]

    st_ref[...] = jnp.zeros(st_ref.shape, F32)

    pshape = (cw, hg * cw)
    ii = lax.broadcasted_iota(jnp.int32, pshape, 0)
    lp = lax.broadcasted_iota(jnp.int32, pshape, 1)
    jl = lp % cw
    left = lp < cw
    diag = ii == jl
    eye_p = jnp.where(diag, 1.0, 0.0).astype(F32)
    blk_base = (ii // DN_BASE) == (jl // DN_BASE)
    incl = [ii >= jl, ii <= jl]
    strict = [ii > jl, ii < jl]
    half = [jnp.where(left, 1.0, 0.0).astype(BF16), jnp.where(left, 0.0, 1.0).astype(BF16)]
    left_sq = lax.broadcasted_iota(jnp.int32, (LANES, LANES), 1) < cw

    def blockdiag(b16):
        return jnp.concatenate([b16 * half[0], b16 * half[1]], axis=0)

    def pprod(a, b):
        return jnp.dot(a.astype(BF16), blockdiag(b.astype(BF16)), preferred_element_type=F32)

    def tri_inverse(lmats):
        ms = [jnp.where(blk_base, -l, 0.0) for l in lmats]
        xs = [eye_p + m for m in ms]
        pws = [pprod(m, m) for m in ms]
        span = 4
        while span <= DN_BASE:
            tts = [pprod(jnp.concatenate([x, pw], axis=0), pw) for x, pw in zip(xs, pws)]
            xs = [x + tt[:cw] for x, tt in zip(xs, tts)]
            pws = [tt[cw:] for tt in tts]
            span *= 2
        bs = DN_BASE
        while bs < cw:
            off = ((ii // (2 * bs)) == (jl // (2 * bs))) & ((ii // bs) != (jl // bs))
            cmats = [jnp.where(off, l, 0.0) for l in lmats]
            ys = [pprod(x, c) for x, c in zip(xs, cmats)]
            zs = [pprod(y, x) for y, x in zip(ys, xs)]
            xs = [x - z for x, z in zip(xs, zs)]
            bs *= 2
        return xs

    def prep_load(c):
        rows = pl.ds(pl.multiple_of(c * cw, cw), cw)
        kk = [kn_ref[rows, j * hw:(j + 1) * hw] for j in range(hg)]
        qq = [qn_ref[rows, j * hw:(j + 1) * hw] for j in range(hg)]
        vv = [vn_ref[rows, j * hw:(j + 1) * hw] for j in range(hg)]
        return kk, qq, vv, sm_ref[rows, :], [rowg_ref[d, c] for d in range(2)]

    def prep_compute(loaded):
        n = len(loaded)
        gram, qk, ktp = [], [], []
        for kk, qq, vv, sm, rgs in loaded:
            gq, kt = [], []
            for j in range(hg):
                kb = kk[j].astype(BF16)
                gq.append(lax.dot_general(jnp.concatenate([kb, qq[j].astype(BF16)], axis=0),
                                          jnp.concatenate([kb, kb], axis=0), (((1,), (1,)), ((), ())),
                                          preferred_element_type=F32))
                kt.append(jnp.concatenate([kk[j], kk[j]], axis=0).T)
            pair = jnp.where(left_sq, gq[0], gq[1])
            gram.append(pair[:cw])
            qk.append(pair[cw:])
            ktp.append(jnp.where(left_sq, kt[0], kt[1]))
        pre = []
        for ci, (kk, qq, vv, sm, rgs) in enumerate(loaded):
            lane_c = lax.broadcasted_iota(jnp.int32, sm.shape, 1)
            for d in range(2):
                bcol = [_sigmoid(jnp.sum(jnp.where(lane_c == d * DN_HEADS + hblk * hg + j, sm, 0.0),
                                         axis=1, keepdims=True)) for j in range(hg)]
                gc_row, g_tot = rgs[d][0:1, :], rgs[d][1:2, :]
                gdiag = jnp.where(diag, gc_row, 0.0)
                gcol = [jnp.sum(jnp.where(left, gdiag, 0.0), axis=1, keepdims=True),
                        jnp.sum(jnp.where(left, 0.0, gdiag), axis=1, keepdims=True)]
                gc = jnp.where(left, gcol[0], gcol[1])
                beta = jnp.where(left, bcol[0], bcol[1])
                dec = jnp.exp(jnp.where(incl[d], gc - gc_row, -jnp.inf))
                lmat = jnp.where(strict[d], beta * gram[ci] * dec, 0.0)
                pre.append((ci, d, bcol, gcol, gc_row, g_tot, dec, lmat))
        tinvs = tri_inverse([p[-1] for p in pre])
        rhss, egs = [], []
        for ci, d, bcol, gcol, gc_row, g_tot, dec, lmat in pre:
            kk, qq, vv = loaded[ci][:3]
            eg = [jnp.exp(gcol[j]) for j in range(hg)]
            egs.append(eg)
            rhss.append(jnp.concatenate(
                [jnp.concatenate([vv[j] * bcol[j], kk[j] * (bcol[j] * eg[j])], axis=1) for j in range(hg)],
                axis=0).astype(BF16))
        sols = []
        for tinv, rhs in zip(tinvs, rhss):
            t16 = tinv.astype(BF16)
            sols.append(jnp.dot(jnp.concatenate([t16 * half[0], t16 * half[1]], axis=0), rhs,
                                preferred_element_type=F32))
        xs = []
        for (ci, d, bcol, gcol, gc_row, g_tot, dec, lmat), sol in zip(pre, sols):
            q_intra = jnp.where(incl[d], qk[ci] * dec, 0.0)
            qk2 = jnp.concatenate([q_intra, ktp[ci] * jnp.exp(g_tot - gc_row)], axis=0).astype(BF16)
            s16 = sol.astype(BF16)
            zero = jnp.zeros((cw, 2 * hw), BF16)
            bd = jnp.concatenate([jnp.concatenate([s16[:cw], zero], axis=1),
                                  jnp.concatenate([zero, s16[cw:]], axis=1)], axis=0)
            xs.append(jnp.dot(qk2, bd, preferred_element_type=F32))
        outs = [[None, None] for _ in range(n)]
        for (ci, d, bcol, gcol, gc_row, g_tot, dec, lmat), x, eg in zip(pre, xs, egs):
            qq = loaded[ci][1]
            e_tot = jnp.exp(g_tot)
            outs[ci][d] = dict(
                ac=[jnp.concatenate([x[cw:, j * 2 * hw + hw:(j + 1) * 2 * hw],
                                     qq[j] * eg[j] - x[:cw, j * 2 * hw + hw:(j + 1) * 2 * hw]], axis=0).astype(BF16)
                    for j in range(hg)],
                b=[x[cw:, j * 2 * hw:j * 2 * hw + hw].astype(BF16) for j in range(hg)],
                dd=[x[:cw, j * 2 * hw:j * 2 * hw + hw].astype(BF16) for j in range(hg)],
                cd=[jnp.broadcast_to(e_tot[:, j * cw:j * cw + 1], (1, hw)) for j in range(hg)])
        return outs

    def prep_store(c, outs):
        rows = pl.ds(pl.multiple_of(c * cw, cw), cw)
        for d in range(2):
            for j in range(hg):
                ac_ref[hg * d + j, c] = outs[d]["ac"][j]
                b_ref[hg * d + j, c] = outs[d]["b"][j]
                d_ref[hg * d + j, rows, :] = outs[d]["dd"][j]
                cd_ref[hg * d + j, c] = outs[d]["cd"][j]

    group = _pick(nc, DN_PREP_GROUPS)

    def prep_body(g, carry):
        cs = [g * group + cc for cc in range(group)]
        loaded = [prep_load(c) for c in cs]
        outs = prep_compute(loaded)
        for c, o in zip(cs, outs):
            prep_store(c, o)
        return carry

    lax.fori_loop(0, nc // group, prep_body, 0)

    def scan_body(i, carry):
        cf = jnp.where(i < n_ctx, i + n_lat, i - n_ctx)
        cb = nc - 1 - i
        dirs = ((0, cf), (1, cb))
        rows = [pl.ds(pl.multiple_of(c * cw, cw), cw) for _, c in dirs]
        state = [st_ref[s] for s in range(2 * hg)]
        ac = [ac_ref[hg * d + j, c] for d, c in dirs for j in range(hg)]
        bb = [b_ref[hg * d + j, c] for d, c in dirs for j in range(hg)]
        dd = [d_ref[hg * d + j, rows[d], :] for d, _ in dirs for j in range(hg)]
        cd = [cd_ref[hg * d + j, c] for d, c in dirs for j in range(hg)]
        rs = [jnp.dot(ac[s], state[s].astype(BF16), preferred_element_type=F32) for s in range(2 * hg)]
        o_new = [jnp.concatenate([rs[hg * d + j][hw:] + dd[hg * d + j].astype(F32) for j in range(hg)], axis=1)
                 for d in range(2)]
        st_new = [state[s] * cd[s] - rs[s][:hw] + bb[s].astype(F32) for s in range(2 * hg)]
        for d, _ in dirs:
            oacc_ref[d, rows[d], :] = o_new[d]
        for s in range(2 * hg):
            st_ref[s] = st_new[s]
        return carry

    lax.fori_loop(0, nc, scan_body, 0)

    o = oacc_ref[0] + oacc_ref[1]
    for j in range(hg):
        cs = slice(j * hw, (j + 1) * hw)
        oj = o[:, cs]
        y = oj * lax.rsqrt(jnp.mean(oj * oj, axis=-1, keepdims=True) + RMS_EPS) * ng_ref[...]
        o_ref[:, cs] = (y * _silu(z_ref[:, cs].astype(F32))).astype(o_ref.dtype)


def _gated_deltanet(p, small, conv_w, a_log, dt_bias, norm_g, seq):
    bsz, t, _ = p.shape
    nc = t // DN_CHUNK
    hg = DN_HEAD_GROUP
    assert hg == 2
    w = hg * DN_HD
    rw = 2 * hg * DN_CHUNK
    a = small[..., 2 * DN_HEADS:4 * DN_HEADS].reshape(bsz, nc, DN_CHUNK, 2, DN_HEADS // hg, hg)
    a_row = jnp.transpose(a, (0, 4, 1, 3, 5, 2)).reshape(bsz, DN_HEADS // hg, nc, 1, rw)
    a_row = jnp.broadcast_to(a_row, (bsz, DN_HEADS // hg, nc, 8, rw))
    cq, ck, cv, cz = (OFF_DN_Q // w, OFF_DN_K // w, OFF_DN_V // w, OFF_DN_Z // w)
    slab = lambda c0: pl.BlockSpec((None, t, w), lambda b, h: (b, 0, c0 + h))
    smem = pl.BlockSpec(memory_space=pltpu.SMEM)
    return pl.pallas_call(
        functools.partial(_dn_kernel, seq=seq, t=t),
        grid=(bsz, DN_HEADS // hg),
        in_specs=[
            smem, smem,
            slab(cq), slab(ck), slab(cv), slab(cz),
            pl.BlockSpec((None, t, LANES), lambda b, h: (b, 0, 0)),
            pl.BlockSpec((None, None, nc, 8, rw), lambda b, h: (b, h, 0, 0, 0)),
            pl.BlockSpec((3, DN_CONV, w), lambda b, h: (0, 0, h)),
            pl.BlockSpec((1, DN_HD), lambda b, h: (0, 0)),
        ],
        out_specs=pl.BlockSpec((None, t, w), lambda b, h: (b, 0, h)),
        out_shape=jax.ShapeDtypeStruct((bsz, t, BRANCH_W), BF16),
        scratch_shapes=[
            pltpu.VMEM((t, w), F32), pltpu.VMEM((t, w), F32), pltpu.VMEM((t, w), F32),
            pltpu.VMEM((2, nc, 8, hg * DN_CHUNK), F32),
            pltpu.VMEM((2 * hg, nc, DN_HD + DN_CHUNK, DN_HD), BF16),
            pltpu.VMEM((2 * hg, nc, DN_HD, DN_HD), BF16),
            pltpu.VMEM((2 * hg, t, DN_HD), BF16),
            pltpu.VMEM((2 * hg, nc, 1, DN_HD), F32),
            pltpu.VMEM((2, t, w), F32),
            pltpu.VMEM((2 * hg, DN_HD, DN_HD), F32),
        ],
        compiler_params=_cparams(("parallel", "parallel")),
        name="gated_deltanet",
    )(a_log, dt_bias, p, p, p, p, small, a_row, conv_w, norm_g.reshape(1, -1))


def _merge_kernel(ya_ref, yg_ref, yd_ref, ga_ref, gg_ref, gd_ref, wb_ref, bg_ref, o_ref):
    acc = None
    for i, (y_ref, g_ref) in enumerate(((ya_ref, ga_ref), (yg_ref, gg_ref), (yd_ref, gd_ref))):
        gate = _sigmoid(g_ref[...].astype(F32) + bg_ref[i])
        term = gate * jnp.dot(y_ref[...], wb_ref[i], preferred_element_type=F32)
        acc = term if acc is None else acc + term
    o_ref[...] = acc.astype(o_ref.dtype)


def _merge(ya, yg, yd, p, w_branch, b_gate, rows):
    bsz = p.shape[0]
    d = D_MODEL
    tm = _pick(rows, (768, 512, 384, 256, 128))
    tn = 512
    g0 = OFF_GATE_MAIN // tn
    y_spec = pl.BlockSpec((None, tm, BRANCH_W), lambda b, i, j: (b, i, 0))
    gate_spec = lambda k: pl.BlockSpec((None, tm, tn), lambda b, i, j: (b, i, g0 + k * (d // tn) + j))
    return pl.pallas_call(
        _merge_kernel,
        grid=(bsz, rows // tm, d // tn),
        in_specs=[y_spec, y_spec, y_spec, gate_spec(0), gate_spec(1), gate_spec(2),
                  pl.BlockSpec((N_BRANCH, BRANCH_W, tn), lambda b, i, j: (0, 0, j)),
                  pl.BlockSpec((N_BRANCH, 1, tn), lambda b, i, j: (0, 0, j))],
        out_specs=pl.BlockSpec((None, tm, tn), lambda b, i, j: (b, i, j)),
        out_shape=jax.ShapeDtypeStruct((bsz, rows, d), BF16),
        compiler_params=_cparams(("parallel", "parallel", "arbitrary")),
        name="merge_branches",
    )(ya, yg, yd, p, p, p, w_branch, b_gate.reshape(N_BRANCH, 1, d))


def _outproj_kernel(z_ref, w_ref, x_ref, gl_ref, gc_ref, o_ref, *, tm, seq):
    acc = jnp.dot(z_ref[...], w_ref[...], preferred_element_type=F32)
    row = pl.program_id(1) * tm + lax.broadcasted_iota(jnp.int32, (tm, 1), 0)
    gate = jnp.where(row >= seq, gc_ref[...], gl_ref[...])
    o_ref[...] = x_ref[...] + gate * acc


def _out_proj_residual(z, w_out, xu, gate_l, gate_c, seq, rows):
    bsz, t, d = xu.shape
    tm = _pick(rows, (768, 512, 384, 256, 128))
    tn = 512
    return pl.pallas_call(
        functools.partial(_outproj_kernel, tm=tm, seq=seq),
        grid=(bsz, rows // tm, d // tn),
        in_specs=[
            pl.BlockSpec((None, tm, d), lambda b, i, j: (b, i, 0)),
            pl.BlockSpec((d, tn), lambda b, i, j: (0, j)),
            pl.BlockSpec((None, tm, tn), lambda b, i, j: (b, i, j)),
            pl.BlockSpec((None, 1, tn), lambda b, i, j: (b, 0, j)),
            pl.BlockSpec((1, tn), lambda b, i, j: (0, j)),
        ],
        out_specs=pl.BlockSpec((None, tm, tn), lambda b, i, j: (b, i, j)),
        out_shape=jax.ShapeDtypeStruct((bsz, rows, d), F32),
        compiler_params=_cparams(("parallel", "parallel", "arbitrary")),
        name="out_proj_residual",
    )(z, w_out, xu, gate_l, gate_c)


W1_BLOCK = 2 * LANES


def _w1_prep_kernel(w_ref, perm_ref, o_ref):
    w = w_ref[...].astype(BF16)
    for blk in range(w.shape[1] // W1_BLOCK):
        cs = slice(blk * W1_BLOCK, (blk + 1) * W1_BLOCK)
        o_ref[:, cs] = jnp.dot(w[:, cs], perm_ref[...], preferred_element_type=F32).astype(o_ref.dtype)


def _w1_prep(w_e1):
    nl, ne, d, n = w_e1.shape
    tk = 1024
    j = jnp.arange(W1_BLOCK)
    src = jnp.where(j < LANES, 2 * j, 2 * (j - LANES) + 1)
    perm = (jnp.arange(W1_BLOCK)[:, None] == src[None, :]).astype(BF16)
    return pl.pallas_call(
        _w1_prep_kernel,
        grid=(nl * ne, d // tk),
        in_specs=[pl.BlockSpec((None, tk, n), lambda e, k: (e, k, 0)),
                  pl.BlockSpec((W1_BLOCK, W1_BLOCK), lambda e, k: (0, 0))],
        out_specs=pl.BlockSpec((None, tk, n), lambda e, k: (e, k, 0)),
        out_shape=jax.ShapeDtypeStruct((nl * ne, d, n), BF16),
        compiler_params=_cparams(("parallel", "parallel")),
        name="expert_w1_prep",
    )(w_e1.reshape(nl * ne, d, n), perm)


def _regroup_bias(b_e1):
    ne, n = b_e1.shape
    return jnp.transpose(b_e1.reshape(ne, n // W1_BLOCK, LANES, 2), (0, 1, 3, 2)).reshape(ne, 1, n)


def _expert_kernel(be_ref, bv_ref, x_ref, w1_ref, b1_ref, w2_ref, b2_ref, o_ref, hid_ref):
    i = pl.program_id(0)

    @pl.when(bv_ref[i] > 0)
    def _():
        hgl = jnp.dot(x_ref[...], w1_ref[...], preferred_element_type=F32) + b1_ref[...]
        for blk in range(hgl.shape[1] // W1_BLOCK):
            xg = jnp.minimum(hgl[:, blk * W1_BLOCK:blk * W1_BLOCK + LANES], SWIGLU_LIMIT)
            xl = jnp.clip(hgl[:, blk * W1_BLOCK + LANES:(blk + 1) * W1_BLOCK], -SWIGLU_LIMIT, SWIGLU_LIMIT)
            hid_ref[:, blk * LANES:(blk + 1) * LANES] = (
                xg * _sigmoid(SWIGLU_ALPHA * xg) * (xl + 1.0)).astype(hid_ref.dtype)
        y = jnp.dot(hid_ref[...], w2_ref[...], preferred_element_type=F32) + b2_ref[...]
        o_ref[...] = y.astype(o_ref.dtype)


def _experts(xs, blk_e, blk_valid, w1, b1, w2, b2, e0):
    n_rows, d = xs.shape
    tm = MOE_TM
    ff = EXPERT_FF
    grid_spec = pltpu.PrefetchScalarGridSpec(
        num_scalar_prefetch=2,
        grid=(n_rows // tm,),
        in_specs=[
            pl.BlockSpec((tm, d), lambda i, be, bv: (i, 0)),
            pl.BlockSpec((None, d, 2 * ff), lambda i, be, bv: (e0 + be[i], 0, 0)),
            pl.BlockSpec((None, 1, 2 * ff), lambda i, be, bv: (be[i], 0, 0)),
            pl.BlockSpec((None, ff, d), lambda i, be, bv: (be[i], 0, 0)),
            pl.BlockSpec((None, 1, d), lambda i, be, bv: (be[i], 0, 0)),
        ],
        out_specs=pl.BlockSpec((tm, d), lambda i, be, bv: (i, 0)),
        scratch_shapes=[pltpu.VMEM((tm, ff), BF16)],
    )
    return pl.pallas_call(
        _expert_kernel,
        grid_spec=grid_spec,
        out_shape=jax.ShapeDtypeStruct((n_rows, d), BF16),
        compiler_params=_cparams(("arbitrary",)),
        name="moe_experts",
    )(blk_e, blk_valid, xs, w1, b1, w2, b2)


def _moe(h2, top_i, w1, b1, w2, b2, e0):
    n_tok, d = h2.shape
    tm = MOE_TM
    n_assign = n_tok * TOP_K
    flat_e = top_i.reshape(n_assign)
    order = jnp.argsort(flat_e).astype(jnp.int32)
    rank = jnp.argsort(order).astype(jnp.int32)
    onehot = flat_e[:, None] == jnp.arange(N_EXPERTS, dtype=flat_e.dtype)[None, :]
    counts = jnp.sum(onehot, axis=0, dtype=jnp.int32)
    padded = (counts + tm - 1) // tm * tm
    pad_end = jnp.cumsum(padded)
    start = jnp.cumsum(counts) - counts
    shift = (pad_end - padded) - start
    pos = rank + jnp.sum(jnp.where(onehot, shift[None, :], 0), axis=1)
    n_blocks = -(-n_assign // tm) + N_EXPERTS
    blk_start = jnp.arange(n_blocks, dtype=jnp.int32) * tm
    blk_valid = (blk_start < pad_end[-1]).astype(jnp.int32)
    blk_e = jnp.sum(blk_start[:, None] >= pad_end[None, :], axis=1, dtype=jnp.int32)
    last_e = jnp.sum(pad_end[-1] - 1 >= pad_end, dtype=jnp.int32)
    blk_e = jnp.where(blk_valid > 0, blk_e, last_e)
    row = blk_start[:, None] + jnp.arange(tm, dtype=jnp.int32)[None, :]
    src = row - shift[blk_e][:, None]
    lo = start[blk_e][:, None]
    live = (src >= lo) & (src < lo + counts[blk_e][:, None]) & (blk_valid[:, None] > 0)
    row_tok = jnp.where(live, order[jnp.clip(src, 0, n_assign - 1)] // TOP_K, row % n_tok).reshape(n_blocks * tm)
    xs = h2[row_tok]
    y = _experts(xs, blk_e, blk_valid, w1, b1, w2, b2, e0)
    return y[pos.reshape(n_tok, TOP_K).T.reshape(n_assign)].reshape(TOP_K, n_tok, d)


def _combine_kernel(y_ref, w_ref, x_ref, gl_ref, gc_ref, *rest, tm, seq):
    o_ref = rest[-1]
    w = w_ref[...]
    acc = y_ref[0].astype(F32) * w[:, 0:1]
    for k in range(1, TOP_K):
        acc = acc + y_ref[k].astype(F32) * w[:, k:k + 1]
    row = pl.program_id(1) * tm + lax.broadcasted_iota(jnp.int32, (tm, 1), 0)
    gate = jnp.where(row >= seq, gc_ref[...], gl_ref[...])
    o_ref[...] = x_ref[...] + gate * acc


def _moe_combine(yk, top_w, xu, gate_l, gate_c, seq, b0, prev):
    bsz, rows, d = xu.shape
    bp = yk.shape[1]
    tm = _pick(rows, (512, 384, 256, 128))
    in_specs = [
        pl.BlockSpec((TOP_K, None, tm, d), lambda b, i: (0, b, i, 0)),
        pl.BlockSpec((None, tm, LANES), lambda b, i: (b + b0, i, 0)),
        pl.BlockSpec((None, tm, d), lambda b, i: (b + b0, i, 0)),
        pl.BlockSpec((None, 1, d), lambda b, i: (b + b0, 0, 0)),
        pl.BlockSpec((1, d), lambda b, i: (0, 0)),
    ]
    args = [yk, top_w, xu, gate_l, gate_c]
    aliases = {}
    if prev is not None:
        in_specs.append(pl.BlockSpec(memory_space=pl.ANY))
        args.append(prev)
        aliases = {len(args) - 1: 0}
    return pl.pallas_call(
        functools.partial(_combine_kernel, tm=tm, seq=seq),
        grid=(bp, rows // tm),
        in_specs=in_specs,
        out_specs=pl.BlockSpec((None, tm, d), lambda b, i: (b + b0, i, 0)),
        out_shape=jax.ShapeDtypeStruct((bsz, rows, d), F32),
        input_output_aliases=aliases,
        compiler_params=_cparams(("parallel", "parallel")),
        name="moe_combine",
    )(*args)


def _final_kernel(x_ref, g_ref, o_ref):
    x = x_ref[...]
    o_ref[...] = x * lax.rsqrt(jnp.mean(x * x, axis=-1, keepdims=True) + RMS_EPS) * g_ref[...]


def _final_norm(xu, g, seq):
    bsz, t, d = xu.shape
    tm = _pick(seq, (512, 256, 128))
    return pl.pallas_call(
        _final_kernel,
        grid=(bsz, seq // tm),
        in_specs=[pl.BlockSpec((None, tm, d), lambda b, i: (b, i, 0)),
                  pl.BlockSpec((1, d), lambda b, i: (0, 0))],
        out_specs=pl.BlockSpec((None, tm, d), lambda b, i: (b, i, 0)),
        out_shape=jax.ShapeDtypeStruct((bsz, seq, d), F32),
        compiler_params=_cparams(("parallel", "parallel")),
        name="final_norm",
    )(xu, g.reshape(1, d))


def _layer(xu, mod_l, mod_c, seq, layer_idx, ctx_out, cos_t, sin_t, norm1, w_in, da_lambda, da_subln,
           gm_ln_g, gm_ln_b, gm_ws, gm_bs, dn_conv, dn_a_log, dn_dt_bias, dn_norm, b_gate, w_branch,
           w_out, norm2, w_router, b_router, w1_all, b_e1, w_e2, b_e2):
    bsz, t, d = xu.shape
    rows = t if ctx_out else seq
    lam_init = 0.8 - 0.6 * math.exp(-0.3 * layer_idx)
    ml = [mod_l[:, k:k + 1, :] for k in range(6)]
    mc = [mod_c[k:k + 1, :] for k in range(6)]

    w_main = jnp.concatenate([w_in[:, :OFF_SMALL], w_in[:, OFF_GATE:]], axis=1).astype(BF16)
    w_small = jnp.zeros((d, LANES), BF16).at[:, :OFF_GATE - OFF_SMALL].set(
        w_in[:, OFF_SMALL:OFF_GATE].astype(BF16))
    p, small = _in_proj(xu, norm1.reshape(1, d), ml[0], ml[1], mc[0], mc[1], w_main, w_small, cos_t, sin_t, seq)

    ya = _diff_attention(p, da_lambda, da_subln, seq, rows, lam_init)
    yg = _spatial_gating(p, gm_ln_g, gm_ln_b, gm_ws, gm_bs, rows)
    yd = _gated_deltanet(p, small, dn_conv, dn_a_log, dn_dt_bias, dn_norm, seq)
    z = _merge(ya, yg, yd, p, w_branch.astype(BF16), b_gate, rows)
    xu = _out_proj_residual(z, w_out.astype(BF16), xu, ml[2], mc[2], seq, rows)

    h2, top_i, top_w = _adaln_router(xu, norm2.reshape(1, d), ml[3], ml[4], mc[3], mc[4],
                                     w_router, b_router, seq, rows)
    parts = MOE_PARTS if bsz % MOE_PARTS == 0 else 1
    bp = bsz // parts
    b1p, w2p, b2p = _regroup_bias(b_e1), w_e2.astype(BF16), b_e2[:, None, :]
    yks = [_moe(h2[i * bp:(i + 1) * bp].reshape(bp * rows, d),
                top_i[i * bp:(i + 1) * bp].reshape(bp * rows, LANES)[:, :TOP_K],
                w1_all, b1p, w2p, b2p, layer_idx * N_EXPERTS).reshape(TOP_K, bp, rows, d) for i in range(parts)]
    out = None
    for i in range(parts):
        out = _moe_combine(yks[i], top_w, xu, ml[5], mc[5], seq, i * bp, out)
    return out


def kernel(x, c, ctx, c_ctx, w_mod, b_mod, norm1, w_in, da_lambda, da_subln, gm_ln_g, gm_ln_b, gm_ws, gm_bs,
           dn_conv, dn_a_log, dn_dt_bias, dn_norm, b_gate, w_branch, w_out, norm2, w_router, b_router,
           w_e1, b_e1, w_e2, b_e2, norm_f):
    bsz, seq, d = x.shape
    n_ctx = ctx.shape[1]
    t = seq + n_ctx
    depth = w_mod.shape[0]
    xu = jnp.concatenate([x, ctx], axis=1)
    r = -(-(bsz + 1) // 8) * 8
    cond = jnp.zeros((r, d), F32).at[:bsz].set(c).at[bsz].set(c_ctx)
    mod = _modulation(cond, w_mod, b_mod).reshape(depth, r, 6, d)
    cos_t, sin_t = _rope_tables(seq, t)
    w1_all = _w1_prep(w_e1)
    for l in range(depth):
        xu = _layer(xu, mod[l, :bsz], mod[l, bsz], seq, l, l < depth - 1, cos_t, sin_t, norm1[l], w_in[l],
                    da_lambda[l], da_subln[l], gm_ln_g[l], gm_ln_b[l], gm_ws[l], gm_bs[l], dn_conv[l],
                    dn_a_log[l], dn_dt_bias[l], dn_norm[l], b_gate[l], w_branch[l], w_out[l], norm2[l],
                    w_router[l], b_router[l], w1_all, b_e1[l], w_e2[l], b_e2[l])
    return _final_norm(xu, norm_f, seq)
```

```python
import functools
import math

import jax
import jax.numpy as jnp
from jax import lax
from jax.experimental import pallas as pl
from jax.experimental.pallas import tpu as pltpu

F32 = jnp.float32
BF16 = jnp.bfloat16

D_MODEL = 2048
GRID_W = 64
RMS_EPS = 1e-6
BRANCH_W = D_MODEL // 2
N_BRANCH = 3
DA_HD = 64
DA_HEADS = BRANCH_W // (2 * DA_HD)
ROPE_THETA = 10000.0
ROPE_PAIRS_AXIS = DA_HD // 4
GM_CHUNK = 128
GM_GW = 128
GM_GROUPS = BRANCH_W // GM_GW
DN_HD = 128
DN_HEADS = BRANCH_W // DN_HD
DN_CHUNK = 64
DN_CONV = 5
N_EXPERTS = 32
TOP_K = 4
EXPERT_FF = D_MODEL // 2
SWIGLU_LIMIT = 7.0
SWIGLU_ALPHA = 1.702

LANES = 128
VMEM_LIMIT = 56 * 1024 * 1024

OFF_DA_Q = 0
OFF_DA_K = 1024
OFF_DA_V = 2048
OFF_GM_U = 3072
OFF_GM_V = 4096
OFF_DN_Q = 5120
OFF_DN_K = 6144
OFF_DN_V = 7168
OFF_DN_Z = 8192
OFF_SMALL = 9216
OFF_GATE = 9248
N_MAIN = 9216 + N_BRANCH * D_MODEL
OFF_GATE_MAIN = 9216

MOE_TM = 512
MOE_PARTS = 1


def _cparams(sem):
    return pltpu.CompilerParams(dimension_semantics=sem, vmem_limit_bytes=VMEM_LIMIT)


def _pick(n, cands):
    for c in cands:
        if n % c == 0:
            return c
    raise ValueError(f"no tile for {n} in {cands}")


def _sigmoid(x):
    return jax.nn.sigmoid(x)


def _silu(x):
    return x * _sigmoid(x)


def _gelu_tanh(x):
    return x * (0.5 * (1.0 + jnp.tanh(0.7978845608028654 * (x + 0.044715 * (x * x * x)))))


def _bdot(a, b):
    return jnp.dot(a.astype(BF16), b.astype(BF16), preferred_element_type=F32)


def _split(a):
    hi = a.astype(BF16)
    lo = (a - hi.astype(F32)).astype(BF16)
    return hi, lo


def _dot3(a, b):
    ah, al = _split(a)
    bh, bl = _split(b)
    return (jnp.dot(ah, bh, preferred_element_type=F32)
            + (jnp.dot(al, bh, preferred_element_type=F32)
               + jnp.dot(ah, bl, preferred_element_type=F32)))


def _mod_kernel(c_ref, w_ref, b_ref, o_ref):
    s = _silu(c_ref[...])
    o_ref[...] = _bdot(s, w_ref[...]) + b_ref[...]


def _modulation(cond, w_mod, b_mod):
    nl, d, n6 = w_mod.shape
    r = cond.shape[0]
    tn = 1024
    return pl.pallas_call(
        _mod_kernel,
        grid=(nl, n6 // tn),
        in_specs=[
            pl.BlockSpec((r, d), lambda l, j: (0, 0)),
            pl.BlockSpec((None, d, tn), lambda l, j: (l, 0, j)),
            pl.BlockSpec((None, 1, tn), lambda l, j: (l, 0, j)),
        ],
        out_specs=pl.BlockSpec((None, r, tn), lambda l, j: (l, 0, j)),
        out_shape=jax.ShapeDtypeStruct((nl, r, n6), F32),
        compiler_params=_cparams(("arbitrary", "arbitrary")),
        name="modulation",
    )(cond, w_mod, b_mod.reshape(nl, 1, n6))


def _adaln_tile(x, g, shl, scl, shc, scc, row0, seq):
    tm = x.shape[0]
    y = x * lax.rsqrt(jnp.mean(x * x, axis=-1, keepdims=True) + RMS_EPS) * g
    row = row0 + lax.broadcasted_iota(jnp.int32, (tm, 1), 0)
    is_ctx = row >= seq
    scale = jnp.where(is_ctx, scc, scl)
    shift = jnp.where(is_ctx, shc, shl)
    return y * (1.0 + scale) + shift


def _mod_specs(d):
    return [
        pl.BlockSpec((1, d), lambda b, i: (0, 0)),
        pl.BlockSpec((None, 1, d), lambda b, i: (b, 0, 0)),
        pl.BlockSpec((None, 1, d), lambda b, i: (b, 0, 0)),
        pl.BlockSpec((1, d), lambda b, i: (0, 0)),
        pl.BlockSpec((1, d), lambda b, i: (0, 0)),
    ]


def _adaln_router_kernel(x_ref, g_ref, shl_ref, scl_ref, shc_ref, scc_ref, wr_ref, br_ref,
                         h_ref, idx_ref, wt_ref, *, tm, seq):
    h = _adaln_tile(x_ref[...], g_ref[...], shl_ref[...], scl_ref[...], shc_ref[...], scc_ref[...],
                    pl.program_id(1) * tm, seq)
    h_ref[...] = h.astype(h_ref.dtype)
    logits = _dot3(h, wr_ref[...]) + br_ref[...]
    lane = lax.broadcasted_iota(jnp.int32, logits.shape, 1).astype(F32)
    vals, idxs = [], []
    cur = logits
    for _ in range(TOP_K):
        m = jnp.max(cur, axis=-1, keepdims=True)
        am = jnp.min(jnp.where(cur == m, lane, float(LANES)), axis=-1, keepdims=True)
        vals.append(m)
        idxs.append(am)
        cur = jnp.where(lane == am, -jnp.inf, cur)
    es = [jnp.exp(v - vals[0]) for v in vals]
    tot = es[0] + es[1] + es[2] + es[3]
    wt = jnp.zeros(logits.shape, F32)
    ix = jnp.zeros(logits.shape, F32)
    for k in range(TOP_K):
        wt = jnp.where(lane == k, es[k] / tot, wt)
        ix = jnp.where(lane == k, idxs[k], ix)
    idx_ref[...] = ix.astype(jnp.int32)
    wt_ref[...] = wt


def _adaln_router(xu, g, shl, scl, shc, scc, w_router, b_router, seq, rows):
    bsz, t, d = xu.shape
    tm = _pick(rows, (768, 512, 384, 256, 128))
    wr = jnp.zeros((d, LANES), F32).at[:, :N_EXPERTS].set(w_router)
    br = jnp.full((1, LANES), -1e30, F32).at[0, :N_EXPERTS].set(b_router)
    return pl.pallas_call(
        functools.partial(_adaln_router_kernel, tm=tm, seq=seq),
        grid=(bsz, rows // tm),
        in_specs=[pl.BlockSpec((None, tm, d), lambda b, i: (b, i, 0))] + _mod_specs(d) + [
            pl.BlockSpec((d, LANES), lambda b, i: (0, 0)),
            pl.BlockSpec((1, LANES), lambda b, i: (0, 0)),
        ],
        out_specs=[
            pl.BlockSpec((None, tm, d), lambda b, i: (b, i, 0)),
            pl.BlockSpec((None, tm, LANES), lambda b, i: (b, i, 0)),
            pl.BlockSpec((None, tm, LANES), lambda b, i: (b, i, 0)),
        ],
        out_shape=[
            jax.ShapeDtypeStruct((bsz, rows, d), BF16),
            jax.ShapeDtypeStruct((bsz, rows, LANES), jnp.int32),
            jax.ShapeDtypeStruct((bsz, rows, LANES), F32),
        ],
        compiler_params=_cparams(("parallel", "parallel")),
        name="adaln_router",
    )(xu, g, shl, scl, shc, scc, wr, br)


def _inproj_kernel(x_ref, g_ref, shl_ref, scl_ref, shc_ref, scc_ref, w_ref, ws_ref, cos_ref, sin_ref,
                   o_ref, small_ref, h_ref, *, tm, tn, seq, tiles_per_batch, n_rope_tiles, n_q_tiles):
    j = pl.program_id(1)

    @pl.when(j == 0)
    def _():
        h = _adaln_tile(x_ref[...], g_ref[...], shl_ref[...], scl_ref[...], shc_ref[...], scc_ref[...],
                        (pl.program_id(0) % tiles_per_batch) * tm, seq)
        h_ref[...] = h.astype(h_ref.dtype)
        small_ref[...] = jnp.dot(h_ref[...], ws_ref[...], preferred_element_type=F32)

    acc = jnp.dot(h_ref[...], w_ref[...], preferred_element_type=F32)

    @pl.when(j >= n_rope_tiles)
    def _():
        o_ref[...] = acc.astype(o_ref.dtype)

    @pl.when(j < n_rope_tiles)
    def _():
        scale = jnp.where(j < n_q_tiles, DA_HD ** -0.5, 1.0).astype(F32)
        cos = cos_ref[...] * scale
        sin = sin_ref[...] * scale
        lane = lax.broadcasted_iota(jnp.int32, cos.shape, 1)
        first = (lane % DA_HD) < (DA_HD // 2)
        for c in range(tn // LANES):
            a = acc[:, c * LANES:(c + 1) * LANES]
            sw = jnp.where(first, pltpu.roll(a, LANES - DA_HD // 2, 1), pltpu.roll(a, DA_HD // 2, 1))
            o_ref[:, c * LANES:(c + 1) * LANES] = (a * cos + sw * sin).astype(o_ref.dtype)


def _in_proj(xu, g, shl, scl, shc, scc, w_main, w_small, cos_t, sin_t, seq):
    bsz, t, d = xu.shape
    n = w_main.shape[1]
    tm = _pick(t, (2304, 1152, 768, 384, 256, 128))
    tpb = t // tm
    tn = 512
    row = lambda i, j: (i // tpb, i % tpb, 0)
    per_batch = pl.BlockSpec((None, 1, d), lambda i, j: (i // tpb, 0, 0))
    const = lambda shape: pl.BlockSpec(shape, lambda i, j: (0, 0))
    return pl.pallas_call(
        functools.partial(_inproj_kernel, tm=tm, tn=tn, seq=seq, tiles_per_batch=tpb,
                          n_rope_tiles=OFF_DA_V // tn, n_q_tiles=OFF_DA_K // tn),
        grid=(bsz * tpb, n // tn),
        in_specs=[
            pl.BlockSpec((None, tm, d), row, pipeline_mode=pl.Buffered(1)),
            const((1, d)), per_batch, per_batch, const((1, d)), const((1, d)),
            pl.BlockSpec((d, tn), lambda i, j: (0, j)),
            const((d, LANES)),
            pl.BlockSpec((tm, LANES), lambda i, j: (i % tpb, 0)),
            pl.BlockSpec((tm, LANES), lambda i, j: (i % tpb, 0)),
        ],
        out_specs=[pl.BlockSpec((None, tm, tn), lambda i, j: (i // tpb, i % tpb, j)),
                   pl.BlockSpec((None, tm, LANES), row)],
        out_shape=[jax.ShapeDtypeStruct((bsz, t, n), BF16), jax.ShapeDtypeStruct((bsz, t, LANES), F32)],
        scratch_shapes=[pltpu.VMEM((tm, d), BF16)],
        compiler_params=_cparams(("parallel", "arbitrary")),
        name="in_proj",
    )(xu, g, shl, scl, shc, scc, w_main, w_small, cos_t, sin_t)


def _rope_tables(seq, t):
    rows = seq // GRID_W
    row = jnp.repeat(jnp.arange(rows, dtype=F32), GRID_W)
    col = jnp.tile(jnp.arange(GRID_W, dtype=F32), rows)
    inv = ROPE_THETA ** (-jnp.arange(ROPE_PAIRS_AXIS, dtype=F32) / ROPE_PAIRS_AXIS)
    ang = jnp.concatenate([row[:, None] * inv, col[:, None] * inv], axis=-1)
    cos, sin = jnp.cos(ang), jnp.sin(ang)
    cos_t = jnp.tile(cos, (1, LANES // (DA_HD // 2)))
    sin_t = jnp.tile(jnp.concatenate([-sin, sin], axis=-1), (1, LANES // DA_HD))
    pad = t - seq
    cos_t = jnp.concatenate([cos_t, jnp.ones((pad, LANES), F32)], axis=0)
    sin_t = jnp.concatenate([sin_t, jnp.zeros((pad, LANES), F32)], axis=0)
    return cos_t, sin_t


ATTN_ROW_GROUPS = 4
ATTN_HEADS_PER_STEP = 4


def _attn_kernel(lam_ref, g_ref, q_ref, k_ref, v_ref, o_ref, *, seq, tq, lam_init):
    qi = pl.program_id(2)
    nh = ATTN_HEADS_PER_STEP
    lp = lam_ref[...]
    l1 = jnp.sum(lp[0:1] * lp[1:2], axis=-1, keepdims=True)
    l2 = jnp.sum(lp[2:3] * lp[3:4], axis=-1, keepdims=True)
    lam = jnp.exp(l1) - jnp.exp(l2) + lam_init
    lane = lax.broadcasted_iota(jnp.int32, (tq, LANES), 1)
    qqs = []
    for h in range(nh):
        q = q_ref[:, h * LANES:(h + 1) * LANES].astype(F32)
        qqs.append(jnp.concatenate([jnp.where(lane < DA_HD, q, 0.0), jnp.where(lane >= DA_HD, q, 0.0)],
                                   axis=0).astype(BF16))

    def core(k_of, v_of):
        rs = 2 * tq // ATTN_ROW_GROUPS
        scores = [[lax.dot_general(qqs[h][i * rs:(i + 1) * rs], k_of(h), (((1,), (1,)), ((), ())),
                                   preferred_element_type=F32) for i in range(ATTN_ROW_GROUPS)] for h in range(nh)]
        for h in range(nh):
            outs = []
            for s in scores[h]:
                m = jnp.max(s, axis=-1, keepdims=True)
                p = jnp.exp(s - m)
                den = jnp.sum(p, axis=-1, keepdims=True)
                outs.append(jnp.dot(p.astype(BF16), v_of(h), preferred_element_type=F32) / den)
            o = jnp.concatenate(outs, axis=0)
            o = o[:tq] - lam * o[tq:]
            y = o * lax.rsqrt(jnp.mean(o * o, axis=-1, keepdims=True) + RMS_EPS) * g_ref[...]
            o_ref[:, h * LANES:(h + 1) * LANES] = (y * (1.0 - lam_init)).astype(o_ref.dtype)

    @pl.when(qi * tq < seq)
    def _():
        core(lambda h: k_ref[:, h * LANES:(h + 1) * LANES], lambda h: v_ref[:, h * LANES:(h + 1) * LANES])

    @pl.when(qi * tq >= seq)
    def _():
        core(lambda h: k_ref[seq:, h * LANES:(h + 1) * LANES], lambda h: v_ref[seq:, h * LANES:(h + 1) * LANES])


def _diff_attention(p, lam_params, subln_g, seq, rows, lam_init):
    bsz, t, _ = p.shape
    tq = _pick(math.gcd(seq, t - seq), (256, 128))
    nh = ATTN_HEADS_PER_STEP
    w = nh * LANES
    cq, ck, cv = OFF_DA_Q // w, OFF_DA_K // w, OFF_DA_V // w
    return pl.pallas_call(
        functools.partial(_attn_kernel, seq=seq, tq=tq, lam_init=lam_init),
        grid=(bsz, DA_HEADS // nh, rows // tq),
        in_specs=[
            pl.BlockSpec((4, DA_HD), lambda b, h, i: (0, 0)),
            pl.BlockSpec((1, 2 * DA_HD), lambda b, h, i: (0, 0)),
            pl.BlockSpec((None, tq, w), lambda b, h, i: (b, i, cq + h)),
            pl.BlockSpec((None, t, w), lambda b, h, i: (b, 0, ck + h)),
            pl.BlockSpec((None, t, w), lambda b, h, i: (b, 0, cv + h)),
        ],
        out_specs=pl.BlockSpec((None, tq, w), lambda b, h, i: (b, i, h)),
        out_shape=jax.ShapeDtypeStruct((bsz, rows, BRANCH_W), BF16),
        compiler_params=_cparams(("parallel", "parallel", "arbitrary")),
        name="diff_attention",
    )(lam_params, subln_g.reshape(1, -1), p, p, p)


def _gmlp_kernel(u_ref, v_ref, lng_ref, lnb_ref, ws_ref, bs_ref, o_ref, *, nchunks):
    for c in range(nchunks):
        r0 = c * GM_CHUNK
        u = _gelu_tanh(u_ref[r0:r0 + GM_CHUNK, :].astype(F32))
        v = _gelu_tanh(v_ref[r0:r0 + GM_CHUNK, :].astype(F32))
        xc = v - jnp.mean(v, axis=-1, keepdims=True)
        var = jnp.mean(xc * xc, axis=-1, keepdims=True)
        vn = (xc * lax.rsqrt(var + RMS_EPS) * lng_ref[...] + lnb_ref[...]).astype(BF16)
        for g in range(GM_GROUPS):
            cs = slice(g * GM_GW, (g + 1) * GM_GW)
            s = jnp.dot(ws_ref[g], vn[:, cs], preferred_element_type=F32) + bs_ref[g]
            o_ref[r0:r0 + GM_CHUNK, cs] = (u[:, cs] * s).astype(o_ref.dtype)


def _spatial_gating(p, ln_g, ln_b, ws, bs, rows):
    bsz, t, _ = p.shape
    tm = _pick(rows, (768, 512, 384, 256, 128))
    cu, cv = OFF_GM_U // BRANCH_W, OFF_GM_V // BRANCH_W
    bs_b = jnp.broadcast_to(bs[:, :, None], (GM_GROUPS, GM_CHUNK, GM_GW)).astype(F32)
    return pl.pallas_call(
        functools.partial(_gmlp_kernel, nchunks=tm // GM_CHUNK),
        grid=(bsz, rows // tm),
        in_specs=[
            pl.BlockSpec((None, tm, BRANCH_W), lambda b, i: (b, i, cu)),
            pl.BlockSpec((None, tm, BRANCH_W), lambda b, i: (b, i, cv)),
            pl.BlockSpec((1, BRANCH_W), lambda b, i: (0, 0)),
            pl.BlockSpec((1, BRANCH_W), lambda b, i: (0, 0)),
            pl.BlockSpec((GM_GROUPS, GM_CHUNK, GM_CHUNK), lambda b, i: (0, 0, 0)),
            pl.BlockSpec((GM_GROUPS, GM_CHUNK, GM_GW), lambda b, i: (0, 0, 0)),
        ],
        out_specs=pl.BlockSpec((None, tm, BRANCH_W), lambda b, i: (b, i, 0)),
        out_shape=jax.ShapeDtypeStruct((bsz, rows, BRANCH_W), BF16),
        compiler_params=_cparams(("parallel", "parallel")),
        name="spatial_gating",
    )(p, p, ln_g.reshape(1, -1), ln_b.reshape(1, -1), ws.astype(BF16), bs_b)


DN_BASE = 8
DN_PREP_GROUPS = (4, 6, 3, 2, 1)
DN_HEAD_GROUP = 2


def _dn_kernel(alog_ref, dtb_ref, q_ref, k_ref, v_ref, z_ref, sm_ref, ar_ref, cw_ref, ng_ref, o_ref,
               qn_ref, kn_ref, vn_ref, rowg_ref, ac_ref, b_ref, d_ref, cd_ref, oacc_ref, st_ref,
               *, seq, t):
    hg = DN_HEAD_GROUP
    hblk = pl.program_id(1)
    nc = t // DN_CHUNK
    n_lat = seq // DN_CHUNK
    n_ctx = nc - n_lat
    hw = DN_HD
    cw = DN_CHUNK

    row = lax.broadcasted_iota(jnp.int32, (t, 1), 0)
    seg_lo = jnp.where(row < seq, 0, seq)
    seg_hi = jnp.where(row < seq, seq, t)

    def conv_silu(x_ref, w):
        x = x_ref[...].astype(F32)
        acc = x * w[DN_CONV // 2:DN_CONV // 2 + 1, :]
        for s in (-2, -1, 1, 2):
            xs = pltpu.roll(x, (-s) % t, 0)
            rs = row + s
            ok = (rs >= seg_lo) & (rs < seg_hi)
            acc = acc + jnp.where(ok, xs, 0.0) * w[DN_CONV // 2 + s:DN_CONV // 2 + s + 1, :]
        return _silu(acc)

    def l2n(x):
        return x * lax.rsqrt(jnp.sum(x * x, axis=-1, keepdims=True) + RMS_EPS)

    qc = conv_silu(q_ref, cw_ref[0])
    kc = conv_silu(k_ref, cw_ref[1])
    vn_ref[...] = conv_silu(v_ref, cw_ref[2])
    for j in range(hg):
        cs = slice(j * hw, (j + 1) * hw)
        qn_ref[:, cs] = l2n(qc[:, cs]) * (DN_HD ** -0.5)
        kn_ref[:, cs] = l2n(kc[:, cs])

    def softplus(x):
        return jnp.maximum(x, 0.0) + jnp.log1p(jnp.exp(-jnp.abs(x)))

    rw = 2 * hg * cw
    lane_r = lax.broadcasted_iota(jnp.int32, (1, rw), 1)
    chain_r = lane_r // cw
    pos_r = lane_r % cw
    alog_r = jnp.zeros((1, rw), F32)
    dt_r = jnp.zeros((1, rw), F32)
    for d in range(2):
        for j in range(hg):
            alog_r = jnp.where(chain_r == hg * d + j, alog_ref[d, hblk * hg + j], alog_r)
            dt_r = jnp.where(chain_r == hg * d + j, dtb_ref[d, hblk * hg + j], dt_r)
    g_all = -jnp.exp(alog_r) * softplus(ar_ref[...].reshape(nc * 8, rw) + dt_r)
    pre = g_all
    suf = g_all
    sh = 1
    while sh < cw:
        pre = pre + jnp.where(pos_r >= sh, pltpu.roll(pre, sh, 1), 0.0)
        suf = suf + jnp.where(pos_r < cw - sh, pltpu.roll(suf, rw - sh, 1), 0.0)
        sh *= 2
    run = jnp.where(lane_r >= hg * cw, suf, pre).reshape(nc, 8, rw)
    tot = (pre + suf - g_all).reshape(nc, 8, rw)
    sub = lax.broadcasted_iota(jnp.int32, (nc, 8, rw), 1)
    both = jnp.where(sub == 0, run, tot)
    for d in range(2):
        rowg_ref[d] = both[:, :, d * hg * cw:(d + 1) * hg * cw]

    st_ref[...] = jnp.zeros(st_ref.shape, F32)

    pshape = (cw, hg * cw)
    ii = lax.broadcasted_iota(jnp.int32, pshape, 0)
    lp = lax.broadcasted_iota(jnp.int32, pshape, 1)
    jl = lp % cw
    left = lp < cw
    diag = ii == jl
    eye_p = jnp.where(diag, 1.0, 0.0).astype(F32)
    blk_base = (ii // DN_BASE) == (jl // DN_BASE)
    incl = [ii >= jl, ii <= jl]
    strict = [ii > jl, ii < jl]
    half = [jnp.where(left, 1.0, 0.0).astype(BF16), jnp.where(left, 0.0, 1.0).astype(BF16)]
    left_sq = lax.broadcasted_iota(jnp.int32, (LANES, LANES), 1) < cw

    def blockdiag(b16):
        return jnp.concatenate([b16 * half[0], b16 * half[1]], axis=0)

    def pprod(a, b):
        return jnp.dot(a.astype(BF16), blockdiag(b.astype(BF16)), preferred_element_type=F32)

    def tri_inverse(lmats):
        ms = [jnp.where(blk_base, -l, 0.0) for l in lmats]
        xs = [eye_p + m for m in ms]
        pws = [pprod(m, m) for m in ms]
        span = 4
        while span <= DN_BASE:
            tts = [pprod(jnp.concatenate([x, pw], axis=0), pw) for x, pw in zip(xs, pws)]
            xs = [x + tt[:cw] for x, tt in zip(xs, tts)]
            pws = [tt[cw:] for tt in tts]
            span *= 2
        bs = DN_BASE
        while bs < cw:
            off = ((ii // (2 * bs)) == (jl // (2 * bs))) & ((ii // bs) != (jl // bs))
            cmats = [jnp.where(off, l, 0.0) for l in lmats]
            ys = [pprod(x, c) for x, c in zip(xs, cmats)]
            zs = [pprod(y, x) for y, x in zip(ys, xs)]
            xs = [x - z for x, z in zip(xs, zs)]
            bs *= 2
        return xs

    def prep_load(c):
        rows = pl.ds(pl.multiple_of(c * cw, cw), cw)
        kk = [kn_ref[rows, j * hw:(j + 1) * hw] for j in range(hg)]
        qq = [qn_ref[rows, j * hw:(j + 1) * hw] for j in range(hg)]
        vv = [vn_ref[rows, j * hw:(j + 1) * hw] for j in range(hg)]
        return kk, qq, vv, sm_ref[rows, :], [rowg_ref[d, c] for d in range(2)]

    def prep_compute(loaded):
        n = len(loaded)
        gram, qk, ktp = [], [], []
        for kk, qq, vv, sm, rgs in loaded:
            gq, kt = [], []
            for j in range(hg):
                kb = kk[j].astype(BF16)
                gq.append(lax.dot_general(jnp.concatenate([kb, qq[j].astype(BF16)], axis=0),
                                          jnp.concatenate([kb, kb], axis=0), (((1,), (1,)), ((), ())),
                                          preferred_element_type=F32))
                kt.append(jnp.concatenate([kk[j], kk[j]], axis=0).T)
            pair = jnp.where(left_sq, gq[0], gq[1])
            gram.append(pair[:cw])
            qk.append(pair[cw:])
            ktp.append(jnp.where(left_sq, kt[0], kt[1]))
        pre = []
        for ci, (kk, qq, vv, sm, rgs) in enumerate(loaded):
            lane_c = lax.broadcasted_iota(jnp.int32, sm.shape, 1)
            for d in range(2):
                bcol = [_sigmoid(jnp.sum(jnp.where(lane_c == d * DN_HEADS + hblk * hg + j, sm, 0.0),
                                         axis=1, keepdims=True)) for j in range(hg)]
                gc_row, g_tot = rgs[d][0:1, :], rgs[d][1:2, :]
                gdiag = jnp.where(diag, gc_row, 0.0)
                gcol = [jnp.sum(jnp.where(left, gdiag, 0.0), axis=1, keepdims=True),
                        jnp.sum(jnp.where(left, 0.0, gdiag), axis=1, keepdims=True)]
                gc = jnp.where(left, gcol[0], gcol[1])
                beta = jnp.where(left, bcol[0], bcol[1])
                dec = jnp.exp(jnp.where(incl[d], gc - gc_row, -jnp.inf))
                lmat = jnp.where(strict[d], beta * gram[ci] * dec, 0.0)
                pre.append((ci, d, bcol, gcol, gc_row, g_tot, dec, lmat))
        tinvs = tri_inverse([p[-1] for p in pre])
        rhss, egs = [], []
        for ci, d, bcol, gcol, gc_row, g_tot, dec, lmat in pre:
            kk, qq, vv = loaded[ci][:3]
            eg = [jnp.exp(gcol[j]) for j in range(hg)]
            egs.append(eg)
            rhss.append(jnp.concatenate(
                [jnp.concatenate([vv[j] * bcol[j], kk[j] * (bcol[j] * eg[j])], axis=1) for j in range(hg)],
                axis=0).astype(BF16))
        sols = []
        for tinv, rhs in zip(tinvs, rhss):
            t16 = tinv.astype(BF16)
            sols.append(jnp.dot(jnp.concatenate([t16 * half[0], t16 * half[1]], axis=0), rhs,
                                preferred_element_type=F32))
        xs = []
        for (ci, d, bcol, gcol, gc_row, g_tot, dec, lmat), sol in zip(pre, sols):
            q_intra = jnp.where(incl[d], qk[ci] * dec, 0.0)
            qk2 = jnp.concatenate([q_intra, ktp[ci] * jnp.exp(g_tot - gc_row)], axis=0).astype(BF16)
            s16 = sol.astype(BF16)
            zero = jnp.zeros((cw, 2 * hw), BF16)
            bd = jnp.concatenate([jnp.concatenate([s16[:cw], zero], axis=1),
                                  jnp.concatenate([zero, s16[cw:]], axis=1)], axis=0)
            xs.append(jnp.dot(qk2, bd, preferred_element_type=F32))
        outs = [[None, None] for _ in range(n)]
        for (ci, d, bcol, gcol, gc_row, g_tot, dec, lmat), x, eg in zip(pre, xs, egs):
            qq = loaded[ci][1]
            e_tot = jnp.exp(g_tot)
            outs[ci][d] = dict(
                ac=[jnp.concatenate([x[cw:, j * 2 * hw + hw:(j + 1) * 2 * hw],
                                     qq[j] * eg[j] - x[:cw, j * 2 * hw + hw:(j + 1) * 2 * hw]], axis=0).astype(BF16)
                    for j in range(hg)],
                b=[x[cw:, j * 2 * hw:j * 2 * hw + hw].astype(BF16) for j in range(hg)],
                dd=[x[:cw, j * 2 * hw:j * 2 * hw + hw].astype(BF16) for j in range(hg)],
                cd=[jnp.broadcast_to(e_tot[:, j * cw:j * cw + 1], (1, hw)) for j in range(hg)])
        return outs

    def prep_store(c, outs):
        rows = pl.ds(pl.multiple_of(c * cw, cw), cw)
        for d in range(2):
            for j in range(hg):
                ac_ref[hg * d + j, c] = outs[d]["ac"][j]
                b_ref[hg * d + j, c] = outs[d]["b"][j]
                d_ref[hg * d + j, rows, :] = outs[d]["dd"][j]
                cd_ref[hg * d + j, c] = outs[d]["cd"][j]

    group = _pick(nc, DN_PREP_GROUPS)

    def prep_body(g, carry):
        cs = [g * group + cc for cc in range(group)]
        loaded = [prep_load(c) for c in cs]
        outs = prep_compute(loaded)
        for c, o in zip(cs, outs):
            prep_store(c, o)
        return carry

    lax.fori_loop(0, nc // group, prep_body, 0)

    def scan_body(i, carry):
        cf = jnp.where(i < n_ctx, i + n_lat, i - n_ctx)
        cb = nc - 1 - i
        dirs = ((0, cf), (1, cb))
        rows = [pl.ds(pl.multiple_of(c * cw, cw), cw) for _, c in dirs]
        state = [st_ref[s] for s in range(2 * hg)]
        ac = [ac_ref[hg * d + j, c] for d, c in dirs for j in range(hg)]
        bb = [b_ref[hg * d + j, c] for d, c in dirs for j in range(hg)]
        dd = [d_ref[hg * d + j, rows[d], :] for d, _ in dirs for j in range(hg)]
        cd = [cd_ref[hg * d + j, c] for d, c in dirs for j in range(hg)]
        rs = [jnp.dot(ac[s], state[s].astype(BF16), preferred_element_type=F32) for s in range(2 * hg)]
        o_new = [jnp.concatenate([rs[hg * d + j][hw:] + dd[hg * d + j].astype(F32) for j in range(hg)], axis=1)
                 for d in range(2)]
        st_new = [state[s] * cd[s] - rs[s][:hw] + bb[s].astype(F32) for s in range(2 * hg)]
        for d, _ in dirs:
            oacc_ref[d, rows[d], :] = o_new[d]
        for s in range(2 * hg):
            st_ref[s] = st_new[s]
        return carry

    lax.fori_loop(0, nc, scan_body, 0)

    o = oacc_ref[0] + oacc_ref[1]
    for j in range(hg):
        cs = slice(j * hw, (j + 1) * hw)
        oj = o[:, cs]
        y = oj * lax.rsqrt(jnp.mean(oj * oj, axis=-1, keepdims=True) + RMS_EPS) * ng_ref[...]
        o_ref[:, cs] = (y * _silu(z_ref[:, cs].astype(F32))).astype(o_ref.dtype)


def _gated_deltanet(p, small, conv_w, a_log, dt_bias, norm_g, seq):
    bsz, t, _ = p.shape
    nc = t // DN_CHUNK
    hg = DN_HEAD_GROUP
    assert hg == 2
    w = hg * DN_HD
    rw = 2 * hg * DN_CHUNK
    a = small[..., 2 * DN_HEADS:4 * DN_HEADS].reshape(bsz, nc, DN_CHUNK, 2, DN_HEADS // hg, hg)
    a_row = jnp.transpose(a, (0, 4, 1, 3, 5, 2)).reshape(bsz, DN_HEADS // hg, nc, 1, rw)
    a_row = jnp.broadcast_to(a_row, (bsz, DN_HEADS // hg, nc, 8, rw))
    cq, ck, cv, cz = (OFF_DN_Q // w, OFF_DN_K // w, OFF_DN_V // w, OFF_DN_Z // w)
    slab = lambda c0: pl.BlockSpec((None, t, w), lambda b, h: (b, 0, c0 + h))
    smem = pl.BlockSpec(memory_space=pltpu.SMEM)
    return pl.pallas_call(
        functools.partial(_dn_kernel, seq=seq, t=t),
        grid=(bsz, DN_HEADS // hg),
        in_specs=[
            smem, smem,
            slab(cq), slab(ck), slab(cv), slab(cz),
            pl.BlockSpec((None, t, LANES), lambda b, h: (b, 0, 0)),
            pl.BlockSpec((None, None, nc, 8, rw), lambda b, h: (b, h, 0, 0, 0)),
            pl.BlockSpec((3, DN_CONV, w), lambda b, h: (0, 0, h)),
            pl.BlockSpec((1, DN_HD), lambda b, h: (0, 0)),
        ],
        out_specs=pl.BlockSpec((None, t, w), lambda b, h: (b, 0, h)),
        out_shape=jax.ShapeDtypeStruct((bsz, t, BRANCH_W), BF16),
        scratch_shapes=[
            pltpu.VMEM((t, w), F32), pltpu.VMEM((t, w), F32), pltpu.VMEM((t, w), F32),
            pltpu.VMEM((2, nc, 8, hg * DN_CHUNK), F32),
            pltpu.VMEM((2 * hg, nc, DN_HD + DN_CHUNK, DN_HD), BF16),
            pltpu.VMEM((2 * hg, nc, DN_HD, DN_HD), BF16),
            pltpu.VMEM((2 * hg, t, DN_HD), BF16),
            pltpu.VMEM((2 * hg, nc, 1, DN_HD), F32),
            pltpu.VMEM((2, t, w), F32),
            pltpu.VMEM((2 * hg, DN_HD, DN_HD), F32),
        ],
        compiler_params=_cparams(("parallel", "parallel")),
        name="gated_deltanet",
    )(a_log, dt_bias, p, p, p, p, small, a_row, conv_w, norm_g.reshape(1, -1))


def _merge_kernel(ya_ref, yg_ref, yd_ref, ga_ref, gg_ref, gd_ref, wb_ref, bg_ref, o_ref):
    acc = None
    for i, (y_ref, g_ref) in enumerate(((ya_ref, ga_ref), (yg_ref, gg_ref), (yd_ref, gd_ref))):
        gate = _sigmoid(g_ref[...].astype(F32) + bg_ref[i])
        term = gate * jnp.dot(y_ref[...], wb_ref[i], preferred_element_type=F32)
        acc = term if acc is None else acc + term
    o_ref[...] = acc.astype(o_ref.dtype)


def _merge(ya, yg, yd, p, w_branch, b_gate, rows):
    bsz = p.shape[0]
    d = D_MODEL
    tm = _pick(rows, (768, 512, 384, 256, 128))
    tn = 1024
    g0 = OFF_GATE_MAIN // tn
    y_spec = pl.BlockSpec((None, tm, BRANCH_W), lambda b, i, j: (b, i, 0))
    gate_spec = lambda k: pl.BlockSpec((None, tm, tn), lambda b, i, j: (b, i, g0 + k * (d // tn) + j))
    return pl.pallas_call(
        _merge_kernel,
        grid=(bsz, rows // tm, d // tn),
        in_specs=[y_spec, y_spec, y_spec, gate_spec(0), gate_spec(1), gate_spec(2),
                  pl.BlockSpec((N_BRANCH, BRANCH_W, tn), lambda b, i, j: (0, 0, j)),
                  pl.BlockSpec((N_BRANCH, 1, tn), lambda b, i, j: (0, 0, j))],
        out_specs=pl.BlockSpec((None, tm, tn), lambda b, i, j: (b, i, j)),
        out_shape=jax.ShapeDtypeStruct((bsz, rows, d), BF16),
        compiler_params=_cparams(("parallel", "parallel", "arbitrary")),
        name="merge_branches",
    )(ya, yg, yd, p, p, p, w_branch, b_gate.reshape(N_BRANCH, 1, d))


def _outproj_kernel(z_ref, w_ref, x_ref, gl_ref, gc_ref, o_ref, *, tm, seq):
    acc = jnp.dot(z_ref[...], w_ref[...], preferred_element_type=F32)
    row = pl.program_id(1) * tm + lax.broadcasted_iota(jnp.int32, (tm, 1), 0)
    gate = jnp.where(row >= seq, gc_ref[...], gl_ref[...])
    o_ref[...] = x_ref[...] + gate * acc


def _out_proj_residual(z, w_out, xu, gate_l, gate_c, seq, rows):
    bsz, t, d = xu.shape
    tm = _pick(rows, (768, 512, 384, 256, 128))
    tn = 1024
    return pl.pallas_call(
        functools.partial(_outproj_kernel, tm=tm, seq=seq),
        grid=(bsz, rows // tm, d // tn),
        in_specs=[
            pl.BlockSpec((None, tm, d), lambda b, i, j: (b, i, 0)),
            pl.BlockSpec((d, tn), lambda b, i, j: (0, j)),
            pl.BlockSpec((None, tm, tn), lambda b, i, j: (b, i, j)),
            pl.BlockSpec((None, 1, tn), lambda b, i, j: (b, 0, j)),
            pl.BlockSpec((1, tn), lambda b, i, j: (0, j)),
        ],
        out_specs=pl.BlockSpec((None, tm, tn), lambda b, i, j: (b, i, j)),
        out_shape=jax.ShapeDtypeStruct((bsz, rows, d), F32),
        compiler_params=_cparams(("parallel", "parallel", "arbitrary")),
        name="out_proj_residual",
    )(z, w_out, xu, gate_l, gate_c)


W1_BLOCK = 2 * LANES


def _w1_prep_kernel(w_ref, perm_ref, o_ref):
    w = w_ref[...].astype(BF16)
    for blk in range(w.shape[1] // W1_BLOCK):
        cs = slice(blk * W1_BLOCK, (blk + 1) * W1_BLOCK)
        o_ref[:, cs] = jnp.dot(w[:, cs], perm_ref[...], preferred_element_type=F32).astype(o_ref.dtype)


def _w1_prep(w_e1):
    nl, ne, d, n = w_e1.shape
    tk = 1024
    j = jnp.arange(W1_BLOCK)
    src = jnp.where(j < LANES, 2 * j, 2 * (j - LANES) + 1)
    perm = (jnp.arange(W1_BLOCK)[:, None] == src[None, :]).astype(BF16)
    return pl.pallas_call(
        _w1_prep_kernel,
        grid=(nl * ne, d // tk),
        in_specs=[pl.BlockSpec((None, tk, n), lambda e, k: (e, k, 0)),
                  pl.BlockSpec((W1_BLOCK, W1_BLOCK), lambda e, k: (0, 0))],
        out_specs=pl.BlockSpec((None, tk, n), lambda e, k: (e, k, 0)),
        out_shape=jax.ShapeDtypeStruct((nl * ne, d, n), BF16),
        compiler_params=_cparams(("parallel", "parallel")),
        name="expert_w1_prep",
    )(w_e1.reshape(nl * ne, d, n), perm)


def _regroup_bias(b_e1):
    ne, n = b_e1.shape
    return jnp.transpose(b_e1.reshape(ne, n // W1_BLOCK, LANES, 2), (0, 1, 3, 2)).reshape(ne, 1, n)


def _expert_kernel(be_ref, bv_ref, x_ref, w1_ref, b1_ref, w2_ref, b2_ref, o_ref, hid_ref):
    i = pl.program_id(0)

    @pl.when(bv_ref[i] > 0)
    def _():
        hgl = jnp.dot(x_ref[...], w1_ref[...], preferred_element_type=F32) + b1_ref[...]
        for blk in range(hgl.shape[1] // W1_BLOCK):
            xg = jnp.minimum(hgl[:, blk * W1_BLOCK:blk * W1_BLOCK + LANES], SWIGLU_LIMIT)
            xl = jnp.clip(hgl[:, blk * W1_BLOCK + LANES:(blk + 1) * W1_BLOCK], -SWIGLU_LIMIT, SWIGLU_LIMIT)
            hid_ref[:, blk * LANES:(blk + 1) * LANES] = (
                xg * _sigmoid(SWIGLU_ALPHA * xg) * (xl + 1.0)).astype(hid_ref.dtype)
        y = jnp.dot(hid_ref[...], w2_ref[...], preferred_element_type=F32) + b2_ref[...]
        o_ref[...] = y.astype(o_ref.dtype)


def _experts(xs, blk_e, blk_valid, w1, b1, w2, b2, e0):
    n_rows, d = xs.shape
    tm = MOE_TM
    ff = EXPERT_FF
    grid_spec = pltpu.PrefetchScalarGridSpec(
        num_scalar_prefetch=2,
        grid=(n_rows // tm,),
        in_specs=[
            pl.BlockSpec((tm, d), lambda i, be, bv: (i, 0)),
            pl.BlockSpec((None, d, 2 * ff), lambda i, be, bv: (e0 + be[i], 0, 0)),
            pl.BlockSpec((None, 1, 2 * ff), lambda i, be, bv: (be[i], 0, 0)),
            pl.BlockSpec((None, ff, d), lambda i, be, bv: (be[i], 0, 0)),
            pl.BlockSpec((None, 1, d), lambda i, be, bv: (be[i], 0, 0)),
        ],
        out_specs=pl.BlockSpec((tm, d), lambda i, be, bv: (i, 0)),
        scratch_shapes=[pltpu.VMEM((tm, ff), BF16)],
    )
    return pl.pallas_call(
        _expert_kernel,
        grid_spec=grid_spec,
        out_shape=jax.ShapeDtypeStruct((n_rows, d), BF16),
        compiler_params=_cparams(("arbitrary",)),
        name="moe_experts",
    )(blk_e, blk_valid, xs, w1, b1, w2, b2)


def _moe(h2, top_i, w1, b1, w2, b2, e0):
    n_tok, d = h2.shape
    tm = MOE_TM
    n_assign = n_tok * TOP_K
    flat_e = top_i.reshape(n_assign)
    order = jnp.argsort(flat_e).astype(jnp.int32)
    rank = jnp.argsort(order).astype(jnp.int32)
    onehot = flat_e[:, None] == jnp.arange(N_EXPERTS, dtype=flat_e.dtype)[None, :]
    counts = jnp.sum(onehot, axis=0, dtype=jnp.int32)
    padded = (counts + tm - 1) // tm * tm
    pad_end = jnp.cumsum(padded)
    start = jnp.cumsum(counts) - counts
    shift = (pad_end - padded) - start
    pos = rank + jnp.sum(jnp.where(onehot, shift[None, :], 0), axis=1)
    n_blocks = -(-n_assign // tm) + N_EXPERTS
    blk_start = jnp.arange(n_blocks, dtype=jnp.int32) * tm
    blk_valid = (blk_start < pad_end[-1]).astype(jnp.int32)
    blk_e = jnp.sum(blk_start[:, None] >= pad_end[None, :], axis=1, dtype=jnp.int32)
    last_e = jnp.sum(pad_end[-1] - 1 >= pad_end, dtype=jnp.int32)
    blk_e = jnp.where(blk_valid > 0, blk_e, last_e)
    row = blk_start[:, None] + jnp.arange(tm, dtype=jnp.int32)[None, :]
    src = row - shift[blk_e][:, None]
    lo = start[blk_e][:, None]
    live = (src >= lo) & (src < lo + counts[blk_e][:, None]) & (blk_valid[:, None] > 0)
    row_tok = jnp.where(live, order[jnp.clip(src, 0, n_assign - 1)] // TOP_K, row % n_tok).reshape(n_blocks * tm)
    xs = h2[row_tok]
    y = _experts(xs, blk_e, blk_valid, w1, b1, w2, b2, e0)
    return y[pos.reshape(n_tok, TOP_K).T.reshape(n_assign)].reshape(TOP_K, n_tok, d)


def _combine_kernel(y_ref, w_ref, x_ref, gl_ref, gc_ref, *rest, tm, seq):
    o_ref = rest[-1]
    w = w_ref[...]
    acc = y_ref[0].astype(F32) * w[:, 0:1]
    for k in range(1, TOP_K):
        acc = acc + y_ref[k].astype(F32) * w[:, k:k + 1]
    row = pl.program_id(1) * tm + lax.broadcasted_iota(jnp.int32, (tm, 1), 0)
    gate = jnp.where(row >= seq, gc_ref[...], gl_ref[...])
    o_ref[...] = x_ref[...] + gate * acc


def _moe_combine(yk, top_w, xu, gate_l, gate_c, seq, b0, prev):
    bsz, rows, d = xu.shape
    bp = yk.shape[1]
    tm = _pick(rows, (512, 384, 256, 128))
    in_specs = [
        pl.BlockSpec((TOP_K, None, tm, d), lambda b, i: (0, b, i, 0)),
        pl.BlockSpec((None, tm, LANES), lambda b, i: (b + b0, i, 0)),
        pl.BlockSpec((None, tm, d), lambda b, i: (b + b0, i, 0)),
        pl.BlockSpec((None, 1, d), lambda b, i: (b + b0, 0, 0)),
        pl.BlockSpec((1, d), lambda b, i: (0, 0)),
    ]
    args = [yk, top_w, xu, gate_l, gate_c]
    aliases = {}
    if prev is not None:
        in_specs.append(pl.BlockSpec(memory_space=pl.ANY))
        args.append(prev)
        aliases = {len(args) - 1: 0}
    return pl.pallas_call(
        functools.partial(_combine_kernel, tm=tm, seq=seq),
        grid=(bp, rows // tm),
        in_specs=in_specs,
        out_specs=pl.BlockSpec((None, tm, d), lambda b, i: (b + b0, i, 0)),
        out_shape=jax.ShapeDtypeStruct((bsz, rows, d), F32),
        input_output_aliases=aliases,
        compiler_params=_cparams(("parallel", "parallel")),
        name="moe_combine",
    )(*args)


def _final_kernel(x_ref, g_ref, o_ref):
    x = x_ref[...]
    o_ref[...] = x * lax.rsqrt(jnp.mean(x * x, axis=-1, keepdims=True) + RMS_EPS) * g_ref[...]


def _final_norm(xu, g, seq):
    bsz, t, d = xu.shape
    tm = _pick(seq, (512, 256, 128))
    return pl.pallas_call(
        _final_kernel,
        grid=(bsz, seq // tm),
        in_specs=[pl.BlockSpec((None, tm, d), lambda b, i: (b, i, 0)),
                  pl.BlockSpec((1, d), lambda b, i: (0, 0))],
        out_specs=pl.BlockSpec((None, tm, d), lambda b, i: (b, i, 0)),
        out_shape=jax.ShapeDtypeStruct((bsz, seq, d), F32),
        compiler_params=_cparams(("parallel", "parallel")),
        name="final_norm",
    )(xu, g.reshape(1, d))


def _layer(xu, mod_l, mod_c, seq, layer_idx, ctx_out, cos_t, sin_t, norm1, w_in, da_lambda, da_subln,
           gm_ln_g, gm_ln_b, gm_ws, gm_bs, dn_conv, dn_a_log, dn_dt_bias, dn_norm, b_gate, w_branch,
           w_out, norm2, w_router, b_router, w1_all, b_e1, w_e2, b_e2):
    bsz, t, d = xu.shape
    rows = t if ctx_out else seq
    lam_init = 0.8 - 0.6 * math.exp(-0.3 * layer_idx)
    ml = [mod_l[:, k:k + 1, :] for k in range(6)]
    mc = [mod_c[k:k + 1, :] for k in range(6)]

    w_main = jnp.concatenate([w_in[:, :OFF_SMALL], w_in[:, OFF_GATE:]], axis=1).astype(BF16)
    w_small = jnp.zeros((d, LANES), BF16).at[:, :OFF_GATE - OFF_SMALL].set(
        w_in[:, OFF_SMALL:OFF_GATE].astype(BF16))
    p, small = _in_proj(xu, norm1.reshape(1, d), ml[0], ml[1], mc[0], mc[1], w_main, w_small, cos_t, sin_t, seq)

    ya = _diff_attention(p, da_lambda, da_subln, seq, rows, lam_init)
    yg = _spatial_gating(p, gm_ln_g, gm_ln_b, gm_ws, gm_bs, rows)
    yd = _gated_deltanet(p, small, dn_conv, dn_a_log, dn_dt_bias, dn_norm, seq)
    z = _merge(ya, yg, yd, p, w_branch.astype(BF16), b_gate, rows)
    xu = _out_proj_residual(z, w_out.astype(BF16), xu, ml[2], mc[2], seq, rows)

    h2, top_i, top_w = _adaln_router(xu, norm2.reshape(1, d), ml[3], ml[4], mc[3], mc[4],
                                     w_router, b_router, seq, rows)
    parts = MOE_PARTS if bsz % MOE_PARTS == 0 else 1
    bp = bsz // parts
    b1p, w2p, b2p = _regroup_bias(b_e1), w_e2.astype(BF16), b_e2[:, None, :]
    yks = [_moe(h2[i * bp:(i + 1) * bp].reshape(bp * rows, d),
                top_i[i * bp:(i + 1) * bp].reshape(bp * rows, LANES)[:, :TOP_K],
                w1_all, b1p, w2p, b2p, layer_idx * N_EXPERTS).reshape(TOP_K, bp, rows, d) for i in range(parts)]
    out = None
    for i in range(parts):
        out = _moe_combine(yks[i], top_w, xu, ml[5], mc[5], seq, i * bp, out)
    return out


def kernel(x, c, ctx, c_ctx, w_mod, b_mod, norm1, w_in, da_lambda, da_subln, gm_ln_g, gm_ln_b, gm_ws, gm_bs,
           dn_conv, dn_a_log, dn_dt_bias, dn_norm, b_gate, w_branch, w_out, norm2, w_router, b_router,
           w_e1, b_e1, w_e2, b_e2, norm_f):
    bsz, seq, d = x.shape
    n_ctx = ctx.shape[1]
    t = seq + n_ctx
    depth = w_mod.shape[0]
    xu = jnp.concatenate([x, ctx], axis=1)
    r = -(-(bsz + 1) // 8) * 8
    cond = jnp.zeros((r, d), F32).at[:bsz].set(c).at[bsz].set(c_ctx)
    mod = _modulation(cond, w_mod, b_mod).reshape(depth, r, 6, d)
    cos_t, sin_t = _rope_tables(seq, t)
    w1_all = _w1_prep(w_e1)
    for l in range(depth):
        xu = _layer(xu, mod[l, :bsz], mod[l, bsz], seq, l, l < depth - 1, cos_t, sin_t, norm1[l], w_in[l],
                    da_lambda[l], da_subln[l], gm_ln_g[l], gm_ln_b[l], gm_ws[l], gm_bs[l], dn_conv[l],
                    dn_a_log[l], dn_dt_bias[l], dn_norm[l], b_gate[l], w_branch[l], w_out[l], norm2[l],
                    w_router[l], b_router[l], w1_all, b_e1[l], w_e2[l], b_e2[l])
    return _final_norm(xu, norm_f, seq)
```

```python
import functools
import math

import jax
import jax.numpy as jnp
from jax import lax
from jax.experimental import pallas as pl
from jax.experimental.pallas import tpu as pltpu

F32 = jnp.float32
BF16 = jnp.bfloat16

D_MODEL = 2048
GRID_W = 64
RMS_EPS = 1e-6
BRANCH_W = D_MODEL // 2
N_BRANCH = 3
DA_HD = 64
DA_HEADS = BRANCH_W // (2 * DA_HD)
ROPE_THETA = 10000.0
ROPE_PAIRS_AXIS = DA_HD // 4
GM_CHUNK = 128
GM_GW = 128
GM_GROUPS = BRANCH_W // GM_GW
DN_HD = 128
DN_HEADS = BRANCH_W // DN_HD
DN_CHUNK = 64
DN_CONV = 5
N_EXPERTS = 32
TOP_K = 4
EXPERT_FF = D_MODEL // 2
SWIGLU_LIMIT = 7.0
SWIGLU_ALPHA = 1.702

LANES = 128
VMEM_LIMIT = 56 * 1024 * 1024

OFF_DA_Q = 0
OFF_DA_K = 1024
OFF_DA_V = 2048
OFF_GM_U = 3072
OFF_GM_V = 4096
OFF_DN_Q = 5120
OFF_DN_K = 6144
OFF_DN_V = 7168
OFF_DN_Z = 8192
OFF_SMALL = 9216
OFF_GATE = 9248
N_MAIN = 9216 + N_BRANCH * D_MODEL
OFF_GATE_MAIN = 9216

MOE_TM = 512
MOE_PARTS = 1


def _cparams(sem):
    return pltpu.CompilerParams(dimension_semantics=sem, vmem_limit_bytes=VMEM_LIMIT)


def _pick(n, cands):
    for c in cands:
        if n % c == 0:
            return c
    raise ValueError(f"no tile for {n} in {cands}")


def _sigmoid(x):
    return jax.nn.sigmoid(x)


def _silu(x):
    return x * _sigmoid(x)


def _gelu_tanh(x):
    return x * (0.5 * (1.0 + jnp.tanh(0.7978845608028654 * (x + 0.044715 * (x * x * x)))))


def _bdot(a, b):
    return jnp.dot(a.astype(BF16), b.astype(BF16), preferred_element_type=F32)


def _split(a):
    hi = a.astype(BF16)
    lo = (a - hi.astype(F32)).astype(BF16)
    return hi, lo


def _dot3(a, b):
    ah, al = _split(a)
    bh, bl = _split(b)
    return (jnp.dot(ah, bh, preferred_element_type=F32)
            + (jnp.dot(al, bh, preferred_element_type=F32)
               + jnp.dot(ah, bl, preferred_element_type=F32)))


def _mod_kernel(c_ref, w_ref, b_ref, o_ref):
    s = _silu(c_ref[...])
    o_ref[...] = _bdot(s, w_ref[...]) + b_ref[...]


def _modulation(cond, w_mod, b_mod):
    nl, d, n6 = w_mod.shape
    r = cond.shape[0]
    tn = 1024
    return pl.pallas_call(
        _mod_kernel,
        grid=(nl, n6 // tn),
        in_specs=[
            pl.BlockSpec((r, d), lambda l, j: (0, 0)),
            pl.BlockSpec((None, d, tn), lambda l, j: (l, 0, j)),
            pl.BlockSpec((None, 1, tn), lambda l, j: (l, 0, j)),
        ],
        out_specs=pl.BlockSpec((None, r, tn), lambda l, j: (l, 0, j)),
        out_shape=jax.ShapeDtypeStruct((nl, r, n6), F32),
        compiler_params=_cparams(("arbitrary", "arbitrary")),
        name="modulation",
    )(cond, w_mod, b_mod.reshape(nl, 1, n6))


def _adaln_tile(x, g, shl, scl, shc, scc, row0, seq):
    tm = x.shape[0]
    y = x * lax.rsqrt(jnp.mean(x * x, axis=-1, keepdims=True) + RMS_EPS) * g
    row = row0 + lax.broadcasted_iota(jnp.int32, (tm, 1), 0)
    is_ctx = row >= seq
    scale = jnp.where(is_ctx, scc, scl)
    shift = jnp.where(is_ctx, shc, shl)
    return y * (1.0 + scale) + shift


def _mod_specs(d):
    return [
        pl.BlockSpec((1, d), lambda b, i: (0, 0)),
        pl.BlockSpec((None, 1, d), lambda b, i: (b, 0, 0)),
        pl.BlockSpec((None, 1, d), lambda b, i: (b, 0, 0)),
        pl.BlockSpec((1, d), lambda b, i: (0, 0)),
        pl.BlockSpec((1, d), lambda b, i: (0, 0)),
    ]


def _adaln_router_kernel(x_ref, g_ref, shl_ref, scl_ref, shc_ref, scc_ref, wr_ref, br_ref,
                         h_ref, idx_ref, wt_ref, *, tm, seq):
    h = _adaln_tile(x_ref[...], g_ref[...], shl_ref[...], scl_ref[...], shc_ref[...], scc_ref[...],
                    pl.program_id(1) * tm, seq)
    h_ref[...] = h.astype(h_ref.dtype)
    logits = _dot3(h, wr_ref[...]) + br_ref[...]
    lane = lax.broadcasted_iota(jnp.int32, logits.shape, 1).astype(F32)
    vals, idxs = [], []
    cur = logits
    for _ in range(TOP_K):
        m = jnp.max(cur, axis=-1, keepdims=True)
        am = jnp.min(jnp.where(cur == m, lane, float(LANES)), axis=-1, keepdims=True)
        vals.append(m)
        idxs.append(am)
        cur = jnp.where(lane == am, -jnp.inf, cur)
    es = [jnp.exp(v - vals[0]) for v in vals]
    tot = es[0] + es[1] + es[2] + es[3]
    wt = jnp.zeros(logits.shape, F32)
    ix = jnp.zeros(logits.shape, F32)
    for k in range(TOP_K):
        wt = jnp.where(lane == k, es[k] / tot, wt)
        ix = jnp.where(lane == k, idxs[k], ix)
    idx_ref[...] = ix.astype(jnp.int32)
    wt_ref[...] = wt


def _adaln_router(xu, g, shl, scl, shc, scc, w_router, b_router, seq, rows):
    bsz, t, d = xu.shape
    tm = _pick(rows, (768, 512, 384, 256, 128))
    wr = jnp.zeros((d, LANES), F32).at[:, :N_EXPERTS].set(w_router)
    br = jnp.full((1, LANES), -1e30, F32).at[0, :N_EXPERTS].set(b_router)
    return pl.pallas_call(
        functools.partial(_adaln_router_kernel, tm=tm, seq=seq),
        grid=(bsz, rows // tm),
        in_specs=[pl.BlockSpec((None, tm, d), lambda b, i: (b, i, 0))] + _mod_specs(d) + [
            pl.BlockSpec((d, LANES), lambda b, i: (0, 0)),
            pl.BlockSpec((1, LANES), lambda b, i: (0, 0)),
        ],
        out_specs=[
            pl.BlockSpec((None, tm, d), lambda b, i: (b, i, 0)),
            pl.BlockSpec((None, tm, LANES), lambda b, i: (b, i, 0)),
            pl.BlockSpec((None, tm, LANES), lambda b, i: (b, i, 0)),
        ],
        out_shape=[
            jax.ShapeDtypeStruct((bsz, rows, d), BF16),
            jax.ShapeDtypeStruct((bsz, rows, LANES), jnp.int32),
            jax.ShapeDtypeStruct((bsz, rows, LANES), F32),
        ],
        compiler_params=_cparams(("parallel", "parallel")),
        name="adaln_router",
    )(xu, g, shl, scl, shc, scc, wr, br)


def _inproj_kernel(x_ref, g_ref, shl_ref, scl_ref, shc_ref, scc_ref, w_ref, ws_ref, cos_ref, sin_ref,
                   o_ref, small_ref, h_ref, *, tm, tn, seq, tiles_per_batch, n_rope_tiles, n_q_tiles):
    j = pl.program_id(1)

    @pl.when(j == 0)
    def _():
        h = _adaln_tile(x_ref[...], g_ref[...], shl_ref[...], scl_ref[...], shc_ref[...], scc_ref[...],
                        (pl.program_id(0) % tiles_per_batch) * tm, seq)
        h_ref[...] = h.astype(h_ref.dtype)
        small_ref[...] = jnp.dot(h_ref[...], ws_ref[...], preferred_element_type=F32)

    acc = jnp.dot(h_ref[...], w_ref[...], preferred_element_type=F32)

    @pl.when(j >= n_rope_tiles)
    def _():
        o_ref[...] = acc.astype(o_ref.dtype)

    @pl.when(j < n_rope_tiles)
    def _():
        scale = jnp.where(j < n_q_tiles, DA_HD ** -0.5, 1.0).astype(F32)
        cos = cos_ref[...] * scale
        sin = sin_ref[...] * scale
        lane = lax.broadcasted_iota(jnp.int32, cos.shape, 1)
        first = (lane % DA_HD) < (DA_HD // 2)
        for c in range(tn // LANES):
            a = acc[:, c * LANES:(c + 1) * LANES]
            sw = jnp.where(first, pltpu.roll(a, LANES - DA_HD // 2, 1), pltpu.roll(a, DA_HD // 2, 1))
            o_ref[:, c * LANES:(c + 1) * LANES] = (a * cos + sw * sin).astype(o_ref.dtype)


def _in_proj(xu, g, shl, scl, shc, scc, w_main, w_small, cos_t, sin_t, seq):
    bsz, t, d = xu.shape
    n = w_main.shape[1]
    tm = _pick(t, (2304, 1152, 768, 384, 256, 128))
    tpb = t // tm
    tn = 512
    row = lambda i, j: (i // tpb, i % tpb, 0)
    per_batch = pl.BlockSpec((None, 1, d), lambda i, j: (i // tpb, 0, 0))
    const = lambda shape: pl.BlockSpec(shape, lambda i, j: (0, 0))
    return pl.pallas_call(
        functools.partial(_inproj_kernel, tm=tm, tn=tn, seq=seq, tiles_per_batch=tpb,
                          n_rope_tiles=OFF_DA_V // tn, n_q_tiles=OFF_DA_K // tn),
        grid=(bsz * tpb, n // tn),
        in_specs=[
            pl.BlockSpec((None, tm, d), row, pipeline_mode=pl.Buffered(1)),
            const((1, d)), per_batch, per_batch, const((1, d)), const((1, d)),
            pl.BlockSpec((d, tn), lambda i, j: (0, j)),
            const((d, LANES)),
            pl.BlockSpec((tm, LANES), lambda i, j: (i % tpb, 0)),
            pl.BlockSpec((tm, LANES), lambda i, j: (i % tpb, 0)),
        ],
        out_specs=[pl.BlockSpec((None, tm, tn), lambda i, j: (i // tpb, i % tpb, j)),
                   pl.BlockSpec((None, tm, LANES), row)],
        out_shape=[jax.ShapeDtypeStruct((bsz, t, n), BF16), jax.ShapeDtypeStruct((bsz, t, LANES), F32)],
        scratch_shapes=[pltpu.VMEM((tm, d), BF16)],
        compiler_params=_cparams(("parallel", "arbitrary")),
        name="in_proj",
    )(xu, g, shl, scl, shc, scc, w_main, w_small, cos_t, sin_t)


def _rope_tables(seq, t):
    rows = seq // GRID_W
    row = jnp.repeat(jnp.arange(rows, dtype=F32), GRID_W)
    col = jnp.tile(jnp.arange(GRID_W, dtype=F32), rows)
    inv = ROPE_THETA ** (-jnp.arange(ROPE_PAIRS_AXIS, dtype=F32) / ROPE_PAIRS_AXIS)
    ang = jnp.concatenate([row[:, None] * inv, col[:, None] * inv], axis=-1)
    cos, sin = jnp.cos(ang), jnp.sin(ang)
    cos_t = jnp.tile(cos, (1, LANES // (DA_HD // 2)))
    sin_t = jnp.tile(jnp.concatenate([-sin, sin], axis=-1), (1, LANES // DA_HD))
    pad = t - seq
    cos_t = jnp.concatenate([cos_t, jnp.ones((pad, LANES), F32)], axis=0)
    sin_t = jnp.concatenate([sin_t, jnp.zeros((pad, LANES), F32)], axis=0)
    return cos_t, sin_t


ATTN_ROW_GROUPS = 4
ATTN_HEADS_PER_STEP = 4


def _attn_kernel(lam_ref, g_ref, q_ref, k_ref, v_ref, o_ref, *, seq, tq, lam_init):
    qi = pl.program_id(2)
    nh = ATTN_HEADS_PER_STEP
    lp = lam_ref[...]
    l1 = jnp.sum(lp[0:1] * lp[1:2], axis=-1, keepdims=True)
    l2 = jnp.sum(lp[2:3] * lp[3:4], axis=-1, keepdims=True)
    lam = jnp.exp(l1) - jnp.exp(l2) + lam_init
    lane = lax.broadcasted_iota(jnp.int32, (tq, LANES), 1)
    qqs = []
    for h in range(nh):
        q = q_ref[:, h * LANES:(h + 1) * LANES].astype(F32)
        qqs.append(jnp.concatenate([jnp.where(lane < DA_HD, q, 0.0), jnp.where(lane >= DA_HD, q, 0.0)],
                                   axis=0).astype(BF16))

    def core(k_of, v_of):
        rs = 2 * tq // ATTN_ROW_GROUPS
        scores = [[lax.dot_general(qqs[h][i * rs:(i + 1) * rs], k_of(h), (((1,), (1,)), ((), ())),
                                   preferred_element_type=F32) for i in range(ATTN_ROW_GROUPS)] for h in range(nh)]
        for h in range(nh):
            outs = []
            for s in scores[h]:
                m = jnp.max(s, axis=-1, keepdims=True)
                p = jnp.exp(s - m)
                den = jnp.sum(p, axis=-1, keepdims=True)
                outs.append(jnp.dot(p.astype(BF16), v_of(h), preferred_element_type=F32) / den)
            o = jnp.concatenate(outs, axis=0)
            o = o[:tq] - lam * o[tq:]
            y = o * lax.rsqrt(jnp.mean(o * o, axis=-1, keepdims=True) + RMS_EPS) * g_ref[...]
            o_ref[:, h * LANES:(h + 1) * LANES] = (y * (1.0 - lam_init)).astype(o_ref.dtype)

    @pl.when(qi * tq < seq)
    def _():
        core(lambda h: k_ref[:, h * LANES:(h + 1) * LANES], lambda h: v_ref[:, h * LANES:(h + 1) * LANES])

    @pl.when(qi * tq >= seq)
    def _():
        core(lambda h: k_ref[seq:, h * LANES:(h + 1) * LANES], lambda h: v_ref[seq:, h * LANES:(h + 1) * LANES])


def _diff_attention(p, lam_params, subln_g, seq, rows, lam_init):
    bsz, t, _ = p.shape
    tq = _pick(math.gcd(seq, t - seq), (256, 128))
    nh = ATTN_HEADS_PER_STEP
    w = nh * LANES
    cq, ck, cv = OFF_DA_Q // w, OFF_DA_K // w, OFF_DA_V // w
    return pl.pallas_call(
        functools.partial(_attn_kernel, seq=seq, tq=tq, lam_init=lam_init),
        grid=(bsz, DA_HEADS // nh, rows // tq),
        in_specs=[
            pl.BlockSpec((4, DA_HD), lambda b, h, i: (0, 0)),
            pl.BlockSpec((1, 2 * DA_HD), lambda b, h, i: (0, 0)),
            pl.BlockSpec((None, tq, w), lambda b, h, i: (b, i, cq + h)),
            pl.BlockSpec((None, t, w), lambda b, h, i: (b, 0, ck + h)),
            pl.BlockSpec((None, t, w), lambda b, h, i: (b, 0, cv + h)),
        ],
        out_specs=pl.BlockSpec((None, tq, w), lambda b, h, i: (b, i, h)),
        out_shape=jax.ShapeDtypeStruct((bsz, rows, BRANCH_W), BF16),
        compiler_params=_cparams(("parallel", "parallel", "arbitrary")),
        name="diff_attention",
    )(lam_params, subln_g.reshape(1, -1), p, p, p)


def _gmlp_kernel(u_ref, v_ref, lng_ref, lnb_ref, ws_ref, bs_ref, o_ref, *, nchunks):
    for c in range(nchunks):
        r0 = c * GM_CHUNK
        u = _gelu_tanh(u_ref[r0:r0 + GM_CHUNK, :].astype(F32))
        v = _gelu_tanh(v_ref[r0:r0 + GM_CHUNK, :].astype(F32))
        xc = v - jnp.mean(v, axis=-1, keepdims=True)
        var = jnp.mean(xc * xc, axis=-1, keepdims=True)
        vn = (xc * lax.rsqrt(var + RMS_EPS) * lng_ref[...] + lnb_ref[...]).astype(BF16)
        for g in range(GM_GROUPS):
            cs = slice(g * GM_GW, (g + 1) * GM_GW)
            s = jnp.dot(ws_ref[g], vn[:, cs], preferred_element_type=F32) + bs_ref[g]
            o_ref[r0:r0 + GM_CHUNK, cs] = (u[:, cs] * s).astype(o_ref.dtype)


def _spatial_gating(p, ln_g, ln_b, ws, bs, rows):
    bsz, t, _ = p.shape
    tm = _pick(rows, (768, 512, 384, 256, 128))
    cu, cv = OFF_GM_U // BRANCH_W, OFF_GM_V // BRANCH_W
    bs_b = jnp.broadcast_to(bs[:, :, None], (GM_GROUPS, GM_CHUNK, GM_GW)).astype(F32)
    return pl.pallas_call(
        functools.partial(_gmlp_kernel, nchunks=tm // GM_CHUNK),
        grid=(bsz, rows // tm),
        in_specs=[
            pl.BlockSpec((None, tm, BRANCH_W), lambda b, i: (b, i, cu)),
            pl.BlockSpec((None, tm, BRANCH_W), lambda b, i: (b, i, cv)),
            pl.BlockSpec((1, BRANCH_W), lambda b, i: (0, 0)),
            pl.BlockSpec((1, BRANCH_W), lambda b, i: (0, 0)),
            pl.BlockSpec((GM_GROUPS, GM_CHUNK, GM_CHUNK), lambda b, i: (0, 0, 0)),
            pl.BlockSpec((GM_GROUPS, GM_CHUNK, GM_GW), lambda b, i: (0, 0, 0)),
        ],
        out_specs=pl.BlockSpec((None, tm, BRANCH_W), lambda b, i: (b, i, 0)),
        out_shape=jax.ShapeDtypeStruct((bsz, rows, BRANCH_W), BF16),
        compiler_params=_cparams(("parallel", "parallel")),
        name="spatial_gating",
    )(p, p, ln_g.reshape(1, -1), ln_b.reshape(1, -1), ws.astype(BF16), bs_b)


DN_BASE = 8
DN_PREP_GROUPS = (9, 6, 4, 3, 2, 1)
DN_HEAD_GROUP = 2


def _dn_kernel(alog_ref, dtb_ref, q_ref, k_ref, v_ref, z_ref, sm_ref, ar_ref, cw_ref, ng_ref, o_ref,
               qn_ref, kn_ref, vn_ref, rowg_ref, ac_ref, b_ref, d_ref, cd_ref, oacc_ref, st_ref,
               *, seq, t):
    hg = DN_HEAD_GROUP
    hblk = pl.program_id(1)
    nc = t // DN_CHUNK
    n_lat = seq // DN_CHUNK
    n_ctx = nc - n_lat
    hw = DN_HD
    cw = DN_CHUNK

    row = lax.broadcasted_iota(jnp.int32, (t, 1), 0)
    seg_lo = jnp.where(row < seq, 0, seq)
    seg_hi = jnp.where(row < seq, seq, t)

    def conv_silu(x_ref, w):
        x = x_ref[...].astype(F32)
        acc = x * w[DN_CONV // 2:DN_CONV // 2 + 1, :]
        for s in (-2, -1, 1, 2):
            xs = pltpu.roll(x, (-s) % t, 0)
            rs = row + s
            ok = (rs >= seg_lo) & (rs < seg_hi)
            acc = acc + jnp.where(ok, xs, 0.0) * w[DN_CONV // 2 + s:DN_CONV // 2 + s + 1, :]
        return _silu(acc)

    def l2n(x):
        return x * lax.rsqrt(jnp.sum(x * x, axis=-1, keepdims=True) + RMS_EPS)

    qc = conv_silu(q_ref, cw_ref[0])
    kc = conv_silu(k_ref, cw_ref[1])
    vn_ref[...] = conv_silu(v_ref, cw_ref[2])
    for j in range(hg):
        cs = slice(j * hw, (j + 1) * hw)
        qn_ref[:, cs] = l2n(qc[:, cs]) * (DN_HD ** -0.5)
        kn_ref[:, cs] = l2n(kc[:, cs])

    def softplus(x):
        return jnp.maximum(x, 0.0) + jnp.log1p(jnp.exp(-jnp.abs(x)))

    rw = 2 * hg * cw
    lane_r = lax.broadcasted_iota(jnp.int32, (1, rw), 1)
    chain_r = lane_r // cw
    pos_r = lane_r % cw
    alog_r = jnp.zeros((1, rw), F32)
    dt_r = jnp.zeros((1, rw), F32)
    for d in range(2):
        for j in range(hg):
            alog_r = jnp.where(chain_r == hg * d + j, alog_ref[d, hblk * hg + j], alog_r)
            dt_r = jnp.where(chain_r == hg * d + j, dtb_ref[d, hblk * hg + j], dt_r)
    g_all = -jnp.exp(alog_r) * softplus(ar_ref[...].reshape(nc * 8, rw) + dt_r)
    pre = g_all
    suf = g_all
    sh = 1
    while sh < cw:
        pre = pre + jnp.where(pos_r >= sh, pltpu.roll(pre, sh, 1), 0.0)
        suf = suf + jnp.where(pos_r < cw - sh, pltpu.roll(suf, rw - sh, 1), 0.0)
        sh *= 2
    run = jnp.where(lane_r >= hg * cw, suf, pre).reshape(nc, 8, rw)
    tot = (pre + suf - g_all).reshape(nc, 8, rw)
    sub = lax.broadcasted_iota(jnp.int32, (nc, 8, rw), 1)
    both = jnp.where(sub == 0, run, tot)
    for d in range(2):
        rowg_ref[d] = both[:, :, d * hg * cw:(d + 1) * hg * cw]

    st_ref[...] = jnp.zeros(st_ref.shape, F32)

    pshape = (cw, hg * cw)
    ii = lax.broadcasted_iota(jnp.int32, pshape, 0)
    lp = lax.broadcasted_iota(jnp.int32, pshape, 1)
    jl = lp % cw
    left = lp < cw
    diag = ii == jl
    eye_p = jnp.where(diag, 1.0, 0.0).astype(F32)
    blk_base = (ii // DN_BASE) == (jl // DN_BASE)
    incl = [ii >= jl, ii <= jl]
    strict = [ii > jl, ii < jl]
    half = [jnp.where(left, 1.0, 0.0).astype(BF16), jnp.where(left, 0.0, 1.0).astype(BF16)]
    left_sq = lax.broadcasted_iota(jnp.int32, (LANES, LANES), 1) < cw

    def blockdiag(b16):
        return jnp.concatenate([b16 * half[0], b16 * half[1]], axis=0)

    def pprod(a, b):
        return jnp.dot(a.astype(BF16), blockdiag(b.astype(BF16)), preferred_element_type=F32)

    def tri_inverse(lmats):
        ms = [jnp.where(blk_base, -l, 0.0) for l in lmats]
        xs = [eye_p + m for m in ms]
        pws = [pprod(m, m) for m in ms]
        span = 4
        while span <= DN_BASE:
            tts = [pprod(jnp.concatenate([x, pw], axis=0), pw) for x, pw in zip(xs, pws)]
            xs = [x + tt[:cw] for x, tt in zip(xs, tts)]
            pws = [tt[cw:] for tt in tts]
            span *= 2
        bs = DN_BASE
        while bs < cw:
            off = ((ii // (2 * bs)) == (jl // (2 * bs))) & ((ii // bs) != (jl // bs))
            cmats = [jnp.where(off, l, 0.0) for l in lmats]
            ys = [pprod(x, c) for x, c in zip(xs, cmats)]
            zs = [pprod(y, x) for y, x in zip(ys, xs)]
            xs = [x - z for x, z in zip(xs, zs)]
            bs *= 2
        return xs

    def prep_load(c):
        rows = pl.ds(pl.multiple_of(c * cw, cw), cw)
        kk = [kn_ref[rows, j * hw:(j + 1) * hw] for j in range(hg)]
        qq = [qn_ref[rows, j * hw:(j + 1) * hw] for j in range(hg)]
        vv = [vn_ref[rows, j * hw:(j + 1) * hw] for j in range(hg)]
        return kk, qq, vv, sm_ref[rows, :], [rowg_ref[d, c] for d in range(2)]

    def prep_compute(loaded):
        n = len(loaded)
        gram, qk, ktp = [], [], []
        for kk, qq, vv, sm, rgs in loaded:
            gq, kt = [], []
            for j in range(hg):
                kb = kk[j].astype(BF16)
                gq.append(lax.dot_general(jnp.concatenate([kb, qq[j].astype(BF16)], axis=0),
                                          jnp.concatenate([kb, kb], axis=0), (((1,), (1,)), ((), ())),
                                          preferred_element_type=F32))
                kt.append(jnp.concatenate([kk[j], kk[j]], axis=0).T)
            pair = jnp.where(left_sq, gq[0], gq[1])
            gram.append(pair[:cw])
            qk.append(pair[cw:])
            ktp.append(jnp.where(left_sq, kt[0], kt[1]))
        pre = []
        for ci, (kk, qq, vv, sm, rgs) in enumerate(loaded):
            lane_c = lax.broadcasted_iota(jnp.int32, sm.shape, 1)
            for d in range(2):
                bcol = [_sigmoid(jnp.sum(jnp.where(lane_c == d * DN_HEADS + hblk * hg + j, sm, 0.0),
                                         axis=1, keepdims=True)) for j in range(hg)]
                gc_row, g_tot = rgs[d][0:1, :], rgs[d][1:2, :]
                gdiag = jnp.where(diag, gc_row, 0.0)
                gcol = [jnp.sum(jnp.where(left, gdiag, 0.0), axis=1, keepdims=True),
                        jnp.sum(jnp.where(left, 0.0, gdiag), axis=1, keepdims=True)]
                gc = jnp.where(left, gcol[0], gcol[1])
                beta = jnp.where(left, bcol[0], bcol[1])
                dec = jnp.exp(jnp.where(incl[d], gc - gc_row, -jnp.inf))
                lmat = jnp.where(strict[d], beta * gram[ci] * dec, 0.0)
                pre.append((ci, d, bcol, gcol, gc_row, g_tot, dec, lmat))
        tinvs = tri_inverse([p[-1] for p in pre])
        rhss, egs = [], []
        for ci, d, bcol, gcol, gc_row, g_tot, dec, lmat in pre:
            kk, qq, vv = loaded[ci][:3]
            eg = [jnp.exp(gcol[j]) for j in range(hg)]
            egs.append(eg)
            rhss.append(jnp.concatenate(
                [jnp.concatenate([vv[j] * bcol[j], kk[j] * (bcol[j] * eg[j])], axis=1) for j in range(hg)],
                axis=0).astype(BF16))
        sols = []
        for tinv, rhs in zip(tinvs, rhss):
            t16 = tinv.astype(BF16)
            sols.append(jnp.dot(jnp.concatenate([t16 * half[0], t16 * half[1]], axis=0), rhs,
                                preferred_element_type=F32))
        xs = []
        for (ci, d, bcol, gcol, gc_row, g_tot, dec, lmat), sol in zip(pre, sols):
            q_intra = jnp.where(incl[d], qk[ci] * dec, 0.0)
            qk2 = jnp.concatenate([q_intra, ktp[ci] * jnp.exp(g_tot - gc_row)], axis=0).astype(BF16)
            s16 = sol.astype(BF16)
            zero = jnp.zeros((cw, 2 * hw), BF16)
            bd = jnp.concatenate([jnp.concatenate([s16[:cw], zero], axis=1),
                                  jnp.concatenate([zero, s16[cw:]], axis=1)], axis=0)
            xs.append(jnp.dot(qk2, bd, preferred_element_type=F32))
        outs = [[None, None] for _ in range(n)]
        for (ci, d, bcol, gcol, gc_row, g_tot, dec, lmat), x, eg in zip(pre, xs, egs):
            qq = loaded[ci][1]
            e_tot = jnp.exp(g_tot)
            outs[ci][d] = dict(
                ac=[jnp.concatenate([x[cw:, j * 2 * hw + hw:(j + 1) * 2 * hw],
                                     qq[j] * eg[j] - x[:cw, j * 2 * hw + hw:(j + 1) * 2 * hw]], axis=0).astype(BF16)
                    for j in range(hg)],
                b=[x[cw:, j * 2 * hw:j * 2 * hw + hw].astype(BF16) for j in range(hg)],
                dd=[x[:cw, j * 2 * hw:j * 2 * hw + hw].astype(BF16) for j in range(hg)],
                cd=[jnp.broadcast_to(e_tot[:, j * cw:j * cw + 1], (1, hw)) for j in range(hg)])
        return outs

    def prep_store(c, outs):
        rows = pl.ds(pl.multiple_of(c * cw, cw), cw)
        for d in range(2):
            for j in range(hg):
                ac_ref[hg * d + j, c] = outs[d]["ac"][j]
                b_ref[hg * d + j, c] = outs[d]["b"][j]
                d_ref[hg * d + j, rows, :] = outs[d]["dd"][j]
                cd_ref[hg * d + j, c] = outs[d]["cd"][j]

    group = _pick(nc, DN_PREP_GROUPS)

    def prep_body(g, carry):
        cs = [g * group + cc for cc in range(group)]
        loaded = [prep_load(c) for c in cs]
        outs = prep_compute(loaded)
        for c, o in zip(cs, outs):
            prep_store(c, o)
        return carry

    lax.fori_loop(0, nc // group, prep_body, 0)

    def scan_body(i, carry):
        cf = jnp.where(i < n_ctx, i + n_lat, i - n_ctx)
        cb = nc - 1 - i
        dirs = ((0, cf), (1, cb))
        rows = [pl.ds(pl.multiple_of(c * cw, cw), cw) for _, c in dirs]
        state = [st_ref[s] for s in range(2 * hg)]
        ac = [ac_ref[hg * d + j, c] for d, c in dirs for j in range(hg)]
        bb = [b_ref[hg * d + j, c] for d, c in dirs for j in range(hg)]
        dd = [d_ref[hg * d + j, rows[d], :] for d, _ in dirs for j in range(hg)]
        cd = [cd_ref[hg * d + j, c] for d, c in dirs for j in range(hg)]
        rs = [jnp.dot(ac[s], state[s].astype(BF16), preferred_element_type=F32) for s in range(2 * hg)]
        o_new = [jnp.concatenate([rs[hg * d + j][hw:] + dd[hg * d + j].astype(F32) for j in range(hg)], axis=1)
                 for d in range(2)]
        st_new = [state[s] * cd[s] - rs[s][:hw] + bb[s].astype(F32) for s in range(2 * hg)]
        for d, _ in dirs:
            oacc_ref[d, rows[d], :] = o_new[d]
        for s in range(2 * hg):
            st_ref[s] = st_new[s]
        return carry

    lax.fori_loop(0, nc, scan_body, 0)

    o = oacc_ref[0] + oacc_ref[1]
    for j in range(hg):
        cs = slice(j * hw, (j + 1) * hw)
        oj = o[:, cs]
        y = oj * lax.rsqrt(jnp.mean(oj * oj, axis=-1, keepdims=True) + RMS_EPS) * ng_ref[...]
        o_ref[:, cs] = (y * _silu(z_ref[:, cs].astype(F32))).astype(o_ref.dtype)


def _gated_deltanet(p, small, conv_w, a_log, dt_bias, norm_g, seq):
    bsz, t, _ = p.shape
    nc = t // DN_CHUNK
    hg = DN_HEAD_GROUP
    assert hg == 2
    w = hg * DN_HD
    rw = 2 * hg * DN_CHUNK
    a = small[..., 2 * DN_HEADS:4 * DN_HEADS].reshape(bsz, nc, DN_CHUNK, 2, DN_HEADS // hg, hg)
    a_row = jnp.transpose(a, (0, 4, 1, 3, 5, 2)).reshape(bsz, DN_HEADS // hg, nc, 1, rw)
    a_row = jnp.broadcast_to(a_row, (bsz, DN_HEADS // hg, nc, 8, rw))
    cq, ck, cv, cz = (OFF_DN_Q // w, OFF_DN_K // w, OFF_DN_V // w, OFF_DN_Z // w)
    slab = lambda c0: pl.BlockSpec((None, t, w), lambda b, h: (b, 0, c0 + h))
    smem = pl.BlockSpec(memory_space=pltpu.SMEM)
    return pl.pallas_call(
        functools.partial(_dn_kernel, seq=seq, t=t),
        grid=(bsz, DN_HEADS // hg),
        in_specs=[
            smem, smem,
            slab(cq), slab(ck), slab(cv), slab(cz),
            pl.BlockSpec((None, t, LANES), lambda b, h: (b, 0, 0)),
            pl.BlockSpec((None, None, nc, 8, rw), lambda b, h: (b, h, 0, 0, 0)),
            pl.BlockSpec((3, DN_CONV, w), lambda b, h: (0, 0, h)),
            pl.BlockSpec((1, DN_HD), lambda b, h: (0, 0)),
        ],
        out_specs=pl.BlockSpec((None, t, w), lambda b, h: (b, 0, h)),
        out_shape=jax.ShapeDtypeStruct((bsz, t, BRANCH_W), BF16),
        scratch_shapes=[
            pltpu.VMEM((t, w), F32), pltpu.VMEM((t, w), F32), pltpu.VMEM((t, w), F32),
            pltpu.VMEM((2, nc, 8, hg * DN_CHUNK), F32),
            pltpu.VMEM((2 * hg, nc, DN_HD + DN_CHUNK, DN_HD), BF16),
            pltpu.VMEM((2 * hg, nc, DN_HD, DN_HD), BF16),
            pltpu.VMEM((2 * hg, t, DN_HD), BF16),
            pltpu.VMEM((2 * hg, nc, 1, DN_HD), F32),
            pltpu.VMEM((2, t, w), F32),
            pltpu.VMEM((2 * hg, DN_HD, DN_HD), F32),
        ],
        compiler_params=_cparams(("parallel", "parallel")),
        name="gated_deltanet",
    )(a_log, dt_bias, p, p, p, p, small, a_row, conv_w, norm_g.reshape(1, -1))


def _merge_kernel(ya_ref, yg_ref, yd_ref, ga_ref, gg_ref, gd_ref, wb_ref, bg_ref, o_ref):
    acc = None
    for i, (y_ref, g_ref) in enumerate(((ya_ref, ga_ref), (yg_ref, gg_ref), (yd_ref, gd_ref))):
        gate = _sigmoid(g_ref[...].astype(F32) + bg_ref[i])
        term = gate * jnp.dot(y_ref[...], wb_ref[i], preferred_element_type=F32)
        acc = term if acc is None else acc + term
    o_ref[...] = acc.astype(o_ref.dtype)


def _merge(ya, yg, yd, p, w_branch, b_gate, rows):
    bsz = p.shape[0]
    d = D_MODEL
    tm = _pick(rows, (768, 512, 384, 256, 128))
    tn = 1024
    g0 = OFF_GATE_MAIN // tn
    y_spec = pl.BlockSpec((None, tm, BRANCH_W), lambda b, i, j: (b, i, 0))
    gate_spec = lambda k: pl.BlockSpec((None, tm, tn), lambda b, i, j: (b, i, g0 + k * (d // tn) + j))
    return pl.pallas_call(
        _merge_kernel,
        grid=(bsz, rows // tm, d // tn),
        in_specs=[y_spec, y_spec, y_spec, gate_spec(0), gate_spec(1), gate_spec(2),
                  pl.BlockSpec((N_BRANCH, BRANCH_W, tn), lambda b, i, j: (0, 0, j)),
                  pl.BlockSpec((N_BRANCH, 1, tn), lambda b, i, j: (0, 0, j))],
        out_specs=pl.BlockSpec((None, tm, tn), lambda b, i, j: (b, i, j)),
        out_shape=jax.ShapeDtypeStruct((bsz, rows, d), BF16),
        compiler_params=_cparams(("parallel", "parallel", "arbitrary")),
        name="merge_branches",
    )(ya, yg, yd, p, p, p, w_branch, b_gate.reshape(N_BRANCH, 1, d))


def _outproj_kernel(z_ref, w_ref, x_ref, gl_ref, gc_ref, o_ref, *, tm, seq):
    acc = jnp.dot(z_ref[...], w_ref[...], preferred_element_type=F32)
    row = pl.program_id(1) * tm + lax.broadcasted_iota(jnp.int32, (tm, 1), 0)
    gate = jnp.where(row >= seq, gc_ref[...], gl_ref[...])
    o_ref[...] = x_ref[...] + gate * acc


def _out_proj_residual(z, w_out, xu, gate_l, gate_c, seq, rows):
    bsz, t, d = xu.shape
    tm = _pick(rows, (768, 512, 384, 256, 128))
    tn = 1024
    return pl.pallas_call(
        functools.partial(_outproj_kernel, tm=tm, seq=seq),
        grid=(bsz, rows // tm, d // tn),
        in_specs=[
            pl.BlockSpec((None, tm, d), lambda b, i, j: (b, i, 0)),
            pl.BlockSpec((d, tn), lambda b, i, j: (0, j)),
            pl.BlockSpec((None, tm, tn), lambda b, i, j: (b, i, j)),
            pl.BlockSpec((None, 1, tn), lambda b, i, j: (b, 0, j)),
            pl.BlockSpec((1, tn), lambda b, i, j: (0, j)),
        ],
        out_specs=pl.BlockSpec((None, tm, tn), lambda b, i, j: (b, i, j)),
        out_shape=jax.ShapeDtypeStruct((bsz, rows, d), F32),
        compiler_params=_cparams(("parallel", "parallel", "arbitrary")),
        name="out_proj_residual",
    )(z, w_out, xu, gate_l, gate_c)


W1_BLOCK = 2 * LANES


def _w1_prep_kernel(w_ref, perm_ref, o_ref):
    w = w_ref[...].astype(BF16)
    for blk in range(w.shape[1] // W1_BLOCK):
        cs = slice(blk * W1_BLOCK, (blk + 1) * W1_BLOCK)
        o_ref[:, cs] = jnp.dot(w[:, cs], perm_ref[...], preferred_element_type=F32).astype(o_ref.dtype)


def _w1_prep(w_e1):
    nl, ne, d, n = w_e1.shape
    tk = 1024
    j = jnp.arange(W1_BLOCK)
    src = jnp.where(j < LANES, 2 * j, 2 * (j - LANES) + 1)
    perm = (jnp.arange(W1_BLOCK)[:, None] == src[None, :]).astype(BF16)
    return pl.pallas_call(
        _w1_prep_kernel,
        grid=(nl * ne, d // tk),
        in_specs=[pl.BlockSpec((None, tk, n), lambda e, k: (e, k, 0)),
                  pl.BlockSpec((W1_BLOCK, W1_BLOCK), lambda e, k: (0, 0))],
        out_specs=pl.BlockSpec((None, tk, n), lambda e, k: (e, k, 0)),
        out_shape=jax.ShapeDtypeStruct((nl * ne, d, n), BF16),
        compiler_params=_cparams(("parallel", "parallel")),
        name="expert_w1_prep",
    )(w_e1.reshape(nl * ne, d, n), perm)


def _regroup_bias(b_e1):
    ne, n = b_e1.shape
    return jnp.transpose(b_e1.reshape(ne, n // W1_BLOCK, LANES, 2), (0, 1, 3, 2)).reshape(ne, 1, n)


def _expert_kernel(be_ref, bv_ref, x_ref, w1_ref, b1_ref, w2_ref, b2_ref, o_ref, hid_ref):
    i = pl.program_id(0)

    @pl.when(bv_ref[i] > 0)
    def _():
        hgl = jnp.dot(x_ref[...], w1_ref[...], preferred_element_type=F32) + b1_ref[...]
        for blk in range(hgl.shape[1] // W1_BLOCK):
            xg = jnp.minimum(hgl[:, blk * W1_BLOCK:blk * W1_BLOCK + LANES], SWIGLU_LIMIT)
            xl = jnp.clip(hgl[:, blk * W1_BLOCK + LANES:(blk + 1) * W1_BLOCK], -SWIGLU_LIMIT, SWIGLU_LIMIT)
            hid_ref[:, blk * LANES:(blk + 1) * LANES] = (
                xg * _sigmoid(SWIGLU_ALPHA * xg) * (xl + 1.0)).astype(hid_ref.dtype)
        y = jnp.dot(hid_ref[...], w2_ref[...], preferred_element_type=F32) + b2_ref[...]
        o_ref[...] = y.astype(o_ref.dtype)


def _experts(xs, blk_e, blk_valid, w1, b1, w2, b2, e0):
    n_rows, d = xs.shape
    tm = MOE_TM
    ff = EXPERT_FF
    grid_spec = pltpu.PrefetchScalarGridSpec(
        num_scalar_prefetch=2,
        grid=(n_rows // tm,),
        in_specs=[
            pl.BlockSpec((tm, d), lambda i, be, bv: (i, 0)),
            pl.BlockSpec((None, d, 2 * ff), lambda i, be, bv: (e0 + be[i], 0, 0)),
            pl.BlockSpec((None, 1, 2 * ff), lambda i, be, bv: (be[i], 0, 0)),
            pl.BlockSpec((None, ff, d), lambda i, be, bv: (be[i], 0, 0)),
            pl.BlockSpec((None, 1, d), lambda i, be, bv: (be[i], 0, 0)),
        ],
        out_specs=pl.BlockSpec((tm, d), lambda i, be, bv: (i, 0)),
        scratch_shapes=[pltpu.VMEM((tm, ff), BF16)],
    )
    return pl.pallas_call(
        _expert_kernel,
        grid_spec=grid_spec,
        out_shape=jax.ShapeDtypeStruct((n_rows, d), BF16),
        compiler_params=_cparams(("arbitrary",)),
        name="moe_experts",
    )(blk_e, blk_valid, xs, w1, b1, w2, b2)


def _moe(h2, top_i, w1, b1, w2, b2, e0):
    n_tok, d = h2.shape
    tm = MOE_TM
    n_assign = n_tok * TOP_K
    flat_e = top_i.reshape(n_assign)
    order = jnp.argsort(flat_e).astype(jnp.int32)
    rank = jnp.argsort(order).astype(jnp.int32)
    onehot = flat_e[:, None] == jnp.arange(N_EXPERTS, dtype=flat_e.dtype)[None, :]
    counts = jnp.sum(onehot, axis=0, dtype=jnp.int32)
    padded = (counts + tm - 1) // tm * tm
    pad_end = jnp.cumsum(padded)
    start = jnp.cumsum(counts) - counts
    shift = (pad_end - padded) - start
    pos = rank + jnp.sum(jnp.where(onehot, shift[None, :], 0), axis=1)
    n_blocks = -(-n_assign // tm) + N_EXPERTS
    blk_start = jnp.arange(n_blocks, dtype=jnp.int32) * tm
    blk_valid = (blk_start < pad_end[-1]).astype(jnp.int32)
    blk_e = jnp.sum(blk_start[:, None] >= pad_end[None, :], axis=1, dtype=jnp.int32)
    last_e = jnp.sum(pad_end[-1] - 1 >= pad_end, dtype=jnp.int32)
    blk_e = jnp.where(blk_valid > 0, blk_e, last_e)
    row = blk_start[:, None] + jnp.arange(tm, dtype=jnp.int32)[None, :]
    src = row - shift[blk_e][:, None]
    lo = start[blk_e][:, None]
    live = (src >= lo) & (src < lo + counts[blk_e][:, None]) & (blk_valid[:, None] > 0)
    row_tok = jnp.where(live, order[jnp.clip(src, 0, n_assign - 1)] // TOP_K, row % n_tok).reshape(n_blocks * tm)
    xs = h2[row_tok]
    y = _experts(xs, blk_e, blk_valid, w1, b1, w2, b2, e0)
    return y[pos.reshape(n_tok, TOP_K).T.reshape(n_assign)].reshape(TOP_K, n_tok, d)


def _combine_kernel(y_ref, w_ref, x_ref, gl_ref, gc_ref, *rest, tm, seq):
    o_ref = rest[-1]
    w = w_ref[...]
    acc = y_ref[0].astype(F32) * w[:, 0:1]
    for k in range(1, TOP_K):
        acc = acc + y_ref[k].astype(F32) * w[:, k:k + 1]
    row = pl.program_id(1) * tm + lax.broadcasted_iota(jnp.int32, (tm, 1), 0)
    gate = jnp.where(row >= seq, gc_ref[...], gl_ref[...])
    o_ref[...] = x_ref[...] + gate * acc


def _moe_combine(yk, top_w, xu, gate_l, gate_c, seq, b0, prev):
    bsz, rows, d = xu.shape
    bp = yk.shape[1]
    tm = _pick(rows, (512, 384, 256, 128))
    in_specs = [
        pl.BlockSpec((TOP_K, None, tm, d), lambda b, i: (0, b, i, 0)),
        pl.BlockSpec((None, tm, LANES), lambda b, i: (b + b0, i, 0)),
        pl.BlockSpec((None, tm, d), lambda b, i: (b + b0, i, 0)),
        pl.BlockSpec((None, 1, d), lambda b, i: (b + b0, 0, 0)),
        pl.BlockSpec((1, d), lambda b, i: (0, 0)),
    ]
    args = [yk, top_w, xu, gate_l, gate_c]
    aliases = {}
    if prev is not None:
        in_specs.append(pl.BlockSpec(memory_space=pl.ANY))
        args.append(prev)
        aliases = {len(args) - 1: 0}
    return pl.pallas_call(
        functools.partial(_combine_kernel, tm=tm, seq=seq),
        grid=(bp, rows // tm),
        in_specs=in_specs,
        out_specs=pl.BlockSpec((None, tm, d), lambda b, i: (b + b0, i, 0)),
        out_shape=jax.ShapeDtypeStruct((bsz, rows, d), F32),
        input_output_aliases=aliases,
        compiler_params=_cparams(("parallel", "parallel")),
        name="moe_combine",
    )(*args)


def _final_kernel(x_ref, g_ref, o_ref):
    x = x_ref[...]
    o_ref[...] = x * lax.rsqrt(jnp.mean(x * x, axis=-1, keepdims=True) + RMS_EPS) * g_ref[...]


def _final_norm(xu, g, seq):
    bsz, t, d = xu.shape
    tm = _pick(seq, (512, 256, 128))
    return pl.pallas_call(
        _final_kernel,
        grid=(bsz, seq // tm),
        in_specs=[pl.BlockSpec((None, tm, d), lambda b, i: (b, i, 0)),
                  pl.BlockSpec((1, d), lambda b, i: (0, 0))],
        out_specs=pl.BlockSpec((None, tm, d), lambda b, i: (b, i, 0)),
        out_shape=jax.ShapeDtypeStruct((bsz, seq, d), F32),
        compiler_params=_cparams(("parallel", "parallel")),
        name="final_norm",
    )(xu, g.reshape(1, d))


def _layer(xu, mod_l, mod_c, seq, layer_idx, ctx_out, cos_t, sin_t, norm1, w_in, da_lambda, da_subln,
           gm_ln_g, gm_ln_b, gm_ws, gm_bs, dn_conv, dn_a_log, dn_dt_bias, dn_norm, b_gate, w_branch,
           w_out, norm2, w_router, b_router, w1_all, b_e1, w_e2, b_e2):
    bsz, t, d = xu.shape
    rows = t if ctx_out else seq
    lam_init = 0.8 - 0.6 * math.exp(-0.3 * layer_idx)
    ml = [mod_l[:, k:k + 1, :] for k in range(6)]
    mc = [mod_c[k:k + 1, :] for k in range(6)]

    w_main = jnp.concatenate([w_in[:, :OFF_SMALL], w_in[:, OFF_GATE:]], axis=1).astype(BF16)
    w_small = jnp.zeros((d, LANES), BF16).at[:, :OFF_GATE - OFF_SMALL].set(
        w_in[:, OFF_SMALL:OFF_GATE].astype(BF16))
    p, small = _in_proj(xu, norm1.reshape(1, d), ml[0], ml[1], mc[0], mc[1], w_main, w_small, cos_t, sin_t, seq)

    ya = _diff_attention(p, da_lambda, da_subln, seq, rows, lam_init)
    yg = _spatial_gating(p, gm_ln_g, gm_ln_b, gm_ws, gm_bs, rows)
    yd = _gated_deltanet(p, small, dn_conv, dn_a_log, dn_dt_bias, dn_norm, seq)
    z = _merge(ya, yg, yd, p, w_branch.astype(BF16), b_gate, rows)
    xu = _out_proj_residual(z, w_out.astype(BF16), xu, ml[2], mc[2], seq, rows)

    h2, top_i, top_w = _adaln_router(xu, norm2.reshape(1, d), ml[3], ml[4], mc[3], mc[4],
                                     w_router, b_router, seq, rows)
    parts = MOE_PARTS if bsz % MOE_PARTS == 0 else 1
    bp = bsz // parts
    b1p, w2p, b2p = _regroup_bias(b_e1), w_e2.astype(BF16), b_e2[:, None, :]
    yks = [_moe(h2[i * bp:(i + 1) * bp].reshape(bp * rows, d),
                top_i[i * bp:(i + 1) * bp].reshape(bp * rows, LANES)[:, :TOP_K],
                w1_all, b1p, w2p, b2p, layer_idx * N_EXPERTS).reshape(TOP_K, bp, rows, d) for i in range(parts)]
    out = None
    for i in range(parts):
        out = _moe_combine(yks[i], top_w, xu, ml[5], mc[5], seq, i * bp, out)
    return out


def kernel(x, c, ctx, c_ctx, w_mod, b_mod, norm1, w_in, da_lambda, da_subln, gm_ln_g, gm_ln_b, gm_ws, gm_bs,
           dn_conv, dn_a_log, dn_dt_bias, dn_norm, b_gate, w_branch, w_out, norm2, w_router, b_router,
           w_e1, b_e1, w_e2, b_e2, norm_f):
    bsz, seq, d = x.shape
    n_ctx = ctx.shape[1]
    t = seq + n_ctx
    depth = w_mod.shape[0]
    xu = jnp.concatenate([x, ctx], axis=1)
    r = -(-(bsz + 1) // 8) * 8
    cond = jnp.zeros((r, d), F32).at[:bsz].set(c).at[bsz].set(c_ctx)
    mod = _modulation(cond, w_mod, b_mod).reshape(depth, r, 6, d)
    cos_t, sin_t = _rope_tables(seq, t)
    w1_all = _w1_prep(w_e1)
    for l in range(depth):
        xu = _layer(xu, mod[l, :bsz], mod[l, bsz], seq, l, l < depth - 1, cos_t, sin_t, norm1[l], w_in[l],
                    da_lambda[l], da_subln[l], gm_ln_g[l], gm_ln_b[l], gm_ws[l], gm_bs[l], dn_conv[l],
                    dn_a_log[l], dn_dt_bias[l], dn_norm[l], b_gate[l], w_branch[l], w_out[l], norm2[l],
                    w_router[l], b_router[l], w1_all, b_e1[l], w_e2[l], b_e2[l])
    return _final_norm(xu, norm_f, seq)
```

```python
import functools
import math

import jax
import jax.numpy as jnp
from jax import lax
from jax.experimental import pallas as pl
from jax.experimental.pallas import tpu as pltpu

F32 = jnp.float32
BF16 = jnp.bfloat16

D_MODEL = 2048
GRID_W = 64
RMS_EPS = 1e-6
BRANCH_W = D_MODEL // 2
N_BRANCH = 3
DA_HD = 64
DA_HEADS = BRANCH_W // (2 * DA_HD)
ROPE_THETA = 10000.0
ROPE_PAIRS_AXIS = DA_HD // 4
GM_CHUNK = 128
GM_GW = 128
GM_GROUPS = BRANCH_W // GM_GW
DN_HD = 128
DN_HEADS = BRANCH_W // DN_HD
DN_CHUNK = 64
DN_CONV = 5
N_EXPERTS = 32
TOP_K = 4
EXPERT_FF = D_MODEL // 2
SWIGLU_LIMIT = 7.0
SWIGLU_ALPHA = 1.702

LANES = 128
VMEM_LIMIT = 56 * 1024 * 1024

OFF_DA_Q = 0
OFF_DA_K = 1024
OFF_DA_V = 2048
OFF_GM_U = 3072
OFF_GM_V = 4096
OFF_DN_Q = 5120
OFF_DN_K = 6144
OFF_DN_V = 7168
OFF_DN_Z = 8192
OFF_SMALL = 9216
OFF_GATE = 9248
N_MAIN = 9216 + N_BRANCH * D_MODEL
OFF_GATE_MAIN = 9216

MOE_TM = 512
MOE_PARTS = 1


def _cparams(sem):
    return pltpu.CompilerParams(dimension_semantics=sem, vmem_limit_bytes=VMEM_LIMIT)


def _pick(n, cands):
    for c in cands:
        if n % c == 0:
            return c
    raise ValueError(f"no tile for {n} in {cands}")


def _sigmoid(x):
    return jax.nn.sigmoid(x)


def _silu(x):
    return x * _sigmoid(x)


def _gelu_tanh(x):
    return x * (0.5 * (1.0 + jnp.tanh(0.7978845608028654 * (x + 0.044715 * (x * x * x)))))


def _bdot(a, b):
    return jnp.dot(a.astype(BF16), b.astype(BF16), preferred_element_type=F32)


def _split(a):
    hi = a.astype(BF16)
    lo = (a - hi.astype(F32)).astype(BF16)
    return hi, lo


def _dot3(a, b):
    ah, al = _split(a)
    bh, bl = _split(b)
    return (jnp.dot(ah, bh, preferred_element_type=F32)
            + (jnp.dot(al, bh, preferred_element_type=F32)
               + jnp.dot(ah, bl, preferred_element_type=F32)))


def _mod_kernel(c_ref, w_ref, b_ref, o_ref):
    s = _silu(c_ref[...])
    o_ref[...] = _bdot(s, w_ref[...]) + b_ref[...]


def _modulation(cond, w_mod, b_mod):
    nl, d, n6 = w_mod.shape
    r = cond.shape[0]
    tn = 1024
    return pl.pallas_call(
        _mod_kernel,
        grid=(nl, n6 // tn),
        in_specs=[
            pl.BlockSpec((r, d), lambda l, j: (0, 0)),
            pl.BlockSpec((None, d, tn), lambda l, j: (l, 0, j)),
            pl.BlockSpec((None, 1, tn), lambda l, j: (l, 0, j)),
        ],
        out_specs=pl.BlockSpec((None, r, tn), lambda l, j: (l, 0, j)),
        out_shape=jax.ShapeDtypeStruct((nl, r, n6), F32),
        compiler_params=_cparams(("arbitrary", "arbitrary")),
        name="modulation",
    )(cond, w_mod, b_mod.reshape(nl, 1, n6))


def _adaln_tile(x, g, shl, scl, shc, scc, row0, seq):
    tm = x.shape[0]
    y = x * lax.rsqrt(jnp.mean(x * x, axis=-1, keepdims=True) + RMS_EPS) * g
    row = row0 + lax.broadcasted_iota(jnp.int32, (tm, 1), 0)
    is_ctx = row >= seq
    scale = jnp.where(is_ctx, scc, scl)
    shift = jnp.where(is_ctx, shc, shl)
    return y * (1.0 + scale) + shift


def _mod_specs(d):
    return [
        pl.BlockSpec((1, d), lambda b, i: (0, 0)),
        pl.BlockSpec((None, 1, d), lambda b, i: (b, 0, 0)),
        pl.BlockSpec((None, 1, d), lambda b, i: (b, 0, 0)),
        pl.BlockSpec((1, d), lambda b, i: (0, 0)),
        pl.BlockSpec((1, d), lambda b, i: (0, 0)),
    ]


def _adaln_router_kernel(x_ref, g_ref, shl_ref, scl_ref, shc_ref, scc_ref, wr_ref, br_ref,
                         h_ref, idx_ref, wt_ref, *, tm, seq):
    h = _adaln_tile(x_ref[...], g_ref[...], shl_ref[...], scl_ref[...], shc_ref[...], scc_ref[...],
                    pl.program_id(1) * tm, seq)
    h_ref[...] = h.astype(h_ref.dtype)
    logits = _dot3(h, wr_ref[...]) + br_ref[...]
    lane = lax.broadcasted_iota(jnp.int32, logits.shape, 1).astype(F32)
    vals, idxs = [], []
    cur = logits
    for _ in range(TOP_K):
        m = jnp.max(cur, axis=-1, keepdims=True)
        am = jnp.min(jnp.where(cur == m, lane, float(LANES)), axis=-1, keepdims=True)
        vals.append(m)
        idxs.append(am)
        cur = jnp.where(lane == am, -jnp.inf, cur)
    es = [jnp.exp(v - vals[0]) for v in vals]
    tot = es[0] + es[1] + es[2] + es[3]
    wt = jnp.zeros(logits.shape, F32)
    ix = jnp.zeros(logits.shape, F32)
    for k in range(TOP_K):
        wt = jnp.where(lane == k, es[k] / tot, wt)
        ix = jnp.where(lane == k, idxs[k], ix)
    idx_ref[...] = ix.astype(jnp.int32)
    wt_ref[...] = wt


def _adaln_router(xu, g, shl, scl, shc, scc, w_router, b_router, seq, rows):
    bsz, t, d = xu.shape
    tm = _pick(rows, (768, 512, 384, 256, 128))
    wr = jnp.zeros((d, LANES), F32).at[:, :N_EXPERTS].set(w_router)
    br = jnp.full((1, LANES), -1e30, F32).at[0, :N_EXPERTS].set(b_router)
    return pl.pallas_call(
        functools.partial(_adaln_router_kernel, tm=tm, seq=seq),
        grid=(bsz, rows // tm),
        in_specs=[pl.BlockSpec((None, tm, d), lambda b, i: (b, i, 0))] + _mod_specs(d) + [
            pl.BlockSpec((d, LANES), lambda b, i: (0, 0)),
            pl.BlockSpec((1, LANES), lambda b, i: (0, 0)),
        ],
        out_specs=[
            pl.BlockSpec((None, tm, d), lambda b, i: (b, i, 0)),
            pl.BlockSpec((None, tm, LANES), lambda b, i: (b, i, 0)),
            pl.BlockSpec((None, tm, LANES), lambda b, i: (b, i, 0)),
        ],
        out_shape=[
            jax.ShapeDtypeStruct((bsz, rows, d), BF16),
            jax.ShapeDtypeStruct((bsz, rows, LANES), jnp.int32),
            jax.ShapeDtypeStruct((bsz, rows, LANES), F32),
        ],
        compiler_params=_cparams(("parallel", "parallel")),
        name="adaln_router",
    )(xu, g, shl, scl, shc, scc, wr, br)


def _inproj_kernel(x_ref, g_ref, shl_ref, scl_ref, shc_ref, scc_ref, w_ref, ws_ref, cos_ref, sin_ref,
                   o_ref, small_ref, h_ref, *, tm, tn, seq, tiles_per_batch, n_rope_tiles, n_q_tiles):
    j = pl.program_id(1)

    @pl.when(j == 0)
    def _():
        h = _adaln_tile(x_ref[...], g_ref[...], shl_ref[...], scl_ref[...], shc_ref[...], scc_ref[...],
                        (pl.program_id(0) % tiles_per_batch) * tm, seq)
        h_ref[...] = h.astype(h_ref.dtype)
        small_ref[...] = jnp.dot(h_ref[...], ws_ref[...], preferred_element_type=F32)

    acc = jnp.dot(h_ref[...], w_ref[...], preferred_element_type=F32)

    @pl.when(j >= n_rope_tiles)
    def _():
        o_ref[...] = acc.astype(o_ref.dtype)

    @pl.when(j < n_rope_tiles)
    def _():
        scale = jnp.where(j < n_q_tiles, DA_HD ** -0.5, 1.0).astype(F32)
        cos = cos_ref[...] * scale
        sin = sin_ref[...] * scale
        lane = lax.broadcasted_iota(jnp.int32, cos.shape, 1)
        first = (lane % DA_HD) < (DA_HD // 2)
        for c in range(tn // LANES):
            a = acc[:, c * LANES:(c + 1) * LANES]
            sw = jnp.where(first, pltpu.roll(a, LANES - DA_HD // 2, 1), pltpu.roll(a, DA_HD // 2, 1))
            o_ref[:, c * LANES:(c + 1) * LANES] = (a * cos + sw * sin).astype(o_ref.dtype)


def _in_proj(xu, g, shl, scl, shc, scc, w_main, w_small, cos_t, sin_t, seq):
    bsz, t, d = xu.shape
    n = w_main.shape[1]
    tm = _pick(t, (2304, 1152, 768, 384, 256, 128))
    tpb = t // tm
    tn = 512
    row = lambda i, j: (i // tpb, i % tpb, 0)
    per_batch = pl.BlockSpec((None, 1, d), lambda i, j: (i // tpb, 0, 0))
    const = lambda shape: pl.BlockSpec(shape, lambda i, j: (0, 0))
    return pl.pallas_call(
        functools.partial(_inproj_kernel, tm=tm, tn=tn, seq=seq, tiles_per_batch=tpb,
                          n_rope_tiles=OFF_DA_V // tn, n_q_tiles=OFF_DA_K // tn),
        grid=(bsz * tpb, n // tn),
        in_specs=[
            pl.BlockSpec((None, tm, d), row, pipeline_mode=pl.Buffered(1)),
            const((1, d)), per_batch, per_batch, const((1, d)), const((1, d)),
            pl.BlockSpec((d, tn), lambda i, j: (0, j)),
            const((d, LANES)),
            pl.BlockSpec((tm, LANES), lambda i, j: (i % tpb, 0)),
            pl.BlockSpec((tm, LANES), lambda i, j: (i % tpb, 0)),
        ],
        out_specs=[pl.BlockSpec((None, tm, tn), lambda i, j: (i // tpb, i % tpb, j)),
                   pl.BlockSpec((None, tm, LANES), row)],
        out_shape=[jax.ShapeDtypeStruct((bsz, t, n), BF16), jax.ShapeDtypeStruct((bsz, t, LANES), F32)],
        scratch_shapes=[pltpu.VMEM((tm, d), BF16)],
        compiler_params=_cparams(("parallel", "arbitrary")),
        name="in_proj",
    )(xu, g, shl, scl, shc, scc, w_main, w_small, cos_t, sin_t)


def _rope_tables(seq, t):
    rows = seq // GRID_W
    row = jnp.repeat(jnp.arange(rows, dtype=F32), GRID_W)
    col = jnp.tile(jnp.arange(GRID_W, dtype=F32), rows)
    inv = ROPE_THETA ** (-jnp.arange(ROPE_PAIRS_AXIS, dtype=F32) / ROPE_PAIRS_AXIS)
    ang = jnp.concatenate([row[:, None] * inv, col[:, None] * inv], axis=-1)
    cos, sin = jnp.cos(ang), jnp.sin(ang)
    cos_t = jnp.tile(cos, (1, LANES // (DA_HD // 2)))
    sin_t = jnp.tile(jnp.concatenate([-sin, sin], axis=-1), (1, LANES // DA_HD))
    pad = t - seq
    cos_t = jnp.concatenate([cos_t, jnp.ones((pad, LANES), F32)], axis=0)
    sin_t = jnp.concatenate([sin_t, jnp.zeros((pad, LANES), F32)], axis=0)
    return cos_t, sin_t


ATTN_ROW_GROUPS = 4
ATTN_HEADS_PER_STEP = 4


def _attn_kernel(lam_ref, g_ref, q_ref, k_ref, v_ref, o_ref, *, seq, tq, lam_init):
    qi = pl.program_id(2)
    nh = ATTN_HEADS_PER_STEP
    lp = lam_ref[...]
    l1 = jnp.sum(lp[0:1] * lp[1:2], axis=-1, keepdims=True)
    l2 = jnp.sum(lp[2:3] * lp[3:4], axis=-1, keepdims=True)
    lam = jnp.exp(l1) - jnp.exp(l2) + lam_init
    lane = lax.broadcasted_iota(jnp.int32, (tq, LANES), 1)
    qqs = []
    for h in range(nh):
        q = q_ref[:, h * LANES:(h + 1) * LANES].astype(F32)
        qqs.append(jnp.concatenate([jnp.where(lane < DA_HD, q, 0.0), jnp.where(lane >= DA_HD, q, 0.0)],
                                   axis=0).astype(BF16))

    def core(k_of, v_of):
        rs = 2 * tq // ATTN_ROW_GROUPS
        scores = [[lax.dot_general(qqs[h][i * rs:(i + 1) * rs], k_of(h), (((1,), (1,)), ((), ())),
                                   preferred_element_type=F32) for i in range(ATTN_ROW_GROUPS)] for h in range(nh)]
        for h in range(nh):
            outs = []
            for s in scores[h]:
                m = jnp.max(s, axis=-1, keepdims=True)
                p = jnp.exp(s - m)
                den = jnp.sum(p, axis=-1, keepdims=True)
                outs.append(jnp.dot(p.astype(BF16), v_of(h), preferred_element_type=F32) / den)
            o = jnp.concatenate(outs, axis=0)
            o = o[:tq] - lam * o[tq:]
            y = o * lax.rsqrt(jnp.mean(o * o, axis=-1, keepdims=True) + RMS_EPS) * g_ref[...]
            o_ref[:, h * LANES:(h + 1) * LANES] = (y * (1.0 - lam_init)).astype(o_ref.dtype)

    @pl.when(qi * tq < seq)
    def _():
        core(lambda h: k_ref[:, h * LANES:(h + 1) * LANES], lambda h: v_ref[:, h * LANES:(h + 1) * LANES])

    @pl.when(qi * tq >= seq)
    def _():
        core(lambda h: k_ref[seq:, h * LANES:(h + 1) * LANES], lambda h: v_ref[seq:, h * LANES:(h + 1) * LANES])


def _diff_attention(p, lam_params, subln_g, seq, rows, lam_init):
    bsz, t, _ = p.shape
    tq = _pick(math.gcd(seq, t - seq), (256, 128))
    nh = ATTN_HEADS_PER_STEP
    w = nh * LANES
    cq, ck, cv = OFF_DA_Q // w, OFF_DA_K // w, OFF_DA_V // w
    return pl.pallas_call(
        functools.partial(_attn_kernel, seq=seq, tq=tq, lam_init=lam_init),
        grid=(bsz, DA_HEADS // nh, rows // tq),
        in_specs=[
            pl.BlockSpec((4, DA_HD), lambda b, h, i: (0, 0)),
            pl.BlockSpec((1, 2 * DA_HD), lambda b, h, i: (0, 0)),
            pl.BlockSpec((None, tq, w), lambda b, h, i: (b, i, cq + h)),
            pl.BlockSpec((None, t, w), lambda b, h, i: (b, 0, ck + h)),
            pl.BlockSpec((None, t, w), lambda b, h, i: (b, 0, cv + h)),
        ],
        out_specs=pl.BlockSpec((None, tq, w), lambda b, h, i: (b, i, h)),
        out_shape=jax.ShapeDtypeStruct((bsz, rows, BRANCH_W), BF16),
        compiler_params=_cparams(("parallel", "parallel", "arbitrary")),
        name="diff_attention",
    )(lam_params, subln_g.reshape(1, -1), p, p, p)


def _gmlp_kernel(u_ref, v_ref, lng_ref, lnb_ref, ws_ref, bs_ref, o_ref, *, nchunks):
    for c in range(nchunks):
        r0 = c * GM_CHUNK
        u = _gelu_tanh(u_ref[r0:r0 + GM_CHUNK, :].astype(F32))
        v = _gelu_tanh(v_ref[r0:r0 + GM_CHUNK, :].astype(F32))
        xc = v - jnp.mean(v, axis=-1, keepdims=True)
        var = jnp.mean(xc * xc, axis=-1, keepdims=True)
        vn = (xc * lax.rsqrt(var + RMS_EPS) * lng_ref[...] + lnb_ref[...]).astype(BF16)
        for g in range(GM_GROUPS):
            cs = slice(g * GM_GW, (g + 1) * GM_GW)
            s = jnp.dot(ws_ref[g], vn[:, cs], preferred_element_type=F32) + bs_ref[g]
            o_ref[r0:r0 + GM_CHUNK, cs] = (u[:, cs] * s).astype(o_ref.dtype)


def _spatial_gating(p, ln_g, ln_b, ws, bs, rows):
    bsz, t, _ = p.shape
    tm = _pick(rows, (768, 512, 384, 256, 128))
    cu, cv = OFF_GM_U // BRANCH_W, OFF_GM_V // BRANCH_W
    bs_b = jnp.broadcast_to(bs[:, :, None], (GM_GROUPS, GM_CHUNK, GM_GW)).astype(F32)
    return pl.pallas_call(
        functools.partial(_gmlp_kernel, nchunks=tm // GM_CHUNK),
        grid=(bsz, rows // tm),
        in_specs=[
            pl.BlockSpec((None, tm, BRANCH_W), lambda b, i: (b, i, cu)),
            pl.BlockSpec((None, tm, BRANCH_W), lambda b, i: (b, i, cv)),
            pl.BlockSpec((1, BRANCH_W), lambda b, i: (0, 0)),
            pl.BlockSpec((1, BRANCH_W), lambda b, i: (0, 0)),
            pl.BlockSpec((GM_GROUPS, GM_CHUNK, GM_CHUNK), lambda b, i: (0, 0, 0)),
            pl.BlockSpec((GM_GROUPS, GM_CHUNK, GM_GW), lambda b, i: (0, 0, 0)),
        ],
        out_specs=pl.BlockSpec((None, tm, BRANCH_W), lambda b, i: (b, i, 0)),
        out_shape=jax.ShapeDtypeStruct((bsz, rows, BRANCH_W), BF16),
        compiler_params=_cparams(("parallel", "parallel")),
        name="spatial_gating",
    )(p, p, ln_g.reshape(1, -1), ln_b.reshape(1, -1), ws.astype(BF16), bs_b)


DN_BASE = 8
DN_PREP_GROUPS = (9, 6, 4, 3, 2, 1)
DN_HEAD_GROUP = 2


def _dn_kernel(alog_ref, dtb_ref, q_ref, k_ref, v_ref, z_ref, sm_ref, ar_ref, cw_ref, ng_ref, o_ref,
               qn_ref, kn_ref, vn_ref, rowg_ref, ac_ref, b_ref, d_ref, cd_ref, oacc_ref, st_ref,
               *, seq, t):
    hg = DN_HEAD_GROUP
    hblk = pl.program_id(1)
    nc = t // DN_CHUNK
    n_lat = seq // DN_CHUNK
    n_ctx = nc - n_lat
    hw = DN_HD
    cw = DN_CHUNK

    row = lax.broadcasted_iota(jnp.int32, (t, 1), 0)
    seg_lo = jnp.where(row < seq, 0, seq)
    seg_hi = jnp.where(row < seq, seq, t)

    def conv_silu(x_ref, w):
        x = x_ref[...].astype(F32)
        acc = x * w[DN_CONV // 2:DN_CONV // 2 + 1, :]
        for s in (-2, -1, 1, 2):
            xs = pltpu.roll(x, (-s) % t, 0)
            rs = row + s
            ok = (rs >= seg_lo) & (rs < seg_hi)
            acc = acc + jnp.where(ok, xs, 0.0) * w[DN_CONV // 2 + s:DN_CONV // 2 + s + 1, :]
        return _silu(acc)

    def l2n(x):
        return x * lax.rsqrt(jnp.sum(x * x, axis=-1, keepdims=True) + RMS_EPS)

    qc = conv_silu(q_ref, cw_ref[0])
    kc = conv_silu(k_ref, cw_ref[1])
    vn_ref[...] = conv_silu(v_ref, cw_ref[2])
    for j in range(hg):
        cs = slice(j * hw, (j + 1) * hw)
        qn_ref[:, cs] = l2n(qc[:, cs]) * (DN_HD ** -0.5)
        kn_ref[:, cs] = l2n(kc[:, cs])

    def softplus(x):
        return jnp.maximum(x, 0.0) + jnp.log1p(jnp.exp(-jnp.abs(x)))

    rw = 2 * hg * cw
    lane_r = lax.broadcasted_iota(jnp.int32, (1, rw), 1)
    chain_r = lane_r // cw
    pos_r = lane_r % cw
    alog_r = jnp.zeros((1, rw), F32)
    dt_r = jnp.zeros((1, rw), F32)
    for d in range(2):
        for j in range(hg):
            alog_r = jnp.where(chain_r == hg * d + j, alog_ref[d, hblk * hg + j], alog_r)
            dt_r = jnp.where(chain_r == hg * d + j, dtb_ref[d, hblk * hg + j], dt_r)
    g_all = -jnp.exp(alog_r) * softplus(ar_ref[...].reshape(nc * 8, rw) + dt_r)
    pre = g_all
    suf = g_all
    sh = 1
    while sh < cw:
        pre = pre + jnp.where(pos_r >= sh, pltpu.roll(pre, sh, 1), 0.0)
        suf = suf + jnp.where(pos_r < cw - sh, pltpu.roll(suf, rw - sh, 1), 0.0)
        sh *= 2
    run = jnp.where(lane_r >= hg * cw, suf, pre).reshape(nc, 8, rw)
    tot = (pre + suf - g_all).reshape(nc, 8, rw)
    sub = lax.broadcasted_iota(jnp.int32, (nc, 8, rw), 1)
    both = jnp.where(sub == 0, run, tot)
    for d in range(2):
        rowg_ref[d] = both[:, :, d * hg * cw:(d + 1) * hg * cw]

    st_ref[...] = jnp.zeros(st_ref.shape, F32)

    pshape = (cw, hg * cw)
    ii = lax.broadcasted_iota(jnp.int32, pshape, 0)
    lp = lax.broadcasted_iota(jnp.int32, pshape, 1)
    jl = lp % cw
    left = lp < cw
    diag = ii == jl
    eye_p = jnp.where(diag, 1.0, 0.0).astype(F32)
    blk_base = (ii // DN_BASE) == (jl // DN_BASE)
    incl = [ii >= jl, ii <= jl]
    strict = [ii > jl, ii < jl]
    half = [jnp.where(left, 1.0, 0.0).astype(BF16), jnp.where(left, 0.0, 1.0).astype(BF16)]
    left_sq = lax.broadcasted_iota(jnp.int32, (LANES, LANES), 1) < cw

    def blockdiag(b16):
        return jnp.concatenate([b16 * half[0], b16 * half[1]], axis=0)

    def pprod(a, b):
        return jnp.dot(a.astype(BF16), blockdiag(b.astype(BF16)), preferred_element_type=F32)

    def tri_inverse(lmats):
        ms = [jnp.where(blk_base, -l, 0.0) for l in lmats]
        xs = [eye_p + m for m in ms]
        pws = [pprod(m, m) for m in ms]
        span = 4
        while span <= DN_BASE:
            tts = [pprod(jnp.concatenate([x, pw], axis=0), pw) for x, pw in zip(xs, pws)]
            xs = [x + tt[:cw] for x, tt in zip(xs, tts)]
            pws = [tt[cw:] for tt in tts]
            span *= 2
        bs = DN_BASE
        while bs < cw:
            off = ((ii // (2 * bs)) == (jl // (2 * bs))) & ((ii // bs) != (jl // bs))
            cmats = [jnp.where(off, l, 0.0) for l in lmats]
            ys = [pprod(x, c) for x, c in zip(xs, cmats)]
            zs = [pprod(y, x) for y, x in zip(ys, xs)]
            xs = [x - z for x, z in zip(xs, zs)]
            bs *= 2
        return xs

    def prep_load(c):
        rows = pl.ds(pl.multiple_of(c * cw, cw), cw)
        kk = [kn_ref[rows, j * hw:(j + 1) * hw] for j in range(hg)]
        qq = [qn_ref[rows, j * hw:(j + 1) * hw] for j in range(hg)]
        vv = [vn_ref[rows, j * hw:(j + 1) * hw] for j in range(hg)]
        return kk, qq, vv, sm_ref[rows, :], [rowg_ref[d, c] for d in range(2)]

    def prep_compute(loaded):
        n = len(loaded)
        gram, qk, ktp = [], [], []
        for kk, qq, vv, sm, rgs in loaded:
            gq, kt = [], []
            for j in range(hg):
                kb = kk[j].astype(BF16)
                gq.append(lax.dot_general(jnp.concatenate([kb, qq[j].astype(BF16)], axis=0),
                                          jnp.concatenate([kb, kb], axis=0), (((1,), (1,)), ((), ())),
                                          preferred_element_type=F32))
                kt.append(jnp.concatenate([kk[j], kk[j]], axis=0).T)
            pair = jnp.where(left_sq, gq[0], gq[1])
            gram.append(pair[:cw])
            qk.append(pair[cw:])
            ktp.append(jnp.where(left_sq, kt[0], kt[1]))
        pre = []
        for ci, (kk, qq, vv, sm, rgs) in enumerate(loaded):
            lane_c = lax.broadcasted_iota(jnp.int32, sm.shape, 1)
            for d in range(2):
                bcol = [_sigmoid(jnp.sum(jnp.where(lane_c == d * DN_HEADS + hblk * hg + j, sm, 0.0),
                                         axis=1, keepdims=True)) for j in range(hg)]
                gc_row, g_tot = rgs[d][0:1, :], rgs[d][1:2, :]
                gdiag = jnp.where(diag, gc_row, 0.0)
                gcol = [jnp.sum(jnp.where(left, gdiag, 0.0), axis=1, keepdims=True),
                        jnp.sum(jnp.where(left, 0.0, gdiag), axis=1, keepdims=True)]
                gc = jnp.where(left, gcol[0], gcol[1])
                beta = jnp.where(left, bcol[0], bcol[1])
                dec = jnp.exp(jnp.where(incl[d], gc - gc_row, -jnp.inf))
                lmat = jnp.where(strict[d], beta * gram[ci] * dec, 0.0)
                pre.append((ci, d, bcol, gcol, gc_row, g_tot, dec, lmat))
        tinvs = tri_inverse([p[-1] for p in pre])
        rhss, egs = [], []
        for ci, d, bcol, gcol, gc_row, g_tot, dec, lmat in pre:
            kk, qq, vv = loaded[ci][:3]
            eg = [jnp.exp(gcol[j]) for j in range(hg)]
            egs.append(eg)
            rhss.append(jnp.concatenate(
                [jnp.concatenate([vv[j] * bcol[j], kk[j] * (bcol[j] * eg[j])], axis=1) for j in range(hg)],
                axis=0).astype(BF16))
        sols = []
        for tinv, rhs in zip(tinvs, rhss):
            t16 = tinv.astype(BF16)
            sols.append(jnp.dot(jnp.concatenate([t16 * half[0], t16 * half[1]], axis=0), rhs,
                                preferred_element_type=F32))
        xs = []
        for (ci, d, bcol, gcol, gc_row, g_tot, dec, lmat), sol in zip(pre, sols):
            q_intra = jnp.where(incl[d], qk[ci] * dec, 0.0)
            qk2 = jnp.concatenate([q_intra, ktp[ci] * jnp.exp(g_tot - gc_row)], axis=0).astype(BF16)
            s16 = sol.astype(BF16)
            zero = jnp.zeros((cw, 2 * hw), BF16)
            bd = jnp.concatenate([jnp.concatenate([s16[:cw], zero], axis=1),
                                  jnp.concatenate([zero, s16[cw:]], axis=1)], axis=0)
            xs.append(jnp.dot(qk2, bd, preferred_element_type=F32))
        outs = [[None, None] for _ in range(n)]
        for (ci, d, bcol, gcol, gc_row, g_tot, dec, lmat), x, eg in zip(pre, xs, egs):
            qq = loaded[ci][1]
            e_tot = jnp.exp(g_tot)
            outs[ci][d] = dict(
                ac=[jnp.concatenate([x[cw:, j * 2 * hw + hw:(j + 1) * 2 * hw],
                                     qq[j] * eg[j] - x[:cw, j * 2 * hw + hw:(j + 1) * 2 * hw]], axis=0).astype(BF16)
                    for j in range(hg)],
                b=[x[cw:, j * 2 * hw:j * 2 * hw + hw].astype(BF16) for j in range(hg)],
                dd=[x[:cw, j * 2 * hw:j * 2 * hw + hw].astype(BF16) for j in range(hg)],
                cd=[jnp.broadcast_to(e_tot[:, j * cw:j * cw + 1], (1, hw)) for j in range(hg)])
        return outs

    def prep_store(c, outs):
        rows = pl.ds(pl.multiple_of(c * cw, cw), cw)
        for d in range(2):
            for j in range(hg):
                ac_ref[hg * d + j, c] = outs[d]["ac"][j]
                b_ref[hg * d + j, c] = outs[d]["b"][j]
                d_ref[hg * d + j, rows, :] = outs[d]["dd"][j]
                cd_ref[hg * d + j, c] = outs[d]["cd"][j]

    group = _pick(nc, DN_PREP_GROUPS)

    def prep_body(g, carry):
        cs = [g * group + cc for cc in range(group)]
        loaded = [prep_load(c) for c in cs]
        outs = prep_compute(loaded)
        for c, o in zip(cs, outs):
            prep_store(c, o)
        return carry

    lax.fori_loop(0, nc // group, prep_body, 0)

    def scan_body(i, carry):
        cf = jnp.where(i < n_ctx, i + n_lat, i - n_ctx)
        cb = nc - 1 - i
        dirs = ((0, cf), (1, cb))
        rows = [pl.ds(pl.multiple_of(c * cw, cw), cw) for _, c in dirs]
        state = [st_ref[s] for s in range(2 * hg)]
        ac = [ac_ref[hg * d + j, c] for d, c in dirs for j in range(hg)]
        bb = [b_ref[hg * d + j, c] for d, c in dirs for j in range(hg)]
        dd = [d_ref[hg * d + j, rows[d], :] for d, _ in dirs for j in range(hg)]
        cd = [cd_ref[hg * d + j, c] for d, c in dirs for j in range(hg)]
        rs = [jnp.dot(ac[s], state[s].astype(BF16), preferred_element_type=F32) for s in range(2 * hg)]
        o_new = [jnp.concatenate([rs[hg * d + j][hw:] + dd[hg * d + j].astype(F32) for j in range(hg)], axis=1)
                 for d in range(2)]
        st_new = [state[s] * cd[s] - rs[s][:hw] + bb[s].astype(F32) for s in range(2 * hg)]
        for d, _ in dirs:
            oacc_ref[d, rows[d], :] = o_new[d]
        for s in range(2 * hg):
            st_ref[s] = st_new[s]
        return carry

    lax.fori_loop(0, nc, scan_body, 0)

    o = oacc_ref[0] + oacc_ref[1]
    for j in range(hg):
        cs = slice(j * hw, (j + 1) * hw)
        oj = o[:, cs]
        y = oj * lax.rsqrt(jnp.mean(oj * oj, axis=-1, keepdims=True) + RMS_EPS) * ng_ref[...]
        o_ref[:, cs] = (y * _silu(z_ref[:, cs].astype(F32))).astype(o_ref.dtype)


def _gated_deltanet(p, small, conv_w, a_log, dt_bias, norm_g, seq):
    bsz, t, _ = p.shape
    nc = t // DN_CHUNK
    hg = DN_HEAD_GROUP
    assert hg == 2
    w = hg * DN_HD
    rw = 2 * hg * DN_CHUNK
    a = small[..., 2 * DN_HEADS:4 * DN_HEADS].reshape(bsz, nc, DN_CHUNK, 2, DN_HEADS // hg, hg)
    a_row = jnp.transpose(a, (0, 4, 1, 3, 5, 2)).reshape(bsz, DN_HEADS // hg, nc, 1, rw)
    a_row = jnp.broadcast_to(a_row, (bsz, DN_HEADS // hg, nc, 8, rw))
    cq, ck, cv, cz = (OFF_DN_Q // w, OFF_DN_K // w, OFF_DN_V // w, OFF_DN_Z // w)
    slab = lambda c0: pl.BlockSpec((None, t, w), lambda b, h: (b, 0, c0 + h))
    smem = pl.BlockSpec(memory_space=pltpu.SMEM)
    return pl.pallas_call(
        functools.partial(_dn_kernel, seq=seq, t=t),
        grid=(bsz, DN_HEADS // hg),
        in_specs=[
            smem, smem,
            slab(cq), slab(ck), slab(cv), slab(cz),
            pl.BlockSpec((None, t, LANES), lambda b, h: (b, 0, 0)),
            pl.BlockSpec((None, None, nc, 8, rw), lambda b, h: (b, h, 0, 0, 0)),
            pl.BlockSpec((3, DN_CONV, w), lambda b, h: (0, 0, h)),
            pl.BlockSpec((1, DN_HD), lambda b, h: (0, 0)),
        ],
        out_specs=pl.BlockSpec((None, t, w), lambda b, h: (b, 0, h)),
        out_shape=jax.ShapeDtypeStruct((bsz, t, BRANCH_W), BF16),
        scratch_shapes=[
            pltpu.VMEM((t, w), F32), pltpu.VMEM((t, w), F32), pltpu.VMEM((t, w), F32),
            pltpu.VMEM((2, nc, 8, hg * DN_CHUNK), F32),
            pltpu.VMEM((2 * hg, nc, DN_HD + DN_CHUNK, DN_HD), BF16),
            pltpu.VMEM((2 * hg, nc, DN_HD, DN_HD), BF16),
            pltpu.VMEM((2 * hg, t, DN_HD), BF16),
            pltpu.VMEM((2 * hg, nc, 1, DN_HD), F32),
            pltpu.VMEM((2, t, w), F32),
            pltpu.VMEM((2 * hg, DN_HD, DN_HD), F32),
        ],
        compiler_params=_cparams(("parallel", "parallel")),
        name="gated_deltanet",
    )(a_log, dt_bias, p, p, p, p, small, a_row, conv_w, norm_g.reshape(1, -1))


def _merge_kernel(ya_ref, yg_ref, yd_ref, ga_ref, gg_ref, gd_ref, wb_ref, bg_ref, o_ref):
    acc = None
    for i, (y_ref, g_ref) in enumerate(((ya_ref, ga_ref), (yg_ref, gg_ref), (yd_ref, gd_ref))):
        gate = _sigmoid(g_ref[...].astype(F32) + bg_ref[i])
        term = gate * jnp.dot(y_ref[...], wb_ref[i], preferred_element_type=F32)
        acc = term if acc is None else acc + term
    o_ref[...] = acc.astype(o_ref.dtype)


def _merge(ya, yg, yd, p, w_branch, b_gate, rows):
    bsz = p.shape[0]
    d = D_MODEL
    tm = _pick(rows, (768, 512, 384, 256, 128))
    tn = 1024
    g0 = OFF_GATE_MAIN // tn
    y_spec = pl.BlockSpec((None, tm, BRANCH_W), lambda b, i, j: (b, i, 0))
    gate_spec = lambda k: pl.BlockSpec((None, tm, tn), lambda b, i, j: (b, i, g0 + k * (d // tn) + j))
    return pl.pallas_call(
        _merge_kernel,
        grid=(bsz, rows // tm, d // tn),
        in_specs=[y_spec, y_spec, y_spec, gate_spec(0), gate_spec(1), gate_spec(2),
                  pl.BlockSpec((N_BRANCH, BRANCH_W, tn), lambda b, i, j: (0, 0, j)),
                  pl.BlockSpec((N_BRANCH, 1, tn), lambda b, i, j: (0, 0, j))],
        out_specs=pl.BlockSpec((None, tm, tn), lambda b, i, j: (b, i, j)),
        out_shape=jax.ShapeDtypeStruct((bsz, rows, d), BF16),
        compiler_params=_cparams(("parallel", "parallel", "arbitrary")),
        name="merge_branches",
    )(ya, yg, yd, p, p, p, w_branch, b_gate.reshape(N_BRANCH, 1, d))


def _outproj_kernel(z_ref, w_ref, x_ref, gl_ref, gc_ref, o_ref, *, tm, seq):
    acc = jnp.dot(z_ref[...], w_ref[...], preferred_element_type=F32)
    row = pl.program_id(1) * tm + lax.broadcasted_iota(jnp.int32, (tm, 1), 0)
    gate = jnp.where(row >= seq, gc_ref[...], gl_ref[...])
    o_ref[...] = x_ref[...] + gate * acc


def _out_proj_residual(z, w_out, xu, gate_l, gate_c, seq, rows):
    bsz, t, d = xu.shape
    tm = _pick(rows, (768, 512, 384, 256, 128))
    tn = 1024
    return pl.pallas_call(
        functools.partial(_outproj_kernel, tm=tm, seq=seq),
        grid=(bsz, rows // tm, d // tn),
        in_specs=[
            pl.BlockSpec((None, tm, d), lambda b, i, j: (b, i, 0)),
            pl.BlockSpec((d, tn), lambda b, i, j: (0, j)),
            pl.BlockSpec((None, tm, tn), lambda b, i, j: (b, i, j)),
            pl.BlockSpec((None, 1, tn), lambda b, i, j: (b, 0, j)),
            pl.BlockSpec((1, tn), lambda b, i, j: (0, j)),
        ],
        out_specs=pl.BlockSpec((None, tm, tn), lambda b, i, j: (b, i, j)),
        out_shape=jax.ShapeDtypeStruct((bsz, rows, d), F32),
        compiler_params=_cparams(("parallel", "parallel", "arbitrary")),
        name="out_proj_residual",
    )(z, w_out, xu, gate_l, gate_c)


W1_BLOCK = 2 * LANES


def _w1_prep_kernel(w_ref, perm_ref, o_ref):
    w = w_ref[...].astype(BF16)
    for blk in range(w.shape[1] // W1_BLOCK):
        cs = slice(blk * W1_BLOCK, (blk + 1) * W1_BLOCK)
        o_ref[:, cs] = jnp.dot(w[:, cs], perm_ref[...], preferred_element_type=F32).astype(o_ref.dtype)


def _w1_prep(w_e1):
    nl, ne, d, n = w_e1.shape
    tk = 1024
    j = jnp.arange(W1_BLOCK)
    src = jnp.where(j < LANES, 2 * j, 2 * (j - LANES) + 1)
    perm = (jnp.arange(W1_BLOCK)[:, None] == src[None, :]).astype(BF16)
    return pl.pallas_call(
        _w1_prep_kernel,
        grid=(nl * ne, d // tk),
        in_specs=[pl.BlockSpec((None, tk, n), lambda e, k: (e, k, 0)),
                  pl.BlockSpec((W1_BLOCK, W1_BLOCK), lambda e, k: (0, 0))],
        out_specs=pl.BlockSpec((None, tk, n), lambda e, k: (e, k, 0)),
        out_shape=jax.ShapeDtypeStruct((nl * ne, d, n), BF16),
        compiler_params=_cparams(("parallel", "parallel")),
        name="expert_w1_prep",
    )(w_e1.reshape(nl * ne, d, n), perm)


def _regroup_bias(b_e1):
    ne, n = b_e1.shape
    return jnp.transpose(b_e1.reshape(ne, n // W1_BLOCK, LANES, 2), (0, 1, 3, 2)).reshape(ne, 1, n)


def _expert_kernel(be_ref, bv_ref, x_ref, w1_ref, b1_ref, w2_ref, b2_ref, o_ref, hid_ref):
    i = pl.program_id(0)

    @pl.when(bv_ref[i] > 0)
    def _():
        hgl = jnp.dot(x_ref[...], w1_ref[...], preferred_element_type=F32) + b1_ref[...]
        for blk in range(hgl.shape[1] // W1_BLOCK):
            xg = jnp.minimum(hgl[:, blk * W1_BLOCK:blk * W1_BLOCK + LANES], SWIGLU_LIMIT)
            xl = jnp.clip(hgl[:, blk * W1_BLOCK + LANES:(blk + 1) * W1_BLOCK], -SWIGLU_LIMIT, SWIGLU_LIMIT)
            hid_ref[:, blk * LANES:(blk + 1) * LANES] = (
                xg * _sigmoid(SWIGLU_ALPHA * xg) * (xl + 1.0)).astype(hid_ref.dtype)
        y = jnp.dot(hid_ref[...], w2_ref[...].astype(BF16), preferred_element_type=F32) + b2_ref[...]
        o_ref[...] = y.astype(o_ref.dtype)


def _experts(xs, blk_e, blk_valid, w1, b1, w2, b2, e0):
    n_rows, d = xs.shape
    tm = MOE_TM
    ff = EXPERT_FF
    grid_spec = pltpu.PrefetchScalarGridSpec(
        num_scalar_prefetch=2,
        grid=(n_rows // tm,),
        in_specs=[
            pl.BlockSpec((tm, d), lambda i, be, bv: (i, 0)),
            pl.BlockSpec((None, d, 2 * ff), lambda i, be, bv: (e0 + be[i], 0, 0)),
            pl.BlockSpec((None, 1, 2 * ff), lambda i, be, bv: (be[i], 0, 0)),
            pl.BlockSpec((None, ff, d), lambda i, be, bv: (be[i], 0, 0)),
            pl.BlockSpec((None, 1, d), lambda i, be, bv: (be[i], 0, 0)),
        ],
        out_specs=pl.BlockSpec((tm, d), lambda i, be, bv: (i, 0)),
        scratch_shapes=[pltpu.VMEM((tm, ff), BF16)],
    )
    return pl.pallas_call(
        _expert_kernel,
        grid_spec=grid_spec,
        out_shape=jax.ShapeDtypeStruct((n_rows, d), BF16),
        compiler_params=_cparams(("arbitrary",)),
        name="moe_experts",
    )(blk_e, blk_valid, xs, w1, b1, w2, b2)


def _moe(h2, top_i, w1, b1, w2, b2, e0):
    n_tok, d = h2.shape
    tm = MOE_TM
    n_assign = n_tok * TOP_K
    flat_e = top_i.reshape(n_assign)
    order = jnp.argsort(flat_e).astype(jnp.int32)
    rank = jnp.argsort(order).astype(jnp.int32)
    onehot = flat_e[:, None] == jnp.arange(N_EXPERTS, dtype=flat_e.dtype)[None, :]
    counts = jnp.sum(onehot, axis=0, dtype=jnp.int32)
    padded = (counts + tm - 1) // tm * tm
    pad_end = jnp.cumsum(padded)
    start = jnp.cumsum(counts) - counts
    shift = (pad_end - padded) - start
    pos = rank + jnp.sum(jnp.where(onehot, shift[None, :], 0), axis=1)
    n_blocks = -(-n_assign // tm) + N_EXPERTS
    blk_start = jnp.arange(n_blocks, dtype=jnp.int32) * tm
    blk_valid = (blk_start < pad_end[-1]).astype(jnp.int32)
    blk_e = jnp.sum(blk_start[:, None] >= pad_end[None, :], axis=1, dtype=jnp.int32)
    last_e = jnp.sum(pad_end[-1] - 1 >= pad_end, dtype=jnp.int32)
    blk_e = jnp.where(blk_valid > 0, blk_e, last_e)
    row = blk_start[:, None] + jnp.arange(tm, dtype=jnp.int32)[None, :]
    src = row - shift[blk_e][:, None]
    lo = start[blk_e][:, None]
    live = (src >= lo) & (src < lo + counts[blk_e][:, None]) & (blk_valid[:, None] > 0)
    row_tok = jnp.where(live, order[jnp.clip(src, 0, n_assign - 1)] // TOP_K, row % n_tok).reshape(n_blocks * tm)
    xs = h2[row_tok]
    y = _experts(xs, blk_e, blk_valid, w1, b1, w2, b2, e0)
    return y[pos.reshape(n_tok, TOP_K).T.reshape(n_assign)].reshape(TOP_K, n_tok, d)


def _combine_kernel(y_ref, w_ref, x_ref, gl_ref, gc_ref, *rest, tm, seq):
    o_ref = rest[-1]
    w = w_ref[...]
    acc = y_ref[0].astype(F32) * w[:, 0:1]
    for k in range(1, TOP_K):
        acc = acc + y_ref[k].astype(F32) * w[:, k:k + 1]
    row = pl.program_id(1) * tm + lax.broadcasted_iota(jnp.int32, (tm, 1), 0)
    gate = jnp.where(row >= seq, gc_ref[...], gl_ref[...])
    o_ref[...] = x_ref[...] + gate * acc


def _moe_combine(yk, top_w, xu, gate_l, gate_c, seq, b0, prev):
    bsz, rows, d = xu.shape
    bp = yk.shape[1]
    tm = _pick(rows, (512, 384, 256, 128))
    in_specs = [
        pl.BlockSpec((TOP_K, None, tm, d), lambda b, i: (0, b, i, 0)),
        pl.BlockSpec((None, tm, LANES), lambda b, i: (b + b0, i, 0)),
        pl.BlockSpec((None, tm, d), lambda b, i: (b + b0, i, 0)),
        pl.BlockSpec((None, 1, d), lambda b, i: (b + b0, 0, 0)),
        pl.BlockSpec((1, d), lambda b, i: (0, 0)),
    ]
    args = [yk, top_w, xu, gate_l, gate_c]
    aliases = {}
    if prev is not None:
        in_specs.append(pl.BlockSpec(memory_space=pl.ANY))
        args.append(prev)
        aliases = {len(args) - 1: 0}
    return pl.pallas_call(
        functools.partial(_combine_kernel, tm=tm, seq=seq),
        grid=(bp, rows // tm),
        in_specs=in_specs,
        out_specs=pl.BlockSpec((None, tm, d), lambda b, i: (b + b0, i, 0)),
        out_shape=jax.ShapeDtypeStruct((bsz, rows, d), F32),
        input_output_aliases=aliases,
        compiler_params=_cparams(("parallel", "parallel")),
        name="moe_combine",
    )(*args)


def _final_kernel(x_ref, g_ref, o_ref):
    x = x_ref[...]
    o_ref[...] = x * lax.rsqrt(jnp.mean(x * x, axis=-1, keepdims=True) + RMS_EPS) * g_ref[...]


def _final_norm(xu, g, seq):
    bsz, t, d = xu.shape
    tm = _pick(seq, (512, 256, 128))
    return pl.pallas_call(
        _final_kernel,
        grid=(bsz, seq // tm),
        in_specs=[pl.BlockSpec((None, tm, d), lambda b, i: (b, i, 0)),
                  pl.BlockSpec((1, d), lambda b, i: (0, 0))],
        out_specs=pl.BlockSpec((None, tm, d), lambda b, i: (b, i, 0)),
        out_shape=jax.ShapeDtypeStruct((bsz, seq, d), F32),
        compiler_params=_cparams(("parallel", "parallel")),
        name="final_norm",
    )(xu, g.reshape(1, d))


def _layer(xu, mod_l, mod_c, seq, layer_idx, ctx_out, cos_t, sin_t, norm1, w_in, da_lambda, da_subln,
           gm_ln_g, gm_ln_b, gm_ws, gm_bs, dn_conv, dn_a_log, dn_dt_bias, dn_norm, b_gate, w_branch,
           w_out, norm2, w_router, b_router, w1_all, b_e1, w_e2, b_e2):
    bsz, t, d = xu.shape
    rows = t if ctx_out else seq
    lam_init = 0.8 - 0.6 * math.exp(-0.3 * layer_idx)
    ml = [mod_l[:, k:k + 1, :] for k in range(6)]
    mc = [mod_c[k:k + 1, :] for k in range(6)]

    w_main = jnp.concatenate([w_in[:, :OFF_SMALL], w_in[:, OFF_GATE:]], axis=1).astype(BF16)
    w_small = jnp.zeros((d, LANES), BF16).at[:, :OFF_GATE - OFF_SMALL].set(
        w_in[:, OFF_SMALL:OFF_GATE].astype(BF16))
    p, small = _in_proj(xu, norm1.reshape(1, d), ml[0], ml[1], mc[0], mc[1], w_main, w_small, cos_t, sin_t, seq)

    ya = _diff_attention(p, da_lambda, da_subln, seq, rows, lam_init)
    yg = _spatial_gating(p, gm_ln_g, gm_ln_b, gm_ws, gm_bs, rows)
    yd = _gated_deltanet(p, small, dn_conv, dn_a_log, dn_dt_bias, dn_norm, seq)
    z = _merge(ya, yg, yd, p, w_branch.astype(BF16), b_gate, rows)
    xu = _out_proj_residual(z, w_out.astype(BF16), xu, ml[2], mc[2], seq, rows)

    h2, top_i, top_w = _adaln_router(xu, norm2.reshape(1, d), ml[3], ml[4], mc[3], mc[4],
                                     w_router, b_router, seq, rows)
    parts = MOE_PARTS if bsz % MOE_PARTS == 0 else 1
    bp = bsz // parts
    b1p, w2p, b2p = _regroup_bias(b_e1), w_e2, b_e2[:, None, :]
    yks = [_moe(h2[i * bp:(i + 1) * bp].reshape(bp * rows, d),
                top_i[i * bp:(i + 1) * bp].reshape(bp * rows, LANES)[:, :TOP_K],
                w1_all, b1p, w2p, b2p, layer_idx * N_EXPERTS).reshape(TOP_K, bp, rows, d) for i in range(parts)]
    out = None
    for i in range(parts):
        out = _moe_combine(yks[i], top_w, xu, ml[5], mc[5], seq, i * bp, out)
    return out


def kernel(x, c, ctx, c_ctx, w_mod, b_mod, norm1, w_in, da_lambda, da_subln, gm_ln_g, gm_ln_b, gm_ws, gm_bs,
           dn_conv, dn_a_log, dn_dt_bias, dn_norm, b_gate, w_branch, w_out, norm2, w_router, b_router,
           w_e1, b_e1, w_e2, b_e2, norm_f):
    bsz, seq, d = x.shape
    n_ctx = ctx.shape[1]
    t = seq + n_ctx
    depth = w_mod.shape[0]
    xu = jnp.concatenate([x, ctx], axis=1)
    r = -(-(bsz + 1) // 8) * 8
    cond = jnp.zeros((r, d), F32).at[:bsz].set(c).at[bsz].set(c_ctx)
    mod = _modulation(cond, w_mod, b_mod).reshape(depth, r, 6, d)
    cos_t, sin_t = _rope_tables(seq, t)
    w1_all = _w1_prep(w_e1)
    for l in range(depth):
        xu = _layer(xu, mod[l, :bsz], mod[l, bsz], seq, l, l < depth - 1, cos_t, sin_t, norm1[l], w_in[l],
                    da_lambda[l], da_subln[l], gm_ln_g[l], gm_ln_b[l], gm_ws[l], gm_bs[l], dn_conv[l],
                    dn_a_log[l], dn_dt_bias[l], dn_norm[l], b_gate[l], w_branch[l], w_out[l], norm2[l],
                    w_router[l], b_router[l], w1_all, b_e1[l], w_e2[l], b_e2[l])
    return _final_norm(xu, norm_f, seq)
```

```python
import functools
import math

import jax
import jax.numpy as jnp
from jax import lax
from jax.experimental import pallas as pl
from jax.experimental.pallas import tpu as pltpu

F32 = jnp.float32
BF16 = jnp.bfloat16

D_MODEL = 2048
GRID_W = 64
RMS_EPS = 1e-6
BRANCH_W = D_MODEL // 2
N_BRANCH = 3
DA_HD = 64
DA_HEADS = BRANCH_W // (2 * DA_HD)
ROPE_THETA = 10000.0
ROPE_PAIRS_AXIS = DA_HD // 4
GM_CHUNK = 128
GM_GW = 128
GM_GROUPS = BRANCH_W // GM_GW
DN_HD = 128
DN_HEADS = BRANCH_W // DN_HD
DN_CHUNK = 64
DN_CONV = 5
N_EXPERTS = 32
TOP_K = 4
EXPERT_FF = D_MODEL // 2
SWIGLU_LIMIT = 7.0
SWIGLU_ALPHA = 1.702

LANES = 128
VMEM_LIMIT = 56 * 1024 * 1024

OFF_DA_Q = 0
OFF_DA_K = 1024
OFF_DA_V = 2048
OFF_GM_U = 3072
OFF_GM_V = 4096
OFF_DN_Q = 5120
OFF_DN_K = 6144
OFF_DN_V = 7168
OFF_DN_Z = 8192
OFF_SMALL = 9216
OFF_GATE = 9248
N_MAIN = 9216 + N_BRANCH * D_MODEL
OFF_GATE_MAIN = 9216

MOE_TM = 512
MOE_PARTS = 1


def _cparams(sem):
    return pltpu.CompilerParams(dimension_semantics=sem, vmem_limit_bytes=VMEM_LIMIT)


def _pick(n, cands):
    for c in cands:
        if n % c == 0:
            return c
    raise ValueError(f"no tile for {n} in {cands}")


def _sigmoid(x):
    return jax.nn.sigmoid(x)


def _silu(x):
    return x * _sigmoid(x)


def _gelu_tanh(x):
    return x * (0.5 * (1.0 + jnp.tanh(0.7978845608028654 * (x + 0.044715 * (x * x * x)))))


def _bdot(a, b):
    return jnp.dot(a.astype(BF16), b.astype(BF16), preferred_element_type=F32)


def _split(a):
    hi = a.astype(BF16)
    lo = (a - hi.astype(F32)).astype(BF16)
    return hi, lo


def _dot3(a, b):
    ah, al = _split(a)
    bh, bl = _split(b)
    return (jnp.dot(ah, bh, preferred_element_type=F32)
            + (jnp.dot(al, bh, preferred_element_type=F32)
               + jnp.dot(ah, bl, preferred_element_type=F32)))


def _mod_kernel(c_ref, w_ref, b_ref, o_ref):
    s = _silu(c_ref[...])
    o_ref[...] = _bdot(s, w_ref[...]) + b_ref[...]


def _modulation(cond, w_mod, b_mod):
    nl, d, n6 = w_mod.shape
    r = cond.shape[0]
    tn = 1024
    return pl.pallas_call(
        _mod_kernel,
        grid=(nl, n6 // tn),
        in_specs=[
            pl.BlockSpec((r, d), lambda l, j: (0, 0)),
            pl.BlockSpec((None, d, tn), lambda l, j: (l, 0, j)),
            pl.BlockSpec((None, 1, tn), lambda l, j: (l, 0, j)),
        ],
        out_specs=pl.BlockSpec((None, r, tn), lambda l, j: (l, 0, j)),
        out_shape=jax.ShapeDtypeStruct((nl, r, n6), F32),
        compiler_params=_cparams(("arbitrary", "arbitrary")),
        name="modulation",
    )(cond, w_mod, b_mod.reshape(nl, 1, n6))


def _adaln_tile(x, g, shl, scl, shc, scc, row0, seq):
    tm = x.shape[0]
    y = x * lax.rsqrt(jnp.mean(x * x, axis=-1, keepdims=True) + RMS_EPS) * g
    row = row0 + lax.broadcasted_iota(jnp.int32, (tm, 1), 0)
    is_ctx = row >= seq
    scale = jnp.where(is_ctx, scc, scl)
    shift = jnp.where(is_ctx, shc, shl)
    return y * (1.0 + scale) + shift


def _mod_specs(d):
    return [
        pl.BlockSpec((1, d), lambda b, i: (0, 0)),
        pl.BlockSpec((None, 1, d), lambda b, i: (b, 0, 0)),
        pl.BlockSpec((None, 1, d), lambda b, i: (b, 0, 0)),
        pl.BlockSpec((1, d), lambda b, i: (0, 0)),
        pl.BlockSpec((1, d), lambda b, i: (0, 0)),
    ]


def _adaln_router_kernel(x_ref, g_ref, shl_ref, scl_ref, shc_ref, scc_ref, wr_ref, br_ref,
                         h_ref, idx_ref, wt_ref, *, tm, seq):
    h = _adaln_tile(x_ref[...], g_ref[...], shl_ref[...], scl_ref[...], shc_ref[...], scc_ref[...],
                    pl.program_id(1) * tm, seq)
    h_ref[...] = h.astype(h_ref.dtype)
    logits = _dot3(h, wr_ref[...]) + br_ref[...]
    lane = lax.broadcasted_iota(jnp.int32, logits.shape, 1).astype(F32)
    vals, idxs = [], []
    cur = logits
    for _ in range(TOP_K):
        m = jnp.max(cur, axis=-1, keepdims=True)
        am = jnp.min(jnp.where(cur == m, lane, float(LANES)), axis=-1, keepdims=True)
        vals.append(m)
        idxs.append(am)
        cur = jnp.where(lane == am, -jnp.inf, cur)
    es = [jnp.exp(v - vals[0]) for v in vals]
    tot = es[0] + es[1] + es[2] + es[3]
    wt = jnp.zeros(logits.shape, F32)
    ix = jnp.zeros(logits.shape, F32)
    for k in range(TOP_K):
        wt = jnp.where(lane == k, es[k] / tot, wt)
        ix = jnp.where(lane == k, idxs[k], ix)
    idx_ref[...] = ix.astype(jnp.int32)
    wt_ref[...] = wt


def _adaln_router(xu, g, shl, scl, shc, scc, w_router, b_router, seq, rows):
    bsz, t, d = xu.shape
    tm = _pick(rows, (768, 512, 384, 256, 128))
    wr = jnp.zeros((d, LANES), F32).at[:, :N_EXPERTS].set(w_router)
    br = jnp.full((1, LANES), -1e30, F32).at[0, :N_EXPERTS].set(b_router)
    return pl.pallas_call(
        functools.partial(_adaln_router_kernel, tm=tm, seq=seq),
        grid=(bsz, rows // tm),
        in_specs=[pl.BlockSpec((None, tm, d), lambda b, i: (b, i, 0))] + _mod_specs(d) + [
            pl.BlockSpec((d, LANES), lambda b, i: (0, 0)),
            pl.BlockSpec((1, LANES), lambda b, i: (0, 0)),
        ],
        out_specs=[
            pl.BlockSpec((None, tm, d), lambda b, i: (b, i, 0)),
            pl.BlockSpec((None, tm, LANES), lambda b, i: (b, i, 0)),
            pl.BlockSpec((None, tm, LANES), lambda b, i: (b, i, 0)),
        ],
        out_shape=[
            jax.ShapeDtypeStruct((bsz, rows, d), BF16),
            jax.ShapeDtypeStruct((bsz, rows, LANES), jnp.int32),
            jax.ShapeDtypeStruct((bsz, rows, LANES), F32),
        ],
        compiler_params=_cparams(("parallel", "parallel")),
        name="adaln_router",
    )(xu, g, shl, scl, shc, scc, wr, br)


def _inproj_kernel(x_ref, g_ref, shl_ref, scl_ref, shc_ref, scc_ref, w_ref, ws_ref, cos_ref, sin_ref,
                   o_ref, small_ref, h_ref, *, tm, tn, seq, tiles_per_batch, n_rope_tiles, n_q_tiles):
    j = pl.program_id(1)

    @pl.when(j == 0)
    def _():
        h = _adaln_tile(x_ref[...], g_ref[...], shl_ref[...], scl_ref[...], shc_ref[...], scc_ref[...],
                        (pl.program_id(0) % tiles_per_batch) * tm, seq)
        h_ref[...] = h.astype(h_ref.dtype)
        small_ref[...] = jnp.dot(h_ref[...], ws_ref[...], preferred_element_type=F32)

    acc = jnp.dot(h_ref[...], w_ref[...], preferred_element_type=F32)

    @pl.when(j >= n_rope_tiles)
    def _():
        o_ref[...] = acc.astype(o_ref.dtype)

    @pl.when(j < n_rope_tiles)
    def _():
        scale = jnp.where(j < n_q_tiles, DA_HD ** -0.5, 1.0).astype(F32)
        cos = cos_ref[...] * scale
        sin = sin_ref[...] * scale
        lane = lax.broadcasted_iota(jnp.int32, cos.shape, 1)
        first = (lane % DA_HD) < (DA_HD // 2)
        for c in range(tn // LANES):
            a = acc[:, c * LANES:(c + 1) * LANES]
            sw = jnp.where(first, pltpu.roll(a, LANES - DA_HD // 2, 1), pltpu.roll(a, DA_HD // 2, 1))
            o_ref[:, c * LANES:(c + 1) * LANES] = (a * cos + sw * sin).astype(o_ref.dtype)


def _in_proj(xu, g, shl, scl, shc, scc, w_main, w_small, cos_t, sin_t, seq):
    bsz, t, d = xu.shape
    n = w_main.shape[1]
    tm = _pick(t, (2304, 1152, 768, 384, 256, 128))
    tpb = t // tm
    tn = 512
    row = lambda i, j: (i // tpb, i % tpb, 0)
    per_batch = pl.BlockSpec((None, 1, d), lambda i, j: (i // tpb, 0, 0))
    const = lambda shape: pl.BlockSpec(shape, lambda i, j: (0, 0))
    return pl.pallas_call(
        functools.partial(_inproj_kernel, tm=tm, tn=tn, seq=seq, tiles_per_batch=tpb,
                          n_rope_tiles=OFF_DA_V // tn, n_q_tiles=OFF_DA_K // tn),
        grid=(bsz * tpb, n // tn),
        in_specs=[
            pl.BlockSpec((None, tm, d), row, pipeline_mode=pl.Buffered(1)),
            const((1, d)), per_batch, per_batch, const((1, d)), const((1, d)),
            pl.BlockSpec((d, tn), lambda i, j: (0, j)),
            const((d, LANES)),
            pl.BlockSpec((tm, LANES), lambda i, j: (i % tpb, 0)),
            pl.BlockSpec((tm, LANES), lambda i, j: (i % tpb, 0)),
        ],
        out_specs=[pl.BlockSpec((None, tm, tn), lambda i, j: (i // tpb, i % tpb, j)),
                   pl.BlockSpec((None, tm, LANES), row)],
        out_shape=[jax.ShapeDtypeStruct((bsz, t, n), BF16), jax.ShapeDtypeStruct((bsz, t, LANES), F32)],
        scratch_shapes=[pltpu.VMEM((tm, d), BF16)],
        compiler_params=_cparams(("parallel", "arbitrary")),
        name="in_proj",
    )(xu, g, shl, scl, shc, scc, w_main, w_small, cos_t, sin_t)


def _rope_tables(seq, t):
    rows = seq // GRID_W
    row = jnp.repeat(jnp.arange(rows, dtype=F32), GRID_W)
    col = jnp.tile(jnp.arange(GRID_W, dtype=F32), rows)
    inv = ROPE_THETA ** (-jnp.arange(ROPE_PAIRS_AXIS, dtype=F32) / ROPE_PAIRS_AXIS)
    ang = jnp.concatenate([row[:, None] * inv, col[:, None] * inv], axis=-1)
    cos, sin = jnp.cos(ang), jnp.sin(ang)
    cos_t = jnp.tile(cos, (1, LANES // (DA_HD // 2)))
    sin_t = jnp.tile(jnp.concatenate([-sin, sin], axis=-1), (1, LANES // DA_HD))
    pad = t - seq
    cos_t = jnp.concatenate([cos_t, jnp.ones((pad, LANES), F32)], axis=0)
    sin_t = jnp.concatenate([sin_t, jnp.zeros((pad, LANES), F32)], axis=0)
    return cos_t, sin_t


ATTN_ROW_GROUPS = 4
ATTN_HEADS_PER_STEP = 4


def _attn_kernel(lam_ref, g_ref, q_ref, k_ref, v_ref, o_ref, *, seq, tq, lam_init):
    qi = pl.program_id(2)
    nh = ATTN_HEADS_PER_STEP
    lp = lam_ref[...]
    l1 = jnp.sum(lp[0:1] * lp[1:2], axis=-1, keepdims=True)
    l2 = jnp.sum(lp[2:3] * lp[3:4], axis=-1, keepdims=True)
    lam = jnp.exp(l1) - jnp.exp(l2) + lam_init
    lane = lax.broadcasted_iota(jnp.int32, (tq, LANES), 1)
    qqs = []
    for h in range(nh):
        q = q_ref[:, h * LANES:(h + 1) * LANES].astype(F32)
        qqs.append(jnp.concatenate([jnp.where(lane < DA_HD, q, 0.0), jnp.where(lane >= DA_HD, q, 0.0)],
                                   axis=0).astype(BF16))

    def core(k_of, v_of):
        rs = 2 * tq // ATTN_ROW_GROUPS
        scores = [[lax.dot_general(qqs[h][i * rs:(i + 1) * rs], k_of(h), (((1,), (1,)), ((), ())),
                                   preferred_element_type=F32) for i in range(ATTN_ROW_GROUPS)] for h in range(nh)]
        for h in range(nh):
            outs = []
            for s in scores[h]:
                m = jnp.max(s, axis=-1, keepdims=True)
                p = jnp.exp(s - m)
                den = jnp.sum(p, axis=-1, keepdims=True)
                outs.append(jnp.dot(p.astype(BF16), v_of(h), preferred_element_type=F32) / den)
            o = jnp.concatenate(outs, axis=0)
            o = o[:tq] - lam * o[tq:]
            y = o * lax.rsqrt(jnp.mean(o * o, axis=-1, keepdims=True) + RMS_EPS) * g_ref[...]
            o_ref[:, h * LANES:(h + 1) * LANES] = (y * (1.0 - lam_init)).astype(o_ref.dtype)

    @pl.when(qi * tq < seq)
    def _():
        core(lambda h: k_ref[:, h * LANES:(h + 1) * LANES], lambda h: v_ref[:, h * LANES:(h + 1) * LANES])

    @pl.when(qi * tq >= seq)
    def _():
        core(lambda h: k_ref[seq:, h * LANES:(h + 1) * LANES], lambda h: v_ref[seq:, h * LANES:(h + 1) * LANES])


def _diff_attention(p, lam_params, subln_g, seq, rows, lam_init):
    bsz, t, _ = p.shape
    tq = _pick(math.gcd(seq, t - seq), (256, 128))
    nh = ATTN_HEADS_PER_STEP
    w = nh * LANES
    cq, ck, cv = OFF_DA_Q // w, OFF_DA_K // w, OFF_DA_V // w
    return pl.pallas_call(
        functools.partial(_attn_kernel, seq=seq, tq=tq, lam_init=lam_init),
        grid=(bsz, DA_HEADS // nh, rows // tq),
        in_specs=[
            pl.BlockSpec((4, DA_HD), lambda b, h, i: (0, 0)),
            pl.BlockSpec((1, 2 * DA_HD), lambda b, h, i: (0, 0)),
            pl.BlockSpec((None, tq, w), lambda b, h, i: (b, i, cq + h)),
            pl.BlockSpec((None, t, w), lambda b, h, i: (b, 0, ck + h)),
            pl.BlockSpec((None, t, w), lambda b, h, i: (b, 0, cv + h)),
        ],
        out_specs=pl.BlockSpec((None, tq, w), lambda b, h, i: (b, i, h)),
        out_shape=jax.ShapeDtypeStruct((bsz, rows, BRANCH_W), BF16),
        compiler_params=_cparams(("parallel", "parallel", "arbitrary")),
        name="diff_attention",
    )(lam_params, subln_g.reshape(1, -1), p, p, p)


def _gmlp_kernel(u_ref, v_ref, lng_ref, lnb_ref, ws_ref, bs_ref, o_ref, *, nchunks):
    for c in range(nchunks):
        r0 = c * GM_CHUNK
        u = _gelu_tanh(u_ref[r0:r0 + GM_CHUNK, :].astype(F32))
        v = _gelu_tanh(v_ref[r0:r0 + GM_CHUNK, :].astype(F32))
        xc = v - jnp.mean(v, axis=-1, keepdims=True)
        var = jnp.mean(xc * xc, axis=-1, keepdims=True)
        vn = (xc * lax.rsqrt(var + RMS_EPS) * lng_ref[...] + lnb_ref[...]).astype(BF16)
        for g in range(GM_GROUPS):
            cs = slice(g * GM_GW, (g + 1) * GM_GW)
            s = jnp.dot(ws_ref[g], vn[:, cs], preferred_element_type=F32) + bs_ref[g]
            o_ref[r0:r0 + GM_CHUNK, cs] = (u[:, cs] * s).astype(o_ref.dtype)


def _spatial_gating(p, ln_g, ln_b, ws, bs, rows):
    bsz, t, _ = p.shape
    tm = _pick(rows, (768, 512, 384, 256, 128))
    cu, cv = OFF_GM_U // BRANCH_W, OFF_GM_V // BRANCH_W
    bs_b = jnp.broadcast_to(bs[:, :, None], (GM_GROUPS, GM_CHUNK, GM_GW)).astype(F32)
    return pl.pallas_call(
        functools.partial(_gmlp_kernel, nchunks=tm // GM_CHUNK),
        grid=(bsz, rows // tm),
        in_specs=[
            pl.BlockSpec((None, tm, BRANCH_W), lambda b, i: (b, i, cu)),
            pl.BlockSpec((None, tm, BRANCH_W), lambda b, i: (b, i, cv)),
            pl.BlockSpec((1, BRANCH_W), lambda b, i: (0, 0)),
            pl.BlockSpec((1, BRANCH_W), lambda b, i: (0, 0)),
            pl.BlockSpec((GM_GROUPS, GM_CHUNK, GM_CHUNK), lambda b, i: (0, 0, 0)),
            pl.BlockSpec((GM_GROUPS, GM_CHUNK, GM_GW), lambda b, i: (0, 0, 0)),
        ],
        out_specs=pl.BlockSpec((None, tm, BRANCH_W), lambda b, i: (b, i, 0)),
        out_shape=jax.ShapeDtypeStruct((bsz, rows, BRANCH_W), BF16),
        compiler_params=_cparams(("parallel", "parallel")),
        name="spatial_gating",
    )(p, p, ln_g.reshape(1, -1), ln_b.reshape(1, -1), ws.astype(BF16), bs_b)


DN_BASE = 8
DN_PREP_GROUPS = (9, 6, 4, 3, 2, 1)
DN_HEAD_GROUP = 2


def _dn_kernel(alog_ref, dtb_ref, q_ref, k_ref, v_ref, z_ref, sm_ref, ar_ref, cw_ref, ng_ref, o_ref,
               qn_ref, kn_ref, vn_ref, rowg_ref, ac_ref, b_ref, d_ref, cd_ref, oacc_ref, st_ref,
               *, seq, t):
    hg = DN_HEAD_GROUP
    hblk = pl.program_id(1)
    nc = t // DN_CHUNK
    n_lat = seq // DN_CHUNK
    n_ctx = nc - n_lat
    hw = DN_HD
    cw = DN_CHUNK

    row = lax.broadcasted_iota(jnp.int32, (t, 1), 0)
    seg_lo = jnp.where(row < seq, 0, seq)
    seg_hi = jnp.where(row < seq, seq, t)

    def conv_silu(x_ref, w):
        x = x_ref[...].astype(F32)
        acc = x * w[DN_CONV // 2:DN_CONV // 2 + 1, :]
        for s in (-2, -1, 1, 2):
            xs = pltpu.roll(x, (-s) % t, 0)
            rs = row + s
            ok = (rs >= seg_lo) & (rs < seg_hi)
            acc = acc + jnp.where(ok, xs, 0.0) * w[DN_CONV // 2 + s:DN_CONV // 2 + s + 1, :]
        return _silu(acc)

    def l2n(x):
        return x * lax.rsqrt(jnp.sum(x * x, axis=-1, keepdims=True) + RMS_EPS)

    qc = conv_silu(q_ref, cw_ref[0])
    kc = conv_silu(k_ref, cw_ref[1])
    vn_ref[...] = conv_silu(v_ref, cw_ref[2])
    for j in range(hg):
        cs = slice(j * hw, (j + 1) * hw)
        qn_ref[:, cs] = l2n(qc[:, cs]) * (DN_HD ** -0.5)
        kn_ref[:, cs] = l2n(kc[:, cs])

    def softplus(x):
        return jnp.maximum(x, 0.0) + jnp.log1p(jnp.exp(-jnp.abs(x)))

    rw = 2 * hg * cw
    lane_r = lax.broadcasted_iota(jnp.int32, (1, rw), 1)
    chain_r = lane_r // cw
    pos_r = lane_r % cw
    alog_r = jnp.zeros((1, rw), F32)
    dt_r = jnp.zeros((1, rw), F32)
    for d in range(2):
        for j in range(hg):
            alog_r = jnp.where(chain_r == hg * d + j, alog_ref[d, hblk * hg + j], alog_r)
            dt_r = jnp.where(chain_r == hg * d + j, dtb_ref[d, hblk * hg + j], dt_r)
    g_all = -jnp.exp(alog_r) * softplus(ar_ref[...].reshape(nc * 8, rw) + dt_r)
    pre = g_all
    suf = g_all
    sh = 1
    while sh < cw:
        pre = pre + jnp.where(pos_r >= sh, pltpu.roll(pre, sh, 1), 0.0)
        suf = suf + jnp.where(pos_r < cw - sh, pltpu.roll(suf, rw - sh, 1), 0.0)
        sh *= 2
    run = jnp.where(lane_r >= hg * cw, suf, pre).reshape(nc, 8, rw)
    tot = (pre + suf - g_all).reshape(nc, 8, rw)
    sub = lax.broadcasted_iota(jnp.int32, (nc, 8, rw), 1)
    both = jnp.where(sub == 0, run, tot)
    for d in range(2):
        rowg_ref[d] = both[:, :, d * hg * cw:(d + 1) * hg * cw]

    st_ref[...] = jnp.zeros(st_ref.shape, F32)

    pshape = (cw, hg * cw)
    ii = lax.broadcasted_iota(jnp.int32, pshape, 0)
    lp = lax.broadcasted_iota(jnp.int32, pshape, 1)
    jl = lp % cw
    left = lp < cw
    diag = ii == jl
    eye_p = jnp.where(diag, 1.0, 0.0).astype(F32)
    blk_base = (ii // DN_BASE) == (jl // DN_BASE)
    incl = [ii >= jl, ii <= jl]
    strict = [ii > jl, ii < jl]
    half = [jnp.where(left, 1.0, 0.0).astype(BF16), jnp.where(left, 0.0, 1.0).astype(BF16)]
    left_sq = lax.broadcasted_iota(jnp.int32, (LANES, LANES), 1) < cw

    def blockdiag(b16):
        return jnp.concatenate([b16 * half[0], b16 * half[1]], axis=0)

    def pprod(a, b):
        return jnp.dot(a.astype(BF16), blockdiag(b.astype(BF16)), preferred_element_type=F32)

    def tri_inverse(lmats):
        ms = [jnp.where(blk_base, -l, 0.0) for l in lmats]
        xs = [eye_p + m for m in ms]
        pws = [pprod(m, m) for m in ms]
        span = 4
        while span <= DN_BASE:
            tts = [pprod(jnp.concatenate([x, pw], axis=0), pw) for x, pw in zip(xs, pws)]
            xs = [x + tt[:cw] for x, tt in zip(xs, tts)]
            pws = [tt[cw:] for tt in tts]
            span *= 2
        bs = DN_BASE
        while bs < cw:
            off = ((ii // (2 * bs)) == (jl // (2 * bs))) & ((ii // bs) != (jl // bs))
            cmats = [jnp.where(off, l, 0.0) for l in lmats]
            ys = [pprod(x, c) for x, c in zip(xs, cmats)]
            zs = [pprod(y, x) for y, x in zip(ys, xs)]
            xs = [x - z for x, z in zip(xs, zs)]
            bs *= 2
        return xs

    def prep_load(c):
        rows = pl.ds(pl.multiple_of(c * cw, cw), cw)
        kk = [kn_ref[rows, j * hw:(j + 1) * hw] for j in range(hg)]
        qq = [qn_ref[rows, j * hw:(j + 1) * hw] for j in range(hg)]
        vv = [vn_ref[rows, j * hw:(j + 1) * hw] for j in range(hg)]
        return kk, qq, vv, sm_ref[rows, :], [rowg_ref[d, c] for d in range(2)]

    def prep_compute(loaded):
        n = len(loaded)
        gram, qk, ktp = [], [], []
        for kk, qq, vv, sm, rgs in loaded:
            gq, kt = [], []
            for j in range(hg):
                kb = kk[j].astype(BF16)
                gq.append(lax.dot_general(jnp.concatenate([kb, qq[j].astype(BF16)], axis=0),
                                          jnp.concatenate([kb, kb], axis=0), (((1,), (1,)), ((), ())),
                                          preferred_element_type=F32))
                kt.append(jnp.concatenate([kk[j], kk[j]], axis=0).T)
            pair = jnp.where(left_sq, gq[0], gq[1])
            gram.append(pair[:cw])
            qk.append(pair[cw:])
            ktp.append(jnp.where(left_sq, kt[0], kt[1]))
        pre = []
        for ci, (kk, qq, vv, sm, rgs) in enumerate(loaded):
            lane_c = lax.broadcasted_iota(jnp.int32, sm.shape, 1)
            for d in range(2):
                bcol = [_sigmoid(jnp.sum(jnp.where(lane_c == d * DN_HEADS + hblk * hg + j, sm, 0.0),
                                         axis=1, keepdims=True)) for j in range(hg)]
                gc_row, g_tot = rgs[d][0:1, :], rgs[d][1:2, :]
                gdiag = jnp.where(diag, gc_row, 0.0)
                gcol = [jnp.sum(jnp.where(left, gdiag, 0.0), axis=1, keepdims=True),
                        jnp.sum(jnp.where(left, 0.0, gdiag), axis=1, keepdims=True)]
                gc = jnp.where(left, gcol[0], gcol[1])
                beta = jnp.where(left, bcol[0], bcol[1])
                dec = jnp.exp(jnp.where(incl[d], gc - gc_row, -jnp.inf))
                lmat = jnp.where(strict[d], beta * gram[ci] * dec, 0.0)
                pre.append((ci, d, bcol, gcol, gc_row, g_tot, dec, lmat))
        tinvs = tri_inverse([p[-1] for p in pre])
        rhss, egs = [], []
        for ci, d, bcol, gcol, gc_row, g_tot, dec, lmat in pre:
            kk, qq, vv = loaded[ci][:3]
            eg = [jnp.exp(gcol[j]) for j in range(hg)]
            egs.append(eg)
            rhss.append(jnp.concatenate(
                [jnp.concatenate([vv[j] * bcol[j], kk[j] * (bcol[j] * eg[j])], axis=1) for j in range(hg)],
                axis=0).astype(BF16))
        sols = []
        for tinv, rhs in zip(tinvs, rhss):
            t16 = tinv.astype(BF16)
            sols.append(jnp.dot(jnp.concatenate([t16 * half[0], t16 * half[1]], axis=0), rhs,
                                preferred_element_type=F32))
        xs = []
        for (ci, d, bcol, gcol, gc_row, g_tot, dec, lmat), sol in zip(pre, sols):
            q_intra = jnp.where(incl[d], qk[ci] * dec, 0.0)
            qk2 = jnp.concatenate([q_intra, ktp[ci] * jnp.exp(g_tot - gc_row)], axis=0).astype(BF16)
            s16 = sol.astype(BF16)
            zero = jnp.zeros((cw, 2 * hw), BF16)
            bd = jnp.concatenate([jnp.concatenate([s16[:cw], zero], axis=1),
                                  jnp.concatenate([zero, s16[cw:]], axis=1)], axis=0)
            xs.append(jnp.dot(qk2, bd, preferred_element_type=F32))
        outs = [[None, None] for _ in range(n)]
        for (ci, d, bcol, gcol, gc_row, g_tot, dec, lmat), x, eg in zip(pre, xs, egs):
            qq = loaded[ci][1]
            e_tot = jnp.exp(g_tot)
            outs[ci][d] = dict(
                ac=[jnp.concatenate([x[cw:, j * 2 * hw + hw:(j + 1) * 2 * hw],
                                     qq[j] * eg[j] - x[:cw, j * 2 * hw + hw:(j + 1) * 2 * hw]], axis=0).astype(BF16)
                    for j in range(hg)],
                b=[x[cw:, j * 2 * hw:j * 2 * hw + hw].astype(BF16) for j in range(hg)],
                dd=[x[:cw, j * 2 * hw:j * 2 * hw + hw].astype(BF16) for j in range(hg)],
                cd=[jnp.broadcast_to(e_tot[:, j * cw:j * cw + 1], (1, hw)) for j in range(hg)])
        return outs

    def prep_store(c, outs):
        rows = pl.ds(pl.multiple_of(c * cw, cw), cw)
        for d in range(2):
            for j in range(hg):
                ac_ref[hg * d + j, c] = outs[d]["ac"][j]
                b_ref[hg * d + j, c] = outs[d]["b"][j]
                d_ref[hg * d + j, rows, :] = outs[d]["dd"][j]
                cd_ref[hg * d + j, c] = outs[d]["cd"][j]

    group = _pick(nc, DN_PREP_GROUPS)

    def prep_body(g, carry):
        cs = [g * group + cc for cc in range(group)]
        loaded = [prep_load(c) for c in cs]
        outs = prep_compute(loaded)
        for c, o in zip(cs, outs):
            prep_store(c, o)
        return carry

    lax.fori_loop(0, nc // group, prep_body, 0)

    def scan_body(i, carry):
        cf = jnp.where(i < n_ctx, i + n_lat, i - n_ctx)
        cb = nc - 1 - i
        dirs = ((0, cf), (1, cb))
        rows = [pl.ds(pl.multiple_of(c * cw, cw), cw) for _, c in dirs]
        state = [st_ref[s] for s in range(2 * hg)]
        ac = [ac_ref[hg * d + j, c] for d, c in dirs for j in range(hg)]
        bb = [b_ref[hg * d + j, c] for d, c in dirs for j in range(hg)]
        dd = [d_ref[hg * d + j, rows[d], :] for d, _ in dirs for j in range(hg)]
        cd = [cd_ref[hg * d + j, c] for d, c in dirs for j in range(hg)]
        rs = [jnp.dot(ac[s], state[s].astype(BF16), preferred_element_type=F32) for s in range(2 * hg)]
        o_new = [jnp.concatenate([rs[hg * d + j][hw:] + dd[hg * d + j].astype(F32) for j in range(hg)], axis=1)
                 for d in range(2)]
        st_new = [state[s] * cd[s] - rs[s][:hw] + bb[s].astype(F32) for s in range(2 * hg)]
        for d, _ in dirs:
            oacc_ref[d, rows[d], :] = o_new[d]
        for s in range(2 * hg):
            st_ref[s] = st_new[s]
        return carry

    lax.fori_loop(0, nc, scan_body, 0)

    o = oacc_ref[0] + oacc_ref[1]
    for j in range(hg):
        cs = slice(j * hw, (j + 1) * hw)
        oj = o[:, cs]
        y = oj * lax.rsqrt(jnp.mean(oj * oj, axis=-1, keepdims=True) + RMS_EPS) * ng_ref[...]
        o_ref[:, cs] = (y * _silu(z_ref[:, cs].astype(F32))).astype(o_ref.dtype)


def _gated_deltanet(p, small, conv_w, a_log, dt_bias, norm_g, seq):
    bsz, t, _ = p.shape
    nc = t // DN_CHUNK
    hg = DN_HEAD_GROUP
    assert hg == 2
    w = hg * DN_HD
    rw = 2 * hg * DN_CHUNK
    a = small[..., 2 * DN_HEADS:4 * DN_HEADS].reshape(bsz, nc, DN_CHUNK, 2, DN_HEADS // hg, hg)
    a_row = jnp.transpose(a, (0, 4, 1, 3, 5, 2)).reshape(bsz, DN_HEADS // hg, nc, 1, rw)
    a_row = jnp.broadcast_to(a_row, (bsz, DN_HEADS // hg, nc, 8, rw))
    cq, ck, cv, cz = (OFF_DN_Q // w, OFF_DN_K // w, OFF_DN_V // w, OFF_DN_Z // w)
    slab = lambda c0: pl.BlockSpec((None, t, w), lambda b, h: (b, 0, c0 + h))
    smem = pl.BlockSpec(memory_space=pltpu.SMEM)
    return pl.pallas_call(
        functools.partial(_dn_kernel, seq=seq, t=t),
        grid=(bsz, DN_HEADS // hg),
        in_specs=[
            smem, smem,
            slab(cq), slab(ck), slab(cv), slab(cz),
            pl.BlockSpec((None, t, LANES), lambda b, h: (b, 0, 0)),
            pl.BlockSpec((None, None, nc, 8, rw), lambda b, h: (b, h, 0, 0, 0)),
            pl.BlockSpec((3, DN_CONV, w), lambda b, h: (0, 0, h)),
            pl.BlockSpec((1, DN_HD), lambda b, h: (0, 0)),
        ],
        out_specs=pl.BlockSpec((None, t, w), lambda b, h: (b, 0, h)),
        out_shape=jax.ShapeDtypeStruct((bsz, t, BRANCH_W), BF16),
        scratch_shapes=[
            pltpu.VMEM((t, w), F32), pltpu.VMEM((t, w), F32), pltpu.VMEM((t, w), F32),
            pltpu.VMEM((2, nc, 8, hg * DN_CHUNK), F32),
            pltpu.VMEM((2 * hg, nc, DN_HD + DN_CHUNK, DN_HD), BF16),
            pltpu.VMEM((2 * hg, nc, DN_HD, DN_HD), BF16),
            pltpu.VMEM((2 * hg, t, DN_HD), BF16),
            pltpu.VMEM((2 * hg, nc, 1, DN_HD), F32),
            pltpu.VMEM((2, t, w), F32),
            pltpu.VMEM((2 * hg, DN_HD, DN_HD), F32),
        ],
        compiler_params=_cparams(("parallel", "parallel")),
        name="gated_deltanet",
    )(a_log, dt_bias, p, p, p, p, small, a_row, conv_w, norm_g.reshape(1, -1))


def _merge_kernel(ya_ref, yg_ref, yd_ref, ga_ref, gg_ref, gd_ref, wb_ref, bg_ref, o_ref):
    acc = None
    for i, (y_ref, g_ref) in enumerate(((ya_ref, ga_ref), (yg_ref, gg_ref), (yd_ref, gd_ref))):
        gate = _sigmoid(g_ref[...].astype(F32) + bg_ref[i])
        term = gate * jnp.dot(y_ref[...], wb_ref[i], preferred_element_type=F32)
        acc = term if acc is None else acc + term
    o_ref[...] = acc.astype(o_ref.dtype)


def _merge(ya, yg, yd, p, w_branch, b_gate, rows):
    bsz = p.shape[0]
    d = D_MODEL
    tm = _pick(rows, (768, 512, 384, 256, 128))
    tn = 1024
    g0 = OFF_GATE_MAIN // tn
    y_spec = pl.BlockSpec((None, tm, BRANCH_W), lambda b, i, j: (b, i, 0))
    gate_spec = lambda k: pl.BlockSpec((None, tm, tn), lambda b, i, j: (b, i, g0 + k * (d // tn) + j))
    return pl.pallas_call(
        _merge_kernel,
        grid=(bsz, rows // tm, d // tn),
        in_specs=[y_spec, y_spec, y_spec, gate_spec(0), gate_spec(1), gate_spec(2),
                  pl.BlockSpec((N_BRANCH, BRANCH_W, tn), lambda b, i, j: (0, 0, j)),
                  pl.BlockSpec((N_BRANCH, 1, tn), lambda b, i, j: (0, 0, j))],
        out_specs=pl.BlockSpec((None, tm, tn), lambda b, i, j: (b, i, j)),
        out_shape=jax.ShapeDtypeStruct((bsz, rows, d), BF16),
        compiler_params=_cparams(("parallel", "parallel", "arbitrary")),
        name="merge_branches",
    )(ya, yg, yd, p, p, p, w_branch, b_gate.reshape(N_BRANCH, 1, d))


def _outproj_kernel(z_ref, w_ref, x_ref, gl_ref, gc_ref, o_ref, *, tm, seq):
    acc = jnp.dot(z_ref[...], w_ref[...], preferred_element_type=F32)
    row = pl.program_id(1) * tm + lax.broadcasted_iota(jnp.int32, (tm, 1), 0)
    gate = jnp.where(row >= seq, gc_ref[...], gl_ref[...])
    o_ref[...] = x_ref[...] + gate * acc


def _out_proj_residual(z, w_out, xu, gate_l, gate_c, seq, rows):
    bsz, t, d = xu.shape
    tm = _pick(rows, (768, 512, 384, 256, 128))
    tn = 1024
    return pl.pallas_call(
        functools.partial(_outproj_kernel, tm=tm, seq=seq),
        grid=(bsz, rows // tm, d // tn),
        in_specs=[
            pl.BlockSpec((None, tm, d), lambda b, i, j: (b, i, 0)),
            pl.BlockSpec((d, tn), lambda b, i, j: (0, j)),
            pl.BlockSpec((None, tm, tn), lambda b, i, j: (b, i, j)),
            pl.BlockSpec((None, 1, tn), lambda b, i, j: (b, 0, j)),
            pl.BlockSpec((1, tn), lambda b, i, j: (0, j)),
        ],
        out_specs=pl.BlockSpec((None, tm, tn), lambda b, i, j: (b, i, j)),
        out_shape=jax.ShapeDtypeStruct((bsz, rows, d), F32),
        compiler_params=_cparams(("parallel", "parallel", "arbitrary")),
        name="out_proj_residual",
    )(z, w_out, xu, gate_l, gate_c)


W1_BLOCK = 2 * LANES


def _w1_prep_kernel(w_ref, perm_ref, o_ref):
    w = w_ref[...].astype(BF16)
    for blk in range(w.shape[1] // W1_BLOCK):
        cs = slice(blk * W1_BLOCK, (blk + 1) * W1_BLOCK)
        o_ref[:, cs] = jnp.dot(w[:, cs], perm_ref[...], preferred_element_type=F32).astype(o_ref.dtype)


def _w1_prep(w_e1):
    nl, ne, d, n = w_e1.shape
    tk = 1024
    j = jnp.arange(W1_BLOCK)
    src = jnp.where(j < LANES, 2 * j, 2 * (j - LANES) + 1)
    perm = (jnp.arange(W1_BLOCK)[:, None] == src[None, :]).astype(BF16)
    return pl.pallas_call(
        _w1_prep_kernel,
        grid=(nl * ne, d // tk),
        in_specs=[pl.BlockSpec((None, tk, n), lambda e, k: (e, k, 0)),
                  pl.BlockSpec((W1_BLOCK, W1_BLOCK), lambda e, k: (0, 0))],
        out_specs=pl.BlockSpec((None, tk, n), lambda e, k: (e, k, 0)),
        out_shape=jax.ShapeDtypeStruct((nl * ne, d, n), BF16),
        compiler_params=_cparams(("parallel", "parallel")),
        name="expert_w1_prep",
    )(w_e1.reshape(nl * ne, d, n), perm)


def _regroup_bias(b_e1):
    ne, n = b_e1.shape
    return jnp.transpose(b_e1.reshape(ne, n // W1_BLOCK, LANES, 2), (0, 1, 3, 2)).reshape(ne, 1, n)


def _expert_kernel(be_ref, bv_ref, x_ref, w1_ref, b1_ref, w2_ref, b2_ref, o_ref, hid_ref):
    i = pl.program_id(0)

    @pl.when(bv_ref[i] > 0)
    def _():
        hgl = jnp.dot(x_ref[...], w1_ref[...], preferred_element_type=F32) + b1_ref[...]
        for blk in range(hgl.shape[1] // W1_BLOCK):
            xg = jnp.minimum(hgl[:, blk * W1_BLOCK:blk * W1_BLOCK + LANES], SWIGLU_LIMIT)
            xl = jnp.clip(hgl[:, blk * W1_BLOCK + LANES:(blk + 1) * W1_BLOCK], -SWIGLU_LIMIT, SWIGLU_LIMIT)
            hid_ref[:, blk * LANES:(blk + 1) * LANES] = (
                xg * _sigmoid(SWIGLU_ALPHA * xg) * (xl + 1.0)).astype(hid_ref.dtype)
        y = jnp.dot(hid_ref[...], w2_ref[...].astype(BF16), preferred_element_type=F32) + b2_ref[...]
        o_ref[...] = y.astype(o_ref.dtype)


def _experts(xs, blk_e, blk_valid, w1, b1, w2, b2, e0):
    n_rows, d = xs.shape
    tm = MOE_TM
    ff = EXPERT_FF
    grid_spec = pltpu.PrefetchScalarGridSpec(
        num_scalar_prefetch=2,
        grid=(n_rows // tm,),
        in_specs=[
            pl.BlockSpec((tm, d), lambda i, be, bv: (i, 0)),
            pl.BlockSpec((None, d, 2 * ff), lambda i, be, bv: (e0 + be[i], 0, 0)),
            pl.BlockSpec((None, 1, 2 * ff), lambda i, be, bv: (be[i], 0, 0)),
            pl.BlockSpec((None, ff, d), lambda i, be, bv: (be[i], 0, 0)),
            pl.BlockSpec((None, 1, d), lambda i, be, bv: (be[i], 0, 0)),
        ],
        out_specs=pl.BlockSpec((tm, d), lambda i, be, bv: (i, 0)),
        scratch_shapes=[pltpu.VMEM((tm, ff), BF16)],
    )
    return pl.pallas_call(
        _expert_kernel,
        grid_spec=grid_spec,
        out_shape=jax.ShapeDtypeStruct((n_rows, d), BF16),
        compiler_params=_cparams(("arbitrary",)),
        name="moe_experts",
    )(blk_e, blk_valid, xs, w1, b1, w2, b2)


def _moe(h2, top_i, w1, b1, w2, b2, e0):
    n_tok, d = h2.shape
    tm = MOE_TM
    n_assign = n_tok * TOP_K
    flat_e = top_i.reshape(n_assign)
    order = jnp.argsort(flat_e).astype(jnp.int32)
    rank = jnp.argsort(order).astype(jnp.int32)
    onehot = flat_e[:, None] == jnp.arange(N_EXPERTS, dtype=flat_e.dtype)[None, :]
    counts = jnp.sum(onehot, axis=0, dtype=jnp.int32)
    padded = (counts + tm - 1) // tm * tm
    pad_end = jnp.cumsum(padded)
    start = jnp.cumsum(counts) - counts
    shift = (pad_end - padded) - start
    pos = rank + jnp.sum(jnp.where(onehot, shift[None, :], 0), axis=1)
    n_blocks = -(-n_assign // tm) + N_EXPERTS
    blk_start = jnp.arange(n_blocks, dtype=jnp.int32) * tm
    blk_valid = (blk_start < pad_end[-1]).astype(jnp.int32)
    blk_e = jnp.sum(blk_start[:, None] >= pad_end[None, :], axis=1, dtype=jnp.int32)
    last_e = jnp.sum(pad_end[-1] - 1 >= pad_end, dtype=jnp.int32)
    blk_e = jnp.where(blk_valid > 0, blk_e, last_e)
    row = blk_start[:, None] + jnp.arange(tm, dtype=jnp.int32)[None, :]
    src = row - shift[blk_e][:, None]
    lo = start[blk_e][:, None]
    live = (src >= lo) & (src < lo + counts[blk_e][:, None]) & (blk_valid[:, None] > 0)
    row_tok = jnp.where(live, order[jnp.clip(src, 0, n_assign - 1)] // TOP_K, row % n_tok).reshape(n_blocks * tm)
    xs = h2[row_tok]
    y = _experts(xs, blk_e, blk_valid, w1, b1, w2, b2, e0)
    return y[pos.reshape(n_tok, TOP_K).T.reshape(n_assign)].reshape(TOP_K, n_tok, d)


def _combine_kernel(y_ref, w_ref, x_ref, gl_ref, gc_ref, nf_ref, *rest, tm, seq, final_norm):
    o_ref = rest[-1]
    w = w_ref[...]
    acc = y_ref[0].astype(F32) * w[:, 0:1]
    for k in range(1, TOP_K):
        acc = acc + y_ref[k].astype(F32) * w[:, k:k + 1]
    row = pl.program_id(1) * tm + lax.broadcasted_iota(jnp.int32, (tm, 1), 0)
    gate = jnp.where(row >= seq, gc_ref[...], gl_ref[...])
    out = x_ref[...] + gate * acc
    if final_norm:
        out = out * lax.rsqrt(jnp.mean(out * out, axis=-1, keepdims=True) + RMS_EPS) * nf_ref[...]
    o_ref[...] = out


def _moe_combine(yk, top_w, xu, gate_l, gate_c, seq, b0, prev, norm_f, final_norm):
    bsz, rows, d = xu.shape
    bp = yk.shape[1]
    tm = _pick(rows, (512, 384, 256, 128))
    in_specs = [
        pl.BlockSpec((TOP_K, None, tm, d), lambda b, i: (0, b, i, 0)),
        pl.BlockSpec((None, tm, LANES), lambda b, i: (b + b0, i, 0)),
        pl.BlockSpec((None, tm, d), lambda b, i: (b + b0, i, 0)),
        pl.BlockSpec((None, 1, d), lambda b, i: (b + b0, 0, 0)),
        pl.BlockSpec((1, d), lambda b, i: (0, 0)),
        pl.BlockSpec((1, d), lambda b, i: (0, 0)),
    ]
    args = [yk, top_w, xu, gate_l, gate_c, norm_f.reshape(1, d)]
    aliases = {}
    if prev is not None:
        in_specs.append(pl.BlockSpec(memory_space=pl.ANY))
        args.append(prev)
        aliases = {len(args) - 1: 0}
    return pl.pallas_call(
        functools.partial(_combine_kernel, tm=tm, seq=seq, final_norm=final_norm),
        grid=(bp, rows // tm),
        in_specs=in_specs,
        out_specs=pl.BlockSpec((None, tm, d), lambda b, i: (b + b0, i, 0)),
        out_shape=jax.ShapeDtypeStruct((bsz, rows, d), F32),
        input_output_aliases=aliases,
        compiler_params=_cparams(("parallel", "parallel")),
        name="moe_combine",
    )(*args)


def _final_kernel(x_ref, g_ref, o_ref):
    x = x_ref[...]
    o_ref[...] = x * lax.rsqrt(jnp.mean(x * x, axis=-1, keepdims=True) + RMS_EPS) * g_ref[...]


def _final_norm(xu, g, seq):
    bsz, t, d = xu.shape
    tm = _pick(seq, (512, 256, 128))
    return pl.pallas_call(
        _final_kernel,
        grid=(bsz, seq // tm),
        in_specs=[pl.BlockSpec((None, tm, d), lambda b, i: (b, i, 0)),
                  pl.BlockSpec((1, d), lambda b, i: (0, 0))],
        out_specs=pl.BlockSpec((None, tm, d), lambda b, i: (b, i, 0)),
        out_shape=jax.ShapeDtypeStruct((bsz, seq, d), F32),
        compiler_params=_cparams(("parallel", "parallel")),
        name="final_norm",
    )(xu, g.reshape(1, d))


def _layer(xu, mod_l, mod_c, seq, layer_idx, ctx_out, cos_t, sin_t, norm1, w_in, da_lambda, da_subln,
           gm_ln_g, gm_ln_b, gm_ws, gm_bs, dn_conv, dn_a_log, dn_dt_bias, dn_norm, b_gate, w_branch,
           w_out, norm2, w_router, b_router, w1_all, b_e1, w_e2, b_e2, norm_f):
    bsz, t, d = xu.shape
    rows = t if ctx_out else seq
    lam_init = 0.8 - 0.6 * math.exp(-0.3 * layer_idx)
    ml = [mod_l[:, k:k + 1, :] for k in range(6)]
    mc = [mod_c[k:k + 1, :] for k in range(6)]

    w_main = jnp.concatenate([w_in[:, :OFF_SMALL], w_in[:, OFF_GATE:]], axis=1).astype(BF16)
    w_small = jnp.zeros((d, LANES), BF16).at[:, :OFF_GATE - OFF_SMALL].set(
        w_in[:, OFF_SMALL:OFF_GATE].astype(BF16))
    p, small = _in_proj(xu, norm1.reshape(1, d), ml[0], ml[1], mc[0], mc[1], w_main, w_small, cos_t, sin_t, seq)

    ya = _diff_attention(p, da_lambda, da_subln, seq, rows, lam_init)
    yg = _spatial_gating(p, gm_ln_g, gm_ln_b, gm_ws, gm_bs, rows)
    yd = _gated_deltanet(p, small, dn_conv, dn_a_log, dn_dt_bias, dn_norm, seq)
    z = _merge(ya, yg, yd, p, w_branch.astype(BF16), b_gate, rows)
    xu = _out_proj_residual(z, w_out.astype(BF16), xu, ml[2], mc[2], seq, rows)

    h2, top_i, top_w = _adaln_router(xu, norm2.reshape(1, d), ml[3], ml[4], mc[3], mc[4],
                                     w_router, b_router, seq, rows)
    parts = MOE_PARTS if bsz % MOE_PARTS == 0 else 1
    bp = bsz // parts
    b1p, w2p, b2p = _regroup_bias(b_e1), w_e2, b_e2[:, None, :]
    yks = [_moe(h2[i * bp:(i + 1) * bp].reshape(bp * rows, d),
                top_i[i * bp:(i + 1) * bp].reshape(bp * rows, LANES)[:, :TOP_K],
                w1_all, b1p, w2p, b2p, layer_idx * N_EXPERTS).reshape(TOP_K, bp, rows, d) for i in range(parts)]
    out = None
    for i in range(parts):
        out = _moe_combine(yks[i], top_w, xu, ml[5], mc[5], seq, i * bp, out, norm_f, not ctx_out)
    return out


def kernel(x, c, ctx, c_ctx, w_mod, b_mod, norm1, w_in, da_lambda, da_subln, gm_ln_g, gm_ln_b, gm_ws, gm_bs,
           dn_conv, dn_a_log, dn_dt_bias, dn_norm, b_gate, w_branch, w_out, norm2, w_router, b_router,
           w_e1, b_e1, w_e2, b_e2, norm_f):
    bsz, seq, d = x.shape
    n_ctx = ctx.shape[1]
    t = seq + n_ctx
    depth = w_mod.shape[0]
    xu = jnp.concatenate([x, ctx], axis=1)
    r = -(-(bsz + 1) // 8) * 8
    cond = jnp.zeros((r, d), F32).at[:bsz].set(c).at[bsz].set(c_ctx)
    mod = _modulation(cond, w_mod, b_mod).reshape(depth, r, 6, d)
    cos_t, sin_t = _rope_tables(seq, t)
    w1_all = _w1_prep(w_e1)
    for l in range(depth):
        xu = _layer(xu, mod[l, :bsz], mod[l, bsz], seq, l, l < depth - 1, cos_t, sin_t, norm1[l], w_in[l],
                    da_lambda[l], da_subln[l], gm_ln_g[l], gm_ln_b[l], gm_ws[l], gm_bs[l], dn_conv[l],
                    dn_a_log[l], dn_dt_bias[l], dn_norm[l], b_gate[l], w_branch[l], w_out[l], norm2[l],
                    w_router[l], b_router[l], w1_all, b_e1[l], w_e2[l], b_e2[l], norm_f)
    return xu
```
